```python
import math
import jax, jax.numpy as jnp
from jax import lax
import numpy as np


D_MODEL = 1024
BATCH = 4
SEQ = 8192
DEPTH = 2

ATTN_HEADS = 4
ATTN_QK_DIM = 64
ATTN_V_DIM = 2 * ATTN_QK_DIM
ATTN_WIDTH = ATTN_HEADS * ATTN_V_DIM
QK_COLS = ATTN_HEADS * ATTN_QK_DIM
Q_BLOCK = 128
CONV_WIDTH = D_MODEL // 4
CONV_KERNEL = 31
SSM_WIDTH = D_MODEL // 4
SSM_GROUP = 16
SSM_GROUPS = SSM_WIDTH // SSM_GROUP
SSM_STATE = 64
MIX_WIDTH = ATTN_WIDTH + CONV_WIDTH + SSM_WIDTH
IN_COLS = 4 * QK_COLS + ATTN_WIDTH + 2 * CONV_WIDTH + SSM_WIDTH
REL_BUCKETS = 32
REL_MAX_EXACT = REL_BUCKETS // 2
REL_MAX_DIST = 128
N_EXPERT_GROUPS = 4
EXPERTS_PER_GROUP = 8
N_EXPERTS = N_EXPERT_GROUPS * EXPERTS_PER_GROUP
TOP_K_IN_GROUP = 2
EXPERT_FF = D_MODEL // 2
MOE_BLOCK = 256
RMS_EPS = 1e-6
LN_EPS = 1e-5

kernel_name = "hymba_diffattn_conformer_s5_hmoe"


def rms_norm(x, g):
    xf = x.astype(jnp.float32)
    y = xf * lax.rsqrt(jnp.mean(xf * xf, axis=-1, keepdims=True) + RMS_EPS)
    return (y * g.astype(jnp.float32)).astype(x.dtype)


def layer_norm(x, g, b):
    xf = x.astype(jnp.float32)
    mu = jnp.mean(xf, axis=-1, keepdims=True)
    var = jnp.mean(jnp.square(xf - mu), axis=-1, keepdims=True)
    y = (xf - mu) * lax.rsqrt(var + LN_EPS)
    return (y * g.astype(jnp.float32) + b.astype(jnp.float32)).astype(x.dtype)


def rel_bucket(rel):
    n = jnp.maximum(rel, 0)
    nf = jnp.maximum(n, 1).astype(jnp.float32)
    large = REL_MAX_EXACT + (jnp.log(nf / REL_MAX_EXACT) / math.log(REL_MAX_DIST / REL_MAX_EXACT)
                             * (REL_BUCKETS - REL_MAX_EXACT)).astype(jnp.int32)
    large = jnp.minimum(large, REL_BUCKETS - 1)
    return jnp.where(n < REL_MAX_EXACT, n, large)


def diff_attention(q1, q2, k1, k2, v, lam, rel_table):
    B, L, H, dk = q1.shape
    n_blk = L // Q_BLOCK
    scale = dk ** -0.5
    kpos = jnp.arange(L)

    def block(i):
        start = i * Q_BLOCK
        q1b = lax.dynamic_slice_in_dim(q1, start, Q_BLOCK, axis=1)
        q2b = lax.dynamic_slice_in_dim(q2, start, Q_BLOCK, axis=1)
        rel = (start + jnp.arange(Q_BLOCK))[:, None] - kpos[None, :]
        bias = jnp.transpose(rel_table[rel_bucket(rel)], (2, 0, 1)).astype(jnp.float32)
        causal = rel >= 0

        def probs(qb, k):
            s = jnp.einsum('bqhd,bkhd->bhqk', qb, k).astype(jnp.float32) * scale + bias
            return jax.nn.softmax(jnp.where(causal, s, -1e30), axis=-1)

        p = probs(q1b, k1) - lam * probs(q2b, k2)
        return jnp.einsum('bhqk,bkhd->bqhd', p.astype(v.dtype), v)

    out = lax.map(block, jnp.arange(n_blk))
    return jnp.moveaxis(out, 0, 1).reshape(B, L, H, v.shape[-1])


def conformer_conv(u, conv_w, conv_b, ln_g, ln_b):
    a, gate = jnp.split(u, 2, axis=-1)
    h = a * jax.nn.sigmoid(gate)
    h = lax.conv_general_dilated(h, conv_w[:, None, :], window_strides=(1,),
                                 padding=[(CONV_KERNEL - 1, 0)],
                                 dimension_numbers=('NWC', 'WIO', 'NWC'),
                                 feature_group_count=CONV_WIDTH) + conv_b
    h = layer_norm(h, ln_g, ln_b)
    return jax.nn.silu(h)


def _ssm_combine(e1, e2):
    ar1, ai1, br1, bi1 = e1
    ar2, ai2, br2, bi2 = e2
    ar = ar2 * ar1 - ai2 * ai1
    ai = ar2 * ai1 + ai2 * ar1
    br = ar2 * br1 - ai2 * bi1 + br2
    bi = ar2 * bi1 + ai2 * br1 + bi2
    return (ar, ai, br, bi)


def s5_ssm(u, lam_re, lam_im, log_dt, b_re, b_im, c_re, c_im, d, glu_w, glu_b):
    f32 = jnp.float32
    Bsz, L, _ = u.shape
    uf = u.astype(f32).reshape(Bsz, L, SSM_GROUPS, SSM_GROUP)
    dt = jnp.exp(log_dt.astype(f32))[:, None]
    lr, li = lam_re.astype(f32), lam_im.astype(f32)
    mag = jnp.exp(lr * dt)
    ar, ai = mag * jnp.cos(li * dt), mag * jnp.sin(li * dt)
    den = lr * lr + li * li
    zr = ((ar - 1.0) * lr + ai * li) / den
    zi = (ai * lr - (ar - 1.0) * li) / den
    bre, bim = b_re.astype(f32), b_im.astype(f32)
    bbr = zr[..., None] * bre - zi[..., None] * bim
    bbi = zr[..., None] * bim + zi[..., None] * bre
    bu_re = jnp.einsum('blgh,gph->blgp', uf, bbr)
    bu_im = jnp.einsum('blgh,gph->blgp', uf, bbi)
    a_re = jnp.broadcast_to(ar, (1, L, SSM_GROUPS, SSM_STATE))
    a_im = jnp.broadcast_to(ai, (1, L, SSM_GROUPS, SSM_STATE))
    _, _, x_re, x_im = lax.associative_scan(_ssm_combine, (a_re, a_im, bu_re, bu_im), axis=1)
    y = (jnp.einsum('blgp,ghp->blgh', x_re, c_re.astype(f32))
         - jnp.einsum('blgp,ghp->blgh', x_im, c_im.astype(f32))
         + uf * d.astype(f32).reshape(SSM_GROUPS, SSM_GROUP))
    g = jax.nn.gelu(y.reshape(Bsz, L, SSM_WIDTH))
    out = g * jax.nn.sigmoid(g @ glu_w.astype(f32) + glu_b.astype(f32))
    return out.astype(u.dtype)


def hierarchical_moe(h, gw, gb, ew, eb, w_gate, w_up, w_down):
    f32 = jnp.float32
    T, D = h.shape
    g_logits = (h @ gw).astype(f32) + gb.astype(f32)
    g_prob = jax.nn.softmax(g_logits, axis=-1)
    g_idx = jnp.argmax(g_logits, axis=-1)
    g_p = jnp.take_along_axis(g_prob, g_idx[:, None], axis=1)
    e_logits = ((h @ ew).astype(f32) + eb.astype(f32)).reshape(T, N_EXPERT_GROUPS, EXPERTS_PER_GROUP)
    e_logits = jnp.take_along_axis(e_logits, g_idx[:, None, None], axis=1)[:, 0]
    top_v, top_i = lax.top_k(e_logits, TOP_K_IN_GROUP)
    gate = g_p * jax.nn.softmax(top_v, axis=-1)
    expert = g_idx[:, None].astype(jnp.int32) * EXPERTS_PER_GROUP + top_i.astype(jnp.int32)
    S = T * TOP_K_IN_GROUP
    flat_e = expert.reshape(S)
    flat_tok = jnp.repeat(jnp.arange(T, dtype=jnp.int32), TOP_K_IN_GROUP)
    flat_w = gate.reshape(S)
    order = jnp.argsort(flat_e)
    se, st, sw = flat_e[order], flat_tok[order], flat_w[order]
    counts = jnp.bincount(flat_e, length=N_EXPERTS)
    padded = (counts + MOE_BLOCK - 1) // MOE_BLOCK * MOE_BLOCK
    start = jnp.cumsum(counts) - counts
    pend = jnp.cumsum(padded)
    pstart = pend - padded
    dest = pstart[se] + jnp.arange(S) - start[se]
    n_blocks = -(-S // MOE_BLOCK) + N_EXPERTS
    P = n_blocks * MOE_BLOCK
    tok_pad = jnp.full((P,), T, jnp.int32).at[dest].set(st)
    w_pad = jnp.zeros((P,), f32).at[dest].set(sw)
    blk_e = jnp.minimum(jnp.searchsorted(pend, jnp.arange(n_blocks) * MOE_BLOCK, side='right'),
                        N_EXPERTS - 1)
    h_ext = jnp.concatenate([h, jnp.zeros((1, D), h.dtype)], axis=0)

    def expert_block(args):
        tok, e = args
        xb = h_ext[tok]
        act = jax.nn.silu(xb @ w_gate[e]) * (xb @ w_up[e])
        return act @ w_down[e]

    yb = lax.map(expert_block, (tok_pad.reshape(n_blocks, MOE_BLOCK), blk_e))
    yb = yb.reshape(P, D) * w_pad[:, None].astype(h.dtype)
    return jnp.zeros((T + 1, D), h.dtype).at[tok_pad].add(yb)[:T]


def setup_inputs(seed: int = 0) -> dict:
    key = jax.random.key(seed)
    ks = iter(jax.random.split(key, 48))
    f32 = jnp.float32

    def nrm(shape, scale):
        return scale * jax.random.normal(next(ks), shape, f32)

    G, P, H = SSM_GROUPS, SSM_STATE, SSM_GROUP
    n_idx = jnp.arange(P, dtype=f32)
    return {
        "x": nrm((BATCH, SEQ, D_MODEL), 1.0),
        "rel_bias": nrm((REL_BUCKETS, ATTN_HEADS), 0.1),
        "ln1_g": 1.0 + nrm((DEPTH, D_MODEL), 0.02),
        "w_in": nrm((DEPTH, D_MODEL, IN_COLS), D_MODEL ** -0.5),
        "lam_q1": nrm((DEPTH, ATTN_QK_DIM), 0.1),
        "lam_k1": nrm((DEPTH, ATTN_QK_DIM), 0.1),
        "lam_q2": nrm((DEPTH, ATTN_QK_DIM), 0.1),
        "lam_k2": nrm((DEPTH, ATTN_QK_DIM), 0.1),
        "subln_g": 1.0 + nrm((DEPTH, ATTN_V_DIM), 0.02),
        "conv_w": nrm((DEPTH, CONV_KERNEL, CONV_WIDTH), CONV_KERNEL ** -0.5),
        "conv_b": nrm((DEPTH, CONV_WIDTH), 0.02),
        "conv_ln_g": 1.0 + nrm((DEPTH, CONV_WIDTH), 0.02),
        "conv_ln_b": nrm((DEPTH, CONV_WIDTH), 0.02),
        "ssm_lam_re": -0.5 + nrm((DEPTH, G, P), 0.01),
        "ssm_lam_im": math.pi * n_idx + nrm((DEPTH, G, P), 0.01),
        "ssm_log_dt": jax.random.uniform(next(ks), (DEPTH, G), f32, math.log(1e-3), math.log(1e-1)),
        "ssm_b_re": nrm((DEPTH, G, P, H), (2 * H) ** -0.5),
        "ssm_b_im": nrm((DEPTH, G, P, H), (2 * H) ** -0.5),
        "ssm_c_re": nrm((DEPTH, G, H, P), P ** -0.5),
        "ssm_c_im": nrm((DEPTH, G, H, P), P ** -0.5),
        "ssm_d": nrm((DEPTH, SSM_WIDTH), 1.0),
        "ssm_glu_w": nrm((DEPTH, SSM_WIDTH, SSM_WIDTH), SSM_WIDTH ** -0.5),
        "ssm_glu_b": nrm((DEPTH, SSM_WIDTH), 0.02),
        "w_out": nrm((DEPTH, MIX_WIDTH, D_MODEL), MIX_WIDTH ** -0.5),
        "ln2_g": 1.0 + nrm((DEPTH, D_MODEL), 0.02),
        "group_router_w": nrm((DEPTH, D_MODEL, N_EXPERT_GROUPS), D_MODEL ** -0.5),
        "group_router_b": nrm((DEPTH, N_EXPERT_GROUPS), 0.01),
        "expert_router_w": nrm((DEPTH, D_MODEL, N_EXPERTS), D_MODEL ** -0.5),
        "expert_router_b": nrm((DEPTH, N_EXPERTS), 0.01),
        "w_gate": nrm((DEPTH, N_EXPERTS, D_MODEL, EXPERT_FF), D_MODEL ** -0.5),
        "w_up": nrm((DEPTH, N_EXPERTS, D_MODEL, EXPERT_FF), D_MODEL ** -0.5),
        "w_down": nrm((DEPTH, N_EXPERTS, EXPERT_FF, D_MODEL), EXPERT_FF ** -0.5),
        "final_g": 1.0 + nrm((D_MODEL,), 0.02),
    }


def reference(x, rel_bias, ln1_g, w_in, lam_q1, lam_k1, lam_q2, lam_k2, subln_g,
              conv_w, conv_b, conv_ln_g, conv_ln_b,
              ssm_lam_re, ssm_lam_im, ssm_log_dt, ssm_b_re, ssm_b_im, ssm_c_re, ssm_c_im,
              ssm_d, ssm_glu_w, ssm_glu_b, w_out, ln2_g,
              group_router_w, group_router_b, expert_router_w, expert_router_b,
              w_gate, w_up, w_down, final_g):
    B, L, D = x.shape
    splits = np.cumsum([QK_COLS, QK_COLS, QK_COLS, QK_COLS, ATTN_WIDTH, 2 * CONV_WIDTH]).tolist()
    for l in range(DEPTH):
        h = rms_norm(x, ln1_g[l])
        proj = h @ w_in[l]
        q1, q2, k1, k2, v, conv_in, ssm_in = jnp.split(proj, splits, axis=-1)
        q1, q2, k1, k2 = (t.reshape(B, L, ATTN_HEADS, ATTN_QK_DIM) for t in (q1, q2, k1, k2))
        v = v.reshape(B, L, ATTN_HEADS, ATTN_V_DIM)
        lam_init = 0.8 - 0.6 * math.exp(-0.3 * l)
        lam = (jnp.exp(jnp.sum(lam_q1[l].astype(jnp.float32) * lam_k1[l].astype(jnp.float32)))
               - jnp.exp(jnp.sum(lam_q2[l].astype(jnp.float32) * lam_k2[l].astype(jnp.float32)))
               + lam_init)
        a = diff_attention(q1, q2, k1, k2, v, lam, rel_bias)
        a = (rms_norm(a, subln_g[l]) * (1.0 - lam_init)).reshape(B, L, ATTN_WIDTH)
        c = conformer_conv(conv_in, conv_w[l], conv_b[l], conv_ln_g[l], conv_ln_b[l])
        s = s5_ssm(ssm_in, ssm_lam_re[l], ssm_lam_im[l], ssm_log_dt[l], ssm_b_re[l], ssm_b_im[l],
                   ssm_c_re[l], ssm_c_im[l], ssm_d[l], ssm_glu_w[l], ssm_glu_b[l])
        x = x + jnp.concatenate([a, c, s], axis=-1) @ w_out[l]
        h2 = rms_norm(x, ln2_g[l]).reshape(B * L, D)
        x = x + hierarchical_moe(h2, group_router_w[l], group_router_b[l], expert_router_w[l],
                                 expert_router_b[l], w_gate[l], w_up[l], w_down[l]).reshape(B, L, D)
    return rms_norm(x, final_g)
```

```python
import functools
import math

import jax
import jax.numpy as jnp
from jax import lax
from jax.experimental import pallas as pl
from jax.experimental.pallas import tpu as pltpu

F32 = jnp.float32
BF16 = jnp.bfloat16

D_MODEL = 1024
N_HEADS = 4
QK_DIM = 64
V_DIM = 128
ATTN_WIDTH = N_HEADS * V_DIM
QK_COLS = N_HEADS * QK_DIM
CONV_WIDTH = 256
CONV_TAPS = 31
SSM_WIDTH = 256
SSM_GROUP = 16
SSM_GROUPS = 16
SSM_STATE = 64
SSM_LANES = SSM_GROUPS * SSM_STATE
REL_BUCKETS = 32
REL_MAX_EXACT = 16
REL_MAX_DIST = 128
N_GROUPS = 4
EXPERTS_PER_GROUP = 8
N_EXPERTS = N_GROUPS * EXPERTS_PER_GROUP
EXPERT_FF = 512
RMS_EPS = 1e-6
LN_EPS = 1e-5
NEG_BIG = -1e30

LANES = 128
SUBLANES = 8
VMEM_LIMIT = 48 * 1024 * 1024

PROJ_ROWS = 512
ATTN_TILE = 256
CONV_ROWS = 512
CONV_HALO = 32
SSM_ROWS = 256
MOE_BLOCK = 512
COMBINE_ROWS = 256


def _params(sem):
    return pltpu.CompilerParams(dimension_semantics=sem, vmem_limit_bytes=VMEM_LIMIT)


def _inproj_kernel(x_ref, g_ref, w_ref, qq_ref, kk_ref, v_ref, conv_ref, ssm_ref):
    x = x_ref[...]
    ms = jnp.mean(x * x, axis=-1, keepdims=True)
    h = (x * lax.rsqrt(ms + RMS_EPS) * g_ref[...]).astype(BF16)
    o = 0
    for ref in (qq_ref, kk_ref, v_ref, conv_ref, ssm_ref):
        n = ref.shape[-1]
        ref[...] = jnp.dot(h, w_ref[:, o:o + n], preferred_element_type=F32).astype(ref.dtype)
        o += n


def _inproj(x2, g, w):
    T = x2.shape[0]
    tm = PROJ_ROWS
    widths = (ATTN_WIDTH, ATTN_WIDTH, ATTN_WIDTH, 2 * CONV_WIDTH, SSM_WIDTH)
    dtypes = (BF16, BF16, BF16, F32, F32)
    return pl.pallas_call(
        _inproj_kernel,
        grid=(T // tm,),
        in_specs=[
            pl.BlockSpec((tm, D_MODEL), lambda i: (i, 0)),
            pl.BlockSpec((1, D_MODEL), lambda i: (0, 0)),
            pl.BlockSpec(w.shape, lambda i: (0, 0)),
        ],
        out_specs=[pl.BlockSpec((tm, n), lambda i: (i, 0)) for n in widths],
        out_shape=[jax.ShapeDtypeStruct((T, n), dt) for n, dt in zip(widths, dtypes)],
        compiler_params=_params(("arbitrary",)),
        name="inproj",
    )(x2, g, w)


def _attn_kernel(lam_ref, q_ref, k_ref, v_ref, bd_ref, bs_ref, g_ref, o_ref,
                 qs_ref, m_ref, l_ref, acc_ref, *, out_scale):
    t = ATTN_TILE
    qi = pl.program_id(2)
    q = q_ref[...].astype(F32)
    lane = lax.broadcasted_iota(jnp.int32, q.shape, 1)
    qs_ref[0:t, :] = jnp.where(lane < QK_DIM, q, 0.0).astype(BF16)
    qs_ref[t:2 * t, :] = jnp.where(lane >= QK_DIM, q, 0.0).astype(BF16)
    m_ref[...] = jnp.full(m_ref.shape, NEG_BIG, F32)
    l_ref[...] = jnp.zeros(l_ref.shape, F32)
    acc_ref[...] = jnp.zeros(acc_ref.shape, F32)

    def step(j, bias):
        start = pl.multiple_of(j * t, t)
        k = k_ref[pl.ds(start, t), :]
        v = v_ref[pl.ds(start, t), :]
        s = lax.dot_general(qs_ref[...], k, (((1,), (1,)), ((), ())), preferred_element_type=F32)
        if bias is not None:
            s = s + bias
        m_prev = m_ref[...]
        m_new = jnp.maximum(m_prev, jnp.max(s, axis=-1, keepdims=True))
        alpha = jnp.exp(m_prev - m_new)
        p = jnp.exp(s - m_new)
        l_ref[...] = alpha * l_ref[...] + jnp.sum(p, axis=-1, keepdims=True)
        acc_ref[...] = alpha * acc_ref[...] + jnp.dot(p.astype(BF16), v, preferred_element_type=F32)
        m_ref[...] = m_new

    def far(j, carry):
        step(j, None)
        return carry

    lax.fori_loop(0, jnp.maximum(qi - 1, 0), far, 0)

    @pl.when(qi >= 1)
    def _():
        step(qi - 1, bs_ref[0])

    step(qi, bd_ref[0])

    acc = acc_ref[...]
    l = l_ref[...]
    a = acc[0:t] / l[0:t] - lam_ref[0] * (acc[t:2 * t] / l[t:2 * t])
    ms = jnp.mean(a * a, axis=-1, keepdims=True)
    o_ref[...] = (a * lax.rsqrt(ms + RMS_EPS) * g_ref[...] * out_scale).astype(o_ref.dtype)


def _attention(qq, kk, v, lam, bias_diag, bias_sub, subln_g, out_scale, B, L):
    T = B * L
    t = ATTN_TILE
    nq = L // t
    return pl.pallas_call(
        functools.partial(_attn_kernel, out_scale=out_scale),
        grid=(B, N_HEADS, nq),
        in_specs=[
            pl.BlockSpec(memory_space=pltpu.SMEM),
            pl.BlockSpec((t, LANES), lambda b, h, i: (b * nq + i, h)),
            pl.BlockSpec((L, LANES), lambda b, h, i: (b, h)),
            pl.BlockSpec((L, LANES), lambda b, h, i: (b, h)),
            pl.BlockSpec((1, 2 * t, t), lambda b, h, i: (h, 0, 0)),
            pl.BlockSpec((1, 2 * t, t), lambda b, h, i: (h, 0, 0)),
            pl.BlockSpec((1, V_DIM), lambda b, h, i: (0, 0)),
        ],
        out_specs=pl.BlockSpec((t, LANES), lambda b, h, i: (b * nq + i, h)),
        out_shape=jax.ShapeDtypeStruct((T, ATTN_WIDTH), BF16),
        scratch_shapes=[
            pltpu.VMEM((2 * t, LANES), BF16),
            pltpu.VMEM((2 * t, 1), F32),
            pltpu.VMEM((2 * t, 1), F32),
            pltpu.VMEM((2 * t, V_DIM), F32),
        ],
        compiler_params=_params(("arbitrary", "arbitrary", "arbitrary")),
        name="diff_attn",
    )(lam, qq, kk, v, bias_diag, bias_sub, subln_g)


def _rel_bucket(rel):
    n = jnp.maximum(rel, 0)
    nf = jnp.maximum(n, 1).astype(F32)
    large = REL_MAX_EXACT + (jnp.log(nf / REL_MAX_EXACT) / math.log(REL_MAX_DIST / REL_MAX_EXACT)
                             * (REL_BUCKETS - REL_MAX_EXACT)).astype(jnp.int32)
    large = jnp.minimum(large, REL_BUCKETS - 1)
    return jnp.where(n < REL_MAX_EXACT, n, large)


def _bias_tiles(rel_table):
    t = ATTN_TILE
    assert t >= REL_MAX_DIST
    far = rel_table[REL_BUCKETS - 1].astype(F32)
    i = jnp.arange(t)[:, None]
    j = jnp.arange(t)[None, :]
    rel_d = i - j
    bd = jnp.transpose(rel_table[_rel_bucket(rel_d)].astype(F32) - far, (2, 0, 1))
    bd = jnp.where((rel_d >= 0)[None], bd, NEG_BIG)
    bs = jnp.transpose(rel_table[_rel_bucket(rel_d + t)].astype(F32) - far, (2, 0, 1))
    return jnp.concatenate([bd, bd], axis=1), jnp.concatenate([bs, bs], axis=1)


def _conv_kernel(u_ref, w_ref, b_ref, g_ref, beta_ref, o_ref, h_ref):
    tt = CONV_ROWS
    j = pl.program_id(1)

    @pl.when(j == 0)
    def _():
        h_ref[0:CONV_HALO, :] = jnp.zeros((CONV_HALO, CONV_WIDTH), F32)

    @pl.when(j > 0)
    def _():
        h_ref[0:CONV_HALO, :] = h_ref[tt:tt + CONV_HALO, :]

    u = u_ref[...]
    h_ref[CONV_HALO:CONV_HALO + tt, :] = u[:, 0:CONV_WIDTH] * jax.nn.sigmoid(u[:, CONV_WIDTH:])
    acc = jnp.broadcast_to(b_ref[...], (tt, CONV_WIDTH))
    off = CONV_HALO - (CONV_TAPS - 1)
    for k in range(CONV_TAPS):
        acc = acc + w_ref[k:k + 1, :] * h_ref[off + k:off + k + tt, :]
    mu = jnp.mean(acc, axis=-1, keepdims=True)
    cen = acc - mu
    var = jnp.mean(cen * cen, axis=-1, keepdims=True)
    y = cen * lax.rsqrt(var + LN_EPS) * g_ref[...] + beta_ref[...]
    o_ref[...] = jax.nn.silu(y).astype(o_ref.dtype)


def _conformer_conv(conv_in, w, b, g, beta, B, L):
    T = B * L
    tt = CONV_ROWS
    nt = L // tt
    vec = pl.BlockSpec((1, CONV_WIDTH), lambda bb, j: (0, 0))
    return pl.pallas_call(
        _conv_kernel,
        grid=(B, nt),
        in_specs=[
            pl.BlockSpec((tt, 2 * CONV_WIDTH), lambda bb, j: (bb * nt + j, 0)),
            pl.BlockSpec((CONV_TAPS, CONV_WIDTH), lambda bb, j: (0, 0)),
            vec, vec, vec,
        ],
        out_specs=pl.BlockSpec((tt, CONV_WIDTH), lambda bb, j: (bb * nt + j, 0)),
        out_shape=jax.ShapeDtypeStruct((T, CONV_WIDTH), BF16),
        scratch_shapes=[pltpu.VMEM((tt + CONV_HALO, CONV_WIDTH), F32)],
        compiler_params=_params(("arbitrary", "arbitrary")),
        name="conformer_conv",
    )(conv_in, w, b, g, beta)


def _ssm_kernel(u_ref, wb_ref, apr_ref, api_ref, wc_ref, d_ref, gw_ref, gb_ref, o_ref,
                xs_ref, carry_ref):
    tt = SSM_ROWS
    n = SSM_LANES
    j = pl.program_id(1)

    @pl.when(j == 0)
    def _():
        carry_ref[...] = jnp.zeros(carry_ref.shape, F32)

    u = u_ref[...]
    xs_ref[...] = jnp.dot(u.astype(BF16), wb_ref[...], preferred_element_type=F32)
    apr = apr_ref[...]
    api = api_ref[...]
    row = lax.broadcasted_iota(jnp.int32, (SUBLANES, n), 0)

    def block(r, carry):
        cr, ci = carry
        start = pl.multiple_of(r * SUBLANES, SUBLANES)
        xr = xs_ref[pl.ds(start, SUBLANES), 0:n]
        xi = xs_ref[pl.ds(start, SUBLANES), n:2 * n]
        for shift in (1, 2, 4):
            ar = apr[shift - 1:shift, :]
            ai = api[shift - 1:shift, :]
            sr = jnp.where(row >= shift, pltpu.roll(xr, shift, 0), 0.0)
            si = jnp.where(row >= shift, pltpu.roll(xi, shift, 0), 0.0)
            xr, xi = xr + (ar * sr - ai * si), xi + (ar * si + ai * sr)
        xr, xi = xr + (apr * cr - api * ci), xi + (apr * ci + api * cr)
        xs_ref[pl.ds(start, SUBLANES), 0:n] = xr
        xs_ref[pl.ds(start, SUBLANES), n:2 * n] = xi
        return xr[SUBLANES - 1:SUBLANES, :], xi[SUBLANES - 1:SUBLANES, :]

    cr, ci = lax.fori_loop(0, tt // SUBLANES, block, (carry_ref[0:1, :], carry_ref[1:2, :]))
    carry_ref[0:1, :] = cr
    carry_ref[1:2, :] = ci

    y = jnp.dot(xs_ref[...].astype(BF16), wc_ref[...], preferred_element_type=F32) + u * d_ref[...]
    g = jax.nn.gelu(y)
    z = jnp.dot(g.astype(BF16), gw_ref[...], preferred_element_type=F32) + gb_ref[...]
    o_ref[...] = (g * jax.nn.sigmoid(z)).astype(o_ref.dtype)


def _s5_ssm(ssm_in, wb, apr, api, wc, d, gw, gb, B, L):
    T = B * L
    tt = SSM_ROWS
    nt = L // tt
    const = lambda a: pl.BlockSpec(a.shape, lambda bb, j: (0, 0))
    return pl.pallas_call(
        _ssm_kernel,
        grid=(B, nt),
        in_specs=[pl.BlockSpec((tt, SSM_WIDTH), lambda bb, j: (bb * nt + j, 0)),
                  const(wb), const(apr), const(api), const(wc), const(d), const(gw), const(gb)],
        out_specs=pl.BlockSpec((tt, SSM_WIDTH), lambda bb, j: (bb * nt + j, 0)),
        out_shape=jax.ShapeDtypeStruct((T, SSM_WIDTH), BF16),
        scratch_shapes=[pltpu.VMEM((tt, 2 * SSM_LANES), F32), pltpu.VMEM((SUBLANES, SSM_LANES), F32)],
        compiler_params=_params(("arbitrary", "arbitrary")),
        name="s5_scan",
    )(ssm_in, wb, apr, api, wc, d, gw, gb)


def _ssm_weights(lam_re, lam_im, log_dt, b_re, b_im, c_re, c_im):
    G, P, H = SSM_GROUPS, SSM_STATE, SSM_GROUP
    dt = jnp.exp(log_dt.astype(F32))[:, None]
    lr, li = lam_re.astype(F32), lam_im.astype(F32)
    mag = jnp.exp(lr * dt)
    ar, ai = mag * jnp.cos(li * dt), mag * jnp.sin(li * dt)
    den = lr * lr + li * li
    zr = ((ar - 1.0) * lr + ai * li) / den
    zi = (ai * lr - (ar - 1.0) * li) / den
    bre, bim = b_re.astype(F32), b_im.astype(F32)
    bbr = zr[..., None] * bre - zi[..., None] * bim
    bbi = zr[..., None] * bim + zi[..., None] * bre
    eye = jnp.eye(G, dtype=F32)
    wb = jnp.concatenate([jnp.einsum('gph,gk->ghkp', bbr, eye).reshape(G * H, G * P),
                          jnp.einsum('gph,gk->ghkp', bbi, eye).reshape(G * H, G * P)], axis=1)
    wc = jnp.concatenate([jnp.einsum('ghp,gk->gpkh', c_re.astype(F32), eye).reshape(G * P, G * H),
                          -jnp.einsum('ghp,gk->gpkh', c_im.astype(F32), eye).reshape(G * P, G * H)], axis=0)
    pr, pi = [ar.reshape(1, G * P)], [ai.reshape(1, G * P)]
    for _ in range(SUBLANES - 1):
        pr, pi = (pr + [pr[-1] * pr[0] - pi[-1] * pi[0]], pi + [pr[-1] * pi[0] + pi[-1] * pr[0]])
    return wb.astype(BF16), jnp.concatenate(pr, axis=0), jnp.concatenate(pi, axis=0), wc.astype(BF16)


def _outproj_kernel(x_ref, a_ref, c_ref, s_ref, w_ref, g_ref, wr_ref, br_ref,
                    x1_ref, h2_ref, route_ref, cnt_ref, carry_ref):
    tm = PROJ_ROWS
    i = pl.program_id(0)

    @pl.when(i == 0)
    def _():
        carry_ref[...] = jnp.zeros(carry_ref.shape, F32)

    o1, o2 = ATTN_WIDTH, ATTN_WIDTH + CONV_WIDTH
    y = (jnp.dot(a_ref[...], w_ref[0:o1, :], preferred_element_type=F32)
         + jnp.dot(c_ref[...], w_ref[o1:o2, :], preferred_element_type=F32)
         + jnp.dot(s_ref[...], w_ref[o2:, :], preferred_element_type=F32))
    x1 = x_ref[...] + y
    x1_ref[...] = x1
    ms = jnp.mean(x1 * x1, axis=-1, keepdims=True)
    h2 = x1 * lax.rsqrt(ms + RMS_EPS) * g_ref[...]
    h2_ref[...] = h2

    logits = jnp.dot(h2.astype(BF16), wr_ref[...], preferred_element_type=F32) + br_ref[...]
    col = lax.broadcasted_iota(jnp.int32, logits.shape, 1)

    def first_max(vals):
        top = jnp.max(vals, axis=-1, keepdims=True)
        return top, jnp.min(jnp.where(vals == top, col, LANES), axis=-1, keepdims=True)

    glog = jnp.where(col < N_GROUPS, logits, NEG_BIG)
    gmax, gidx = first_max(glog)
    gp = 1.0 / jnp.sum(jnp.exp(glog - gmax), axis=-1, keepdims=True)
    lo = N_GROUPS + gidx * EXPERTS_PER_GROUP
    e = jnp.where((col >= lo) & (col < lo + EXPERTS_PER_GROUP), logits, NEG_BIG)
    v1, i1 = first_max(e)
    e = jnp.where(col == i1, NEG_BIG, e)
    v2, i2 = first_max(e)
    ex = jnp.exp(v2 - v1)
    w1 = gp * (1.0 / (1.0 + ex))
    w2 = gp * (ex / (1.0 + ex))

    hit1 = col == i1
    hit2 = col == i2
    onehot = jnp.where(hit1 | hit2, 1.0, 0.0)
    r = lax.broadcasted_iota(jnp.int32, (tm, tm), 0)
    c = lax.broadcasted_iota(jnp.int32, (tm, tm), 1)
    before = jnp.where(r > c, 1.0, 0.0).astype(BF16)
    prior = jnp.dot(before, onehot.astype(BF16), preferred_element_type=F32) + carry_ref[...]
    rank1 = jnp.sum(jnp.where(hit1, prior, 0.0), axis=-1, keepdims=True)
    rank2 = jnp.sum(jnp.where(hit2, prior, 0.0), axis=-1, keepdims=True)
    carry_ref[...] = carry_ref[...] + jnp.sum(onehot, axis=0, keepdims=True)
    cnt_ref[...] = carry_ref[...]

    fields = ((i1 - N_GROUPS).astype(F32), (i2 - N_GROUPS).astype(F32), w1, w2, rank1, rank2)
    route = jnp.zeros(logits.shape, F32)
    for k, val in enumerate(fields):
        route = jnp.where(col == k, val, route)
    route_ref[...] = route


def _outproj_route(x2, a, c, s, w_out, g, wr, br):
    T = x2.shape[0]
    tm = PROJ_ROWS
    rows = lambda n: pl.BlockSpec((tm, n), lambda i: (i, 0))
    const = lambda arr: pl.BlockSpec(arr.shape, lambda i: (0, 0))
    return pl.pallas_call(
        _outproj_kernel,
        grid=(T // tm,),
        in_specs=[rows(D_MODEL), rows(ATTN_WIDTH), rows(CONV_WIDTH), rows(SSM_WIDTH),
                  const(w_out), const(g), const(wr), const(br)],
        out_specs=[rows(D_MODEL), rows(D_MODEL), rows(LANES), pl.BlockSpec((1, LANES), lambda i: (0, 0))],
        out_shape=[jax.ShapeDtypeStruct((T, D_MODEL), F32), jax.ShapeDtypeStruct((T, D_MODEL), F32),
                   jax.ShapeDtypeStruct((T, LANES), F32), jax.ShapeDtypeStruct((1, LANES), F32)],
        scratch_shapes=[pltpu.VMEM((1, LANES), F32)],
        compiler_params=_params(("arbitrary",)),
        name="outproj_route",
    )(x2, a, c, s, w_out, g, wr, br)


def _row_copy(src_hbm, row, dst_ref, slot, sem):
    return pltpu.make_async_copy(src_hbm.at[pl.ds(row, 1), :], dst_ref.at[pl.ds(slot, 1), :], sem)


def _dispatch_kernel(nused_ref, idx_ref, src_hbm, o_ref, sem):
    b = pl.program_id(0)

    @pl.when(b < nused_ref[0])
    def _():
        def issue(r, carry):
            _row_copy(src_hbm, idx_ref[0, 0, r], o_ref, r, sem).start()
            return carry

        def drain(r, carry):
            _row_copy(src_hbm, 0, o_ref, r, sem).wait()
            return carry

        lax.fori_loop(0, MOE_BLOCK, issue, 0)
        lax.fori_loop(0, MOE_BLOCK, drain, 0)

    @pl.when(b >= nused_ref[0])
    def _():
        o_ref[...] = jnp.zeros(o_ref.shape, o_ref.dtype)


def _dispatch(nused, tok_pad, h2, n_blocks):
    return pl.pallas_call(
        _dispatch_kernel,
        grid_spec=pltpu.PrefetchScalarGridSpec(
            num_scalar_prefetch=1,
            grid=(n_blocks,),
            in_specs=[pl.BlockSpec((1, 1, MOE_BLOCK), lambda b, nu: (b, 0, 0), memory_space=pltpu.SMEM),
                      pl.BlockSpec(memory_space=pl.ANY)],
            out_specs=pl.BlockSpec((MOE_BLOCK, D_MODEL), lambda b, nu: (b, 0)),
            scratch_shapes=[pltpu.SemaphoreType.DMA],
        ),
        out_shape=jax.ShapeDtypeStruct((n_blocks * MOE_BLOCK, D_MODEL), F32),
        compiler_params=_params(("arbitrary",)),
        name="moe_dispatch",
    )(nused, tok_pad.reshape(n_blocks, 1, MOE_BLOCK), h2)


def _expert_kernel(blk_e_ref, nused_ref, x_ref, wg_ref, wu_ref, wd_ref, o_ref):
    b = pl.program_id(0)

    @pl.when(b < nused_ref[0])
    def _():
        x = x_ref[...].astype(BF16)
        gate = jnp.dot(x, wg_ref[0], preferred_element_type=F32)
        up = jnp.dot(x, wu_ref[0], preferred_element_type=F32)
        act = (jax.nn.silu(gate) * up).astype(BF16)
        o_ref[...] = jnp.dot(act, wd_ref[0], preferred_element_type=F32)

    @pl.when(b >= nused_ref[0])
    def _():
        o_ref[...] = jnp.zeros(o_ref.shape, o_ref.dtype)


def _experts(blk_e, nused, xs, wg, wu, wd, n_blocks):
    return pl.pallas_call(
        _expert_kernel,
        grid_spec=pltpu.PrefetchScalarGridSpec(
            num_scalar_prefetch=2,
            grid=(n_blocks,),
            in_specs=[pl.BlockSpec((MOE_BLOCK, D_MODEL), lambda b, be, nu: (b, 0)),
                      pl.BlockSpec((1, D_MODEL, EXPERT_FF), lambda b, be, nu: (be[b], 0, 0)),
                      pl.BlockSpec((1, D_MODEL, EXPERT_FF), lambda b, be, nu: (be[b], 0, 0)),
                      pl.BlockSpec((1, EXPERT_FF, D_MODEL), lambda b, be, nu: (be[b], 0, 0))],
            out_specs=pl.BlockSpec((MOE_BLOCK, D_MODEL), lambda b, be, nu: (b, 0)),
        ),
        out_shape=jax.ShapeDtypeStruct((n_blocks * MOE_BLOCK, D_MODEL), F32),
        compiler_params=_params(("arbitrary",)),
        name="moe_experts",
    )(blk_e, nused, xs, wg, wu, wd)


def _combine_kernel(d1_ref, d2_ref, yb_hbm, x_ref, route_ref, g_ref, o_ref, y1_ref, y2_ref, sems,
                    *, final_norm):
    tm = COMBINE_ROWS

    def issue(r, carry):
        _row_copy(yb_hbm, d1_ref[0, 0, r], y1_ref, r, sems.at[0]).start()
        _row_copy(yb_hbm, d2_ref[0, 0, r], y2_ref, r, sems.at[1]).start()
        return carry

    def drain(r, carry):
        _row_copy(yb_hbm, 0, y1_ref, r, sems.at[0]).wait()
        _row_copy(yb_hbm, 0, y2_ref, r, sems.at[1]).wait()
        return carry

    lax.fori_loop(0, tm, issue, 0)
    lax.fori_loop(0, tm, drain, 0)
    route = route_ref[...]
    x = x_ref[...] + (route[:, 2:3] * y1_ref[...] + route[:, 3:4] * y2_ref[...])
    if final_norm:
        ms = jnp.mean(x * x, axis=-1, keepdims=True)
        x = x * lax.rsqrt(ms + RMS_EPS) * g_ref[...]
    o_ref[...] = x


def _combine(dest1, dest2, yb, x1, route, g, final_norm):
    T = x1.shape[0]
    tm = COMBINE_ROWS
    nt = T // tm
    idx = pl.BlockSpec((1, 1, tm), lambda i: (i, 0, 0), memory_space=pltpu.SMEM)
    return pl.pallas_call(
        functools.partial(_combine_kernel, final_norm=final_norm),
        grid=(nt,),
        in_specs=[idx, idx, pl.BlockSpec(memory_space=pl.ANY),
                  pl.BlockSpec((tm, D_MODEL), lambda i: (i, 0)),
                  pl.BlockSpec((tm, LANES), lambda i: (i, 0)),
                  pl.BlockSpec((1, D_MODEL), lambda i: (0, 0))],
        out_specs=pl.BlockSpec((tm, D_MODEL), lambda i: (i, 0)),
        out_shape=jax.ShapeDtypeStruct((T, D_MODEL), F32),
        scratch_shapes=[pltpu.VMEM((tm, D_MODEL), F32), pltpu.VMEM((tm, D_MODEL), F32),
                        pltpu.SemaphoreType.DMA((2,))],
        compiler_params=_params(("arbitrary",)),
        name="moe_combine",
    )(dest1.reshape(nt, 1, tm), dest2.reshape(nt, 1, tm), yb, x1, route, g)


def _routing_plan(route, counts, T):
    e1 = route[:, 0].astype(jnp.int32)
    e2 = route[:, 1].astype(jnp.int32)
    rank1 = route[:, 4].astype(jnp.int32)
    rank2 = route[:, 5].astype(jnp.int32)
    cnt = counts[0, N_GROUPS:N_GROUPS + N_EXPERTS].astype(jnp.int32)
    nblk = (cnt + MOE_BLOCK - 1) // MOE_BLOCK
    blk_end = jnp.cumsum(nblk)
    first_row = (blk_end - nblk) * MOE_BLOCK
    dest1 = first_row[e1] + rank1
    dest2 = first_row[e2] + rank2
    n_blocks = (T * 2) // MOE_BLOCK + N_EXPERTS
    blk_e = jnp.minimum(jnp.sum(jnp.arange(n_blocks)[:, None] >= blk_end[None, :], axis=1),
                        N_EXPERTS - 1).astype(jnp.int32)
    tok = jnp.arange(T, dtype=jnp.int32)
    tok_pad = jnp.zeros((n_blocks * MOE_BLOCK,), jnp.int32).at[dest1].set(tok).at[dest2].set(tok)
    return dest1, dest2, tok_pad, blk_e, blk_end[-1:].astype(jnp.int32), n_blocks


def _pack_w_in(w):
    scale = QK_DIM ** -0.5
    q1, q2, k1, k2 = (w[:, i * QK_COLS:(i + 1) * QK_COLS].reshape(D_MODEL, N_HEADS, QK_DIM) for i in range(4))
    qq = (jnp.concatenate([q1, q2], axis=-1) * scale).reshape(D_MODEL, ATTN_WIDTH)
    kk = jnp.concatenate([k1, k2], axis=-1).reshape(D_MODEL, ATTN_WIDTH)
    return jnp.concatenate([qq, kk, w[:, 4 * QK_COLS:]], axis=1).astype(BF16)


def kernel(x, rel_bias, ln1_g, w_in, lam_q1, lam_k1, lam_q2, lam_k2, subln_g, conv_w, conv_b, conv_ln_g, conv_ln_b, ssm_lam_re, ssm_lam_im, ssm_log_dt, ssm_b_re, ssm_b_im, ssm_c_re, ssm_c_im, ssm_d, ssm_glu_w, ssm_glu_b, w_out, ln2_g, group_router_w, group_router_b, expert_router_w, expert_router_b, w_gate, w_up, w_down, final_g):
    B, L, D = x.shape
    T = B * L
    depth = w_in.shape[0]
    assert D == D_MODEL and L % CONV_ROWS == 0 and L % ATTN_TILE == 0 and T % PROJ_ROWS == 0
    x2 = x.reshape(T, D)
    bias_diag, bias_sub = _bias_tiles(rel_bias)
    row = lambda v: v.astype(F32).reshape(1, -1)
    for l in range(depth):
        lam_init = 0.8 - 0.6 * math.exp(-0.3 * l)
        lam = (jnp.exp(jnp.sum(lam_q1[l].astype(F32) * lam_k1[l].astype(F32)))
               - jnp.exp(jnp.sum(lam_q2[l].astype(F32) * lam_k2[l].astype(F32))) + lam_init).reshape(1)
        qq, kk, v, conv_in, ssm_in = _inproj(x2, row(ln1_g[l]), _pack_w_in(w_in[l]))
        a = _attention(qq, kk, v, lam, bias_diag, bias_sub, row(subln_g[l]), 1.0 - lam_init, B, L)
        c = _conformer_conv(conv_in, conv_w[l], row(conv_b[l]), row(conv_ln_g[l]), row(conv_ln_b[l]), B, L)
        wb, apr, api, wc = _ssm_weights(ssm_lam_re[l], ssm_lam_im[l], ssm_log_dt[l], ssm_b_re[l], ssm_b_im[l],
                                        ssm_c_re[l], ssm_c_im[l])
        s = _s5_ssm(ssm_in, wb, apr, api, wc, row(ssm_d[l]), ssm_glu_w[l].astype(BF16), row(ssm_glu_b[l]), B, L)
        wr = jnp.zeros((D, LANES), F32).at[:, :N_GROUPS].set(group_router_w[l]) \
            .at[:, N_GROUPS:N_GROUPS + N_EXPERTS].set(expert_router_w[l]).astype(BF16)
        br = jnp.zeros((1, LANES), F32).at[0, :N_GROUPS].set(group_router_b[l]) \
            .at[0, N_GROUPS:N_GROUPS + N_EXPERTS].set(expert_router_b[l])
        x1, h2, route, counts = _outproj_route(x2, a, c, s, w_out[l].astype(BF16), row(ln2_g[l]), wr, br)
        dest1, dest2, tok_pad, blk_e, nused, n_blocks = _routing_plan(route, counts, T)
        xs = _dispatch(nused, tok_pad, h2, n_blocks)
        yb = _experts(blk_e, nused, xs, w_gate[l].astype(BF16), w_up[l].astype(BF16), w_down[l].astype(BF16),
                      n_blocks)
        x2 = _combine(dest1, dest2, yb, x1, route, row(final_g), final_norm=(l == depth - 1))
    return x2.reshape(B, L, D)
```

```python
import functools
import math

import jax
import jax.numpy as jnp
from jax import lax
from jax.experimental import pallas as pl
from jax.experimental.pallas import tpu as pltpu

F32 = jnp.float32
BF16 = jnp.bfloat16

D_MODEL = 1024
N_HEADS = 4
QK_DIM = 64
V_DIM = 128
ATTN_WIDTH = N_HEADS * V_DIM
QK_COLS = N_HEADS * QK_DIM
CONV_WIDTH = 256
CONV_TAPS = 31
SSM_WIDTH = 256
SSM_GROUP = 16
SSM_GROUPS = 16
SSM_STATE = 64
SSM_LANES = SSM_GROUPS * SSM_STATE
REL_BUCKETS = 32
REL_MAX_EXACT = 16
REL_MAX_DIST = 128
N_GROUPS = 4
EXPERTS_PER_GROUP = 8
N_EXPERTS = N_GROUPS * EXPERTS_PER_GROUP
EXPERT_FF = 512
RMS_EPS = 1e-6
LN_EPS = 1e-5
NEG_BIG = -1e30

LANES = 128
SUBLANES = 8
VMEM_LIMIT = 48 * 1024 * 1024

PROJ_ROWS = 512
ATTN_TILE = 512
CONV_ROWS = 512
CONV_HALO = 32
SSM_ROWS = 256
MOE_BLOCK = 512
COMBINE_ROWS = 256


def _params(sem):
    return pltpu.CompilerParams(dimension_semantics=sem, vmem_limit_bytes=VMEM_LIMIT)


def _inproj_kernel(x_ref, g_ref, w_ref, wvt_ref, qq_ref, kk_ref, conv_ref, ssm_ref, vt_ref):
    x = x_ref[...]
    ms = jnp.mean(x * x, axis=-1, keepdims=True)
    h = (x * lax.rsqrt(ms + RMS_EPS) * g_ref[...]).astype(BF16)
    o = 0
    for ref in (qq_ref, kk_ref, conv_ref, ssm_ref):
        n = ref.shape[-1]
        ref[...] = jnp.dot(h, w_ref[:, o:o + n], preferred_element_type=F32).astype(ref.dtype)
        o += n
    vt_ref[0] = lax.dot_general(wvt_ref[...], h, (((1,), (1,)), ((), ())),
                                preferred_element_type=F32).astype(vt_ref.dtype)


def _inproj(x2, g, w, wvt):
    T = x2.shape[0]
    tm = PROJ_ROWS
    widths = (ATTN_WIDTH, ATTN_WIDTH, 2 * CONV_WIDTH, SSM_WIDTH)
    dtypes = (BF16, BF16, F32, F32)
    return pl.pallas_call(
        _inproj_kernel,
        grid=(T // tm,),
        in_specs=[
            pl.BlockSpec((tm, D_MODEL), lambda i: (i, 0)),
            pl.BlockSpec((1, D_MODEL), lambda i: (0, 0)),
            pl.BlockSpec(w.shape, lambda i: (0, 0)),
            pl.BlockSpec(wvt.shape, lambda i: (0, 0)),
        ],
        out_specs=[pl.BlockSpec((tm, n), lambda i: (i, 0)) for n in widths]
        + [pl.BlockSpec((1, ATTN_WIDTH, tm), lambda i: (i, 0, 0))],
        out_shape=[jax.ShapeDtypeStruct((T, n), dt) for n, dt in zip(widths, dtypes)]
        + [jax.ShapeDtypeStruct((T // tm, ATTN_WIDTH, tm), BF16)],
        compiler_params=_params(("arbitrary",)),
        name="inproj",
    )(x2, g, w, wvt)


def _attn_kernel(lam_ref, q_ref, k_ref, vt_ref, bd_ref, bs_ref, g_ref, o_ref,
                 qs_ref, m_ref, l_ref, acc_ref, *, out_scale):
    t = ATTN_TILE
    qi = pl.program_id(2)
    q = q_ref[...].astype(F32)
    lane = lax.broadcasted_iota(jnp.int32, q.shape, 1)
    qs_ref[0:t, :] = jnp.where(lane < QK_DIM, q, 0.0).astype(BF16)
    qs_ref[t:2 * t, :] = jnp.where(lane >= QK_DIM, q, 0.0).astype(BF16)
    m_ref[...] = jnp.full(m_ref.shape, NEG_BIG, F32)
    l_ref[...] = jnp.zeros(l_ref.shape, F32)
    acc_ref[...] = jnp.zeros(acc_ref.shape, F32)

    def step(j, bias):
        k = k_ref[pl.ds(pl.multiple_of(j * t, t), t), :]
        s = lax.dot_general(k, qs_ref[...], (((1,), (1,)), ((), ())), preferred_element_type=F32)
        if bias is not None:
            s = s + jnp.concatenate([bias, bias], axis=1)
        m_prev = m_ref[...]
        m_new = jnp.maximum(m_prev, jnp.max(s, axis=0, keepdims=True))
        alpha = jnp.exp(m_prev - m_new)
        p = jnp.exp(s - m_new)
        l_ref[...] = alpha * l_ref[...] + jnp.sum(p, axis=0, keepdims=True)
        acc_ref[...] = alpha * acc_ref[...] + jnp.dot(vt_ref[j], p.astype(BF16), preferred_element_type=F32)
        m_ref[...] = m_new

    def far(j, carry):
        step(j, None)
        return carry

    lax.fori_loop(0, jnp.maximum(qi - 1, 0), far, 0)

    @pl.when(qi >= 1)
    def _():
        step(qi - 1, bs_ref[0])

    step(qi, bd_ref[0])

    acc = acc_ref[...]
    l = l_ref[...]
    a = acc[:, 0:t] / l[:, 0:t] - lam_ref[0] * (acc[:, t:2 * t] / l[:, t:2 * t])
    ms = jnp.mean(a * a, axis=0, keepdims=True)
    y = a * lax.rsqrt(ms + RMS_EPS) * g_ref[...] * out_scale
    o_ref[...] = jnp.transpose(y).astype(o_ref.dtype)


def _attention(qq, kk, vt, lam, bias_diag, bias_sub, subln_g, out_scale, B, L):
    T = B * L
    t = ATTN_TILE
    nq = L // t
    return pl.pallas_call(
        functools.partial(_attn_kernel, out_scale=out_scale),
        grid=(B, N_HEADS, nq),
        in_specs=[
            pl.BlockSpec(memory_space=pltpu.SMEM),
            pl.BlockSpec((t, LANES), lambda b, h, i: (b * nq + i, h)),
            pl.BlockSpec((L, LANES), lambda b, h, i: (b, h)),
            pl.BlockSpec((nq, V_DIM, t), lambda b, h, i: (b, h, 0)),
            pl.BlockSpec((1, t, t), lambda b, h, i: (h, 0, 0)),
            pl.BlockSpec((1, t, t), lambda b, h, i: (h, 0, 0)),
            pl.BlockSpec((V_DIM, 1), lambda b, h, i: (0, 0)),
        ],
        out_specs=pl.BlockSpec((t, LANES), lambda b, h, i: (b * nq + i, h)),
        out_shape=jax.ShapeDtypeStruct((T, ATTN_WIDTH), BF16),
        scratch_shapes=[
            pltpu.VMEM((2 * t, LANES), BF16),
            pltpu.VMEM((1, 2 * t), F32),
            pltpu.VMEM((1, 2 * t), F32),
            pltpu.VMEM((V_DIM, 2 * t), F32),
        ],
        compiler_params=_params(("arbitrary", "arbitrary", "arbitrary")),
        name="diff_attn",
    )(lam, qq, kk, vt, bias_diag, bias_sub, subln_g)


def _rel_bucket(rel):
    n = jnp.maximum(rel, 0)
    nf = jnp.maximum(n, 1).astype(F32)
    large = REL_MAX_EXACT + (jnp.log(nf / REL_MAX_EXACT) / math.log(REL_MAX_DIST / REL_MAX_EXACT)
                             * (REL_BUCKETS - REL_MAX_EXACT)).astype(jnp.int32)
    large = jnp.minimum(large, REL_BUCKETS - 1)
    return jnp.where(n < REL_MAX_EXACT, n, large)


def _bias_tiles(rel_table):
    t = ATTN_TILE
    assert t >= REL_MAX_DIST
    far = rel_table[REL_BUCKETS - 1].astype(F32)
    rel_d = jnp.arange(t)[None, :] - jnp.arange(t)[:, None]
    bd = jnp.transpose(rel_table[_rel_bucket(rel_d)].astype(F32) - far, (2, 0, 1))
    bd = jnp.where((rel_d >= 0)[None], bd, NEG_BIG)
    bs = jnp.transpose(rel_table[_rel_bucket(rel_d + t)].astype(F32) - far, (2, 0, 1))
    return bd, bs


def _conv_kernel(u_ref, w_ref, b_ref, g_ref, beta_ref, o_ref, h_ref):
    tt = CONV_ROWS
    j = pl.program_id(1)

    @pl.when(j == 0)
    def _():
        h_ref[0:CONV_HALO, :] = jnp.zeros((CONV_HALO, CONV_WIDTH), F32)

    @pl.when(j > 0)
    def _():
        h_ref[0:CONV_HALO, :] = h_ref[tt:tt + CONV_HALO, :]

    u = u_ref[...]
    h_ref[CONV_HALO:CONV_HALO + tt, :] = u[:, 0:CONV_WIDTH] * jax.nn.sigmoid(u[:, CONV_WIDTH:])
    acc = jnp.broadcast_to(b_ref[...], (tt, CONV_WIDTH))
    off = CONV_HALO - (CONV_TAPS - 1)
    for k in range(CONV_TAPS):
        acc = acc + w_ref[k:k + 1, :] * h_ref[off + k:off + k + tt, :]
    mu = jnp.mean(acc, axis=-1, keepdims=True)
    cen = acc - mu
    var = jnp.mean(cen * cen, axis=-1, keepdims=True)
    y = cen * lax.rsqrt(var + LN_EPS) * g_ref[...] + beta_ref[...]
    o_ref[...] = jax.nn.silu(y).astype(o_ref.dtype)


def _conformer_conv(conv_in, w, b, g, beta, B, L):
    T = B * L
    tt = CONV_ROWS
    nt = L // tt
    vec = pl.BlockSpec((1, CONV_WIDTH), lambda bb, j: (0, 0))
    return pl.pallas_call(
        _conv_kernel,
        grid=(B, nt),
        in_specs=[
            pl.BlockSpec((tt, 2 * CONV_WIDTH), lambda bb, j: (bb * nt + j, 0)),
            pl.BlockSpec((CONV_TAPS, CONV_WIDTH), lambda bb, j: (0, 0)),
            vec, vec, vec,
        ],
        out_specs=pl.BlockSpec((tt, CONV_WIDTH), lambda bb, j: (bb * nt + j, 0)),
        out_shape=jax.ShapeDtypeStruct((T, CONV_WIDTH), BF16),
        scratch_shapes=[pltpu.VMEM((tt + CONV_HALO, CONV_WIDTH), F32)],
        compiler_params=_params(("arbitrary", "arbitrary")),
        name="conformer_conv",
    )(conv_in, w, b, g, beta)


def _ssm_kernel(u_ref, wb_ref, apr_ref, api_ref, wc_ref, d_ref, gw_ref, gb_ref, o_ref,
                xs_ref, carry_ref):
    tt = SSM_ROWS
    n = SSM_LANES
    j = pl.program_id(1)

    @pl.when(j == 0)
    def _():
        carry_ref[...] = jnp.zeros(carry_ref.shape, F32)

    u = u_ref[...]
    xs_ref[...] = jnp.dot(u.astype(BF16), wb_ref[...], preferred_element_type=F32)
    apr = apr_ref[...]
    api = api_ref[...]
    row = lax.broadcasted_iota(jnp.int32, (SUBLANES, n), 0)

    def block(r, carry):
        cr, ci = carry
        start = pl.multiple_of(r * SUBLANES, SUBLANES)
        xr = xs_ref[pl.ds(start, SUBLANES), 0:n]
        xi = xs_ref[pl.ds(start, SUBLANES), n:2 * n]
        for shift in (1, 2, 4):
            ar = apr[shift - 1:shift, :]
            ai = api[shift - 1:shift, :]
            sr = jnp.where(row >= shift, pltpu.roll(xr, shift, 0), 0.0)
            si = jnp.where(row >= shift, pltpu.roll(xi, shift, 0), 0.0)
            xr, xi = xr + (ar * sr - ai * si), xi + (ar * si + ai * sr)
        xr, xi = xr + (apr * cr - api * ci), xi + (apr * ci + api * cr)
        xs_ref[pl.ds(start, SUBLANES), 0:n] = xr
        xs_ref[pl.ds(start, SUBLANES), n:2 * n] = xi
        return xr[SUBLANES - 1:SUBLANES, :], xi[SUBLANES - 1:SUBLANES, :]

    cr, ci = lax.fori_loop(0, tt // SUBLANES, block, (carry_ref[0:1, :], carry_ref[1:2, :]))
    carry_ref[0:1, :] = cr
    carry_ref[1:2, :] = ci

    y = jnp.dot(xs_ref[...].astype(BF16), wc_ref[...], preferred_element_type=F32) + u * d_ref[...]
    g = jax.nn.gelu(y)
    z = jnp.dot(g.astype(BF16), gw_ref[...], preferred_element_type=F32) + gb_ref[...]
    o_ref[...] = (g * jax.nn.sigmoid(z)).astype(o_ref.dtype)


def _s5_ssm(ssm_in, wb, apr, api, wc, d, gw, gb, B, L):
    T = B * L
    tt = SSM_ROWS
    nt = L // tt
    const = lambda a: pl.BlockSpec(a.shape, lambda bb, j: (0, 0))
    return pl.pallas_call(
        _ssm_kernel,
        grid=(B, nt),
        in_specs=[pl.BlockSpec((tt, SSM_WIDTH), lambda bb, j: (bb * nt + j, 0)),
                  const(wb), const(apr), const(api), const(wc), const(d), const(gw), const(gb)],
        out_specs=pl.BlockSpec((tt, SSM_WIDTH), lambda bb, j: (bb * nt + j, 0)),
        out_shape=jax.ShapeDtypeStruct((T, SSM_WIDTH), BF16),
        scratch_shapes=[pltpu.VMEM((tt, 2 * SSM_LANES), F32), pltpu.VMEM((SUBLANES, SSM_LANES), F32)],
        compiler_params=_params(("arbitrary", "arbitrary")),
        name="s5_scan",
    )(ssm_in, wb, apr, api, wc, d, gw, gb)


def _ssm_weights(lam_re, lam_im, log_dt, b_re, b_im, c_re, c_im):
    G, P, H = SSM_GROUPS, SSM_STATE, SSM_GROUP
    dt = jnp.exp(log_dt.astype(F32))[:, None]
    lr, li = lam_re.astype(F32), lam_im.astype(F32)
    mag = jnp.exp(lr * dt)
    ar, ai = mag * jnp.cos(li * dt), mag * jnp.sin(li * dt)
    den = lr * lr + li * li
    zr = ((ar - 1.0) * lr + ai * li) / den
    zi = (ai * lr - (ar - 1.0) * li) / den
    bre, bim = b_re.astype(F32), b_im.astype(F32)
    bbr = zr[..., None] * bre - zi[..., None] * bim
    bbi = zr[..., None] * bim + zi[..., None] * bre
    eye = jnp.eye(G, dtype=F32)
    wb = jnp.concatenate([jnp.einsum('gph,gk->ghkp', bbr, eye).reshape(G * H, G * P),
                          jnp.einsum('gph,gk->ghkp', bbi, eye).reshape(G * H, G * P)], axis=1)
    wc = jnp.concatenate([jnp.einsum('ghp,gk->gpkh', c_re.astype(F32), eye).reshape(G * P, G * H),
                          -jnp.einsum('ghp,gk->gpkh', c_im.astype(F32), eye).reshape(G * P, G * H)], axis=0)
    pr, pi = [ar.reshape(1, G * P)], [ai.reshape(1, G * P)]
    for _ in range(SUBLANES - 1):
        pr, pi = (pr + [pr[-1] * pr[0] - pi[-1] * pi[0]], pi + [pr[-1] * pi[0] + pi[-1] * pr[0]])
    return wb.astype(BF16), jnp.concatenate(pr, axis=0), jnp.concatenate(pi, axis=0), wc.astype(BF16)


def _outproj_kernel(x_ref, a_ref, c_ref, s_ref, w_ref, g_ref, wr_ref, br_ref,
                    x1_ref, h2_ref, route_ref, cnt_ref, carry_ref):
    tm = PROJ_ROWS
    i = pl.program_id(0)

    @pl.when(i == 0)
    def _():
        carry_ref[...] = jnp.zeros(carry_ref.shape, F32)

    o1, o2 = ATTN_WIDTH, ATTN_WIDTH + CONV_WIDTH
    y = (jnp.dot(a_ref[...], w_ref[0:o1, :], preferred_element_type=F32)
         + jnp.dot(c_ref[...], w_ref[o1:o2, :], preferred_element_type=F32)
         + jnp.dot(s_ref[...], w_ref[o2:, :], preferred_element_type=F32))
    x1 = x_ref[...] + y
    x1_ref[...] = x1
    ms = jnp.mean(x1 * x1, axis=-1, keepdims=True)
    h2 = x1 * lax.rsqrt(ms + RMS_EPS) * g_ref[...]
    h2_ref[...] = h2

    logits = jnp.dot(h2.astype(BF16), wr_ref[...], preferred_element_type=F32) + br_ref[...]
    col = lax.broadcasted_iota(jnp.int32, logits.shape, 1)

    def first_max(vals):
        top = jnp.max(vals, axis=-1, keepdims=True)
        return top, jnp.min(jnp.where(vals == top, col, LANES), axis=-1, keepdims=True)

    glog = jnp.where(col < N_GROUPS, logits, NEG_BIG)
    gmax, gidx = first_max(glog)
    gp = 1.0 / jnp.sum(jnp.exp(glog - gmax), axis=-1, keepdims=True)
    lo = N_GROUPS + gidx * EXPERTS_PER_GROUP
    e = jnp.where((col >= lo) & (col < lo + EXPERTS_PER_GROUP), logits, NEG_BIG)
    v1, i1 = first_max(e)
    e = jnp.where(col == i1, NEG_BIG, e)
    v2, i2 = first_max(e)
    ex = jnp.exp(v2 - v1)
    w1 = gp * (1.0 / (1.0 + ex))
    w2 = gp * (ex / (1.0 + ex))

    hit1 = col == i1
    hit2 = col == i2
    onehot = jnp.where(hit1 | hit2, 1.0, 0.0)
    r = lax.broadcasted_iota(jnp.int32, (tm, tm), 0)
    c = lax.broadcasted_iota(jnp.int32, (tm, tm), 1)
    before = jnp.where(r > c, 1.0, 0.0).astype(BF16)
    prior = jnp.dot(before, onehot.astype(BF16), preferred_element_type=F32) + carry_ref[...]
    rank1 = jnp.sum(jnp.where(hit1, prior, 0.0), axis=-1, keepdims=True)
    rank2 = jnp.sum(jnp.where(hit2, prior, 0.0), axis=-1, keepdims=True)
    carry_ref[...] = carry_ref[...] + jnp.sum(onehot, axis=0, keepdims=True)
    cnt_ref[...] = carry_ref[...]

    fields = ((i1 - N_GROUPS).astype(F32), (i2 - N_GROUPS).astype(F32), w1, w2, rank1, rank2)
    route = jnp.zeros(logits.shape, F32)
    for k, val in enumerate(fields):
        route = jnp.where(col == k, val, route)
    route_ref[...] = route


def _outproj_route(x2, a, c, s, w_out, g, wr, br):
    T = x2.shape[0]
    tm = PROJ_ROWS
    rows = lambda n: pl.BlockSpec((tm, n), lambda i: (i, 0))
    const = lambda arr: pl.BlockSpec(arr.shape, lambda i: (0, 0))
    return pl.pallas_call(
        _outproj_kernel,
        grid=(T // tm,),
        in_specs=[rows(D_MODEL), rows(ATTN_WIDTH), rows(CONV_WIDTH), rows(SSM_WIDTH),
                  const(w_out), const(g), const(wr), const(br)],
        out_specs=[rows(D_MODEL), rows(D_MODEL), rows(LANES), pl.BlockSpec((1, LANES), lambda i: (0, 0))],
        out_shape=[jax.ShapeDtypeStruct((T, D_MODEL), F32), jax.ShapeDtypeStruct((T, D_MODEL), F32),
                   jax.ShapeDtypeStruct((T, LANES), F32), jax.ShapeDtypeStruct((1, LANES), F32)],
        scratch_shapes=[pltpu.VMEM((1, LANES), F32)],
        compiler_params=_params(("arbitrary",)),
        name="outproj_route",
    )(x2, a, c, s, w_out, g, wr, br)


def _row_copy(src_hbm, row, dst_ref, slot, sem):
    return pltpu.make_async_copy(src_hbm.at[pl.ds(row, 1), :], dst_ref.at[pl.ds(slot, 1), :], sem)


def _dispatch_kernel(nused_ref, idx_ref, src_hbm, o_ref, sem):
    b = pl.program_id(0)

    @pl.when(b < nused_ref[0])
    def _():
        def issue(r, carry):
            _row_copy(src_hbm, idx_ref[0, 0, r], o_ref, r, sem).start()
            return carry

        def drain(r, carry):
            _row_copy(src_hbm, 0, o_ref, r, sem).wait()
            return carry

        lax.fori_loop(0, MOE_BLOCK, issue, 0)
        lax.fori_loop(0, MOE_BLOCK, drain, 0)

    @pl.when(b >= nused_ref[0])
    def _():
        o_ref[...] = jnp.zeros(o_ref.shape, o_ref.dtype)


def _dispatch(nused, tok_pad, h2, n_blocks):
    return pl.pallas_call(
        _dispatch_kernel,
        grid_spec=pltpu.PrefetchScalarGridSpec(
            num_scalar_prefetch=1,
            grid=(n_blocks,),
            in_specs=[pl.BlockSpec((1, 1, MOE_BLOCK), lambda b, nu: (b, 0, 0), memory_space=pltpu.SMEM),
                      pl.BlockSpec(memory_space=pl.ANY)],
            out_specs=pl.BlockSpec((MOE_BLOCK, D_MODEL), lambda b, nu: (b, 0)),
            scratch_shapes=[pltpu.SemaphoreType.DMA],
        ),
        out_shape=jax.ShapeDtypeStruct((n_blocks * MOE_BLOCK, D_MODEL), F32),
        compiler_params=_params(("arbitrary",)),
        name="moe_dispatch",
    )(nused, tok_pad.reshape(n_blocks, 1, MOE_BLOCK), h2)


def _expert_kernel(blk_e_ref, nused_ref, x_ref, wg_ref, wu_ref, wd_ref, o_ref):
    b = pl.program_id(0)

    @pl.when(b < nused_ref[0])
    def _():
        x = x_ref[...].astype(BF16)
        gate = jnp.dot(x, wg_ref[0], preferred_element_type=F32)
        up = jnp.dot(x, wu_ref[0], preferred_element_type=F32)
        act = (jax.nn.silu(gate) * up).astype(BF16)
        o_ref[...] = jnp.dot(act, wd_ref[0], preferred_element_type=F32)

    @pl.when(b >= nused_ref[0])
    def _():
        o_ref[...] = jnp.zeros(o_ref.shape, o_ref.dtype)


def _experts(blk_e, nused, xs, wg, wu, wd, n_blocks):
    return pl.pallas_call(
        _expert_kernel,
        grid_spec=pltpu.PrefetchScalarGridSpec(
            num_scalar_prefetch=2,
            grid=(n_blocks,),
            in_specs=[pl.BlockSpec((MOE_BLOCK, D_MODEL), lambda b, be, nu: (b, 0)),
                      pl.BlockSpec((1, D_MODEL, EXPERT_FF), lambda b, be, nu: (be[b], 0, 0)),
                      pl.BlockSpec((1, D_MODEL, EXPERT_FF), lambda b, be, nu: (be[b], 0, 0)),
                      pl.BlockSpec((1, EXPERT_FF, D_MODEL), lambda b, be, nu: (be[b], 0, 0))],
            out_specs=pl.BlockSpec((MOE_BLOCK, D_MODEL), lambda b, be, nu: (b, 0)),
        ),
        out_shape=jax.ShapeDtypeStruct((n_blocks * MOE_BLOCK, D_MODEL), F32),
        compiler_params=_params(("arbitrary",)),
        name="moe_experts",
    )(blk_e, nused, xs, wg, wu, wd)


def _combine_kernel(d1_ref, d2_ref, yb_hbm, x_ref, route_ref, g_ref, o_ref, y1_ref, y2_ref, sems,
                    *, final_norm):
    tm = COMBINE_ROWS

    def issue(r, carry):
        _row_copy(yb_hbm, d1_ref[0, 0, r], y1_ref, r, sems.at[0]).start()
        _row_copy(yb_hbm, d2_ref[0, 0, r], y2_ref, r, sems.at[1]).start()
        return carry

    def drain(r, carry):
        _row_copy(yb_hbm, 0, y1_ref, r, sems.at[0]).wait()
        _row_copy(yb_hbm, 0, y2_ref, r, sems.at[1]).wait()
        return carry

    lax.fori_loop(0, tm, issue, 0)
    lax.fori_loop(0, tm, drain, 0)
    route = route_ref[...]
    x = x_ref[...] + (route[:, 2:3] * y1_ref[...] + route[:, 3:4] * y2_ref[...])
    if final_norm:
        ms = jnp.mean(x * x, axis=-1, keepdims=True)
        x = x * lax.rsqrt(ms + RMS_EPS) * g_ref[...]
    o_ref[...] = x


def _combine(dest1, dest2, yb, x1, route, g, final_norm):
    T = x1.shape[0]
    tm = COMBINE_ROWS
    nt = T // tm
    idx = pl.BlockSpec((1, 1, tm), lambda i: (i, 0, 0), memory_space=pltpu.SMEM)
    return pl.pallas_call(
        functools.partial(_combine_kernel, final_norm=final_norm),
        grid=(nt,),
        in_specs=[idx, idx, pl.BlockSpec(memory_space=pl.ANY),
                  pl.BlockSpec((tm, D_MODEL), lambda i: (i, 0)),
                  pl.BlockSpec((tm, LANES), lambda i: (i, 0)),
                  pl.BlockSpec((1, D_MODEL), lambda i: (0, 0))],
        out_specs=pl.BlockSpec((tm, D_MODEL), lambda i: (i, 0)),
        out_shape=jax.ShapeDtypeStruct((T, D_MODEL), F32),
        scratch_shapes=[pltpu.VMEM((tm, D_MODEL), F32), pltpu.VMEM((tm, D_MODEL), F32),
                        pltpu.SemaphoreType.DMA((2,))],
        compiler_params=_params(("arbitrary",)),
        name="moe_combine",
    )(dest1.reshape(nt, 1, tm), dest2.reshape(nt, 1, tm), yb, x1, route, g)


def _routing_plan(route, counts, T):
    e1 = route[:, 0].astype(jnp.int32)
    e2 = route[:, 1].astype(jnp.int32)
    rank1 = route[:, 4].astype(jnp.int32)
    rank2 = route[:, 5].astype(jnp.int32)
    cnt = counts[0, N_GROUPS:N_GROUPS + N_EXPERTS].astype(jnp.int32)
    nblk = (cnt + MOE_BLOCK - 1) // MOE_BLOCK
    blk_end = jnp.cumsum(nblk)
    first_row = (blk_end - nblk) * MOE_BLOCK
    dest1 = first_row[e1] + rank1
    dest2 = first_row[e2] + rank2
    n_blocks = (T * 2) // MOE_BLOCK + N_EXPERTS
    blk_e = jnp.minimum(jnp.sum(jnp.arange(n_blocks)[:, None] >= blk_end[None, :], axis=1),
                        N_EXPERTS - 1).astype(jnp.int32)
    tok = jnp.arange(T, dtype=jnp.int32)
    tok_pad = jnp.zeros((n_blocks * MOE_BLOCK,), jnp.int32).at[dest1].set(tok).at[dest2].set(tok)
    return dest1, dest2, tok_pad, blk_e, blk_end[-1:].astype(jnp.int32), n_blocks


def _pack_w_in(w):
    scale = QK_DIM ** -0.5
    q1, q2, k1, k2 = (w[:, i * QK_COLS:(i + 1) * QK_COLS].reshape(D_MODEL, N_HEADS, QK_DIM) for i in range(4))
    qq = (jnp.concatenate([q1, q2], axis=-1) * scale).reshape(D_MODEL, ATTN_WIDTH)
    kk = jnp.concatenate([k1, k2], axis=-1).reshape(D_MODEL, ATTN_WIDTH)
    v0 = 4 * QK_COLS
    packed = jnp.concatenate([qq, kk, w[:, v0 + ATTN_WIDTH:]], axis=1).astype(BF16)
    return packed, jnp.transpose(w[:, v0:v0 + ATTN_WIDTH]).astype(BF16)


def kernel(x, rel_bias, ln1_g, w_in, lam_q1, lam_k1, lam_q2, lam_k2, subln_g, conv_w, conv_b, conv_ln_g, conv_ln_b, ssm_lam_re, ssm_lam_im, ssm_log_dt, ssm_b_re, ssm_b_im, ssm_c_re, ssm_c_im, ssm_d, ssm_glu_w, ssm_glu_b, w_out, ln2_g, group_router_w, group_router_b, expert_router_w, expert_router_b, w_gate, w_up, w_down, final_g):
    B, L, D = x.shape
    T = B * L
    depth = w_in.shape[0]
    assert D == D_MODEL and L % CONV_ROWS == 0 and L % ATTN_TILE == 0 and L % SSM_ROWS == 0
    assert PROJ_ROWS == ATTN_TILE and T % COMBINE_ROWS == 0 and (2 * T) % MOE_BLOCK == 0
    x2 = x.reshape(T, D)
    bias_diag, bias_sub = _bias_tiles(rel_bias)
    row = lambda v: v.astype(F32).reshape(1, -1)
    for l in range(depth):
        lam_init = 0.8 - 0.6 * math.exp(-0.3 * l)
        lam = (jnp.exp(jnp.sum(lam_q1[l].astype(F32) * lam_k1[l].astype(F32)))
               - jnp.exp(jnp.sum(lam_q2[l].astype(F32) * lam_k2[l].astype(F32))) + lam_init).reshape(1)
        qq, kk, conv_in, ssm_in, vt = _inproj(x2, row(ln1_g[l]), *_pack_w_in(w_in[l]))
        a = _attention(qq, kk, vt, lam, bias_diag, bias_sub, subln_g[l].astype(F32).reshape(V_DIM, 1),
                       1.0 - lam_init, B, L)
        c = _conformer_conv(conv_in, conv_w[l], row(conv_b[l]), row(conv_ln_g[l]), row(conv_ln_b[l]), B, L)
        wb, apr, api, wc = _ssm_weights(ssm_lam_re[l], ssm_lam_im[l], ssm_log_dt[l], ssm_b_re[l], ssm_b_im[l],
                                        ssm_c_re[l], ssm_c_im[l])
        s = _s5_ssm(ssm_in, wb, apr, api, wc, row(ssm_d[l]), ssm_glu_w[l].astype(BF16), row(ssm_glu_b[l]), B, L)
        wr = jnp.zeros((D, LANES), F32).at[:, :N_GROUPS].set(group_router_w[l]) \
            .at[:, N_GROUPS:N_GROUPS + N_EXPERTS].set(expert_router_w[l]).astype(BF16)
        br = jnp.zeros((1, LANES), F32).at[0, :N_GROUPS].set(group_router_b[l]) \
            .at[0, N_GROUPS:N_GROUPS + N_EXPERTS].set(expert_router_b[l])
        x1, h2, route, counts = _outproj_route(x2, a, c, s, w_out[l].astype(BF16), row(ln2_g[l]), wr, br)
        dest1, dest2, tok_pad, blk_e, nused, n_blocks = _routing_plan(route, counts, T)
        xs = _dispatch(nused, tok_pad, h2, n_blocks)
        yb = _experts(blk_e, nused, xs, w_gate[l].astype(BF16), w_up[l].astype(BF16), w_down[l].astype(BF16),
                      n_blocks)
        x2 = _combine(dest1, dest2, yb, x1, route, row(final_g), final_norm=(l == depth - 1))
    return x2.reshape(B, L, D)
```

```python
import functools
import math

import jax
import jax.numpy as jnp
from jax import lax
from jax.experimental import pallas as pl
from jax.experimental.pallas import tpu as pltpu

F32 = jnp.float32
BF16 = jnp.bfloat16

D_MODEL = 1024
N_HEADS = 4
QK_DIM = 64
V_DIM = 128
ATTN_WIDTH = N_HEADS * V_DIM
QK_COLS = N_HEADS * QK_DIM
CONV_WIDTH = 256
CONV_TAPS = 31
SSM_WIDTH = 256
SSM_GROUP = 16
SSM_GROUPS = 16
SSM_STATE = 64
SSM_LANES = SSM_GROUPS * SSM_STATE
REL_BUCKETS = 32
REL_MAX_EXACT = 16
REL_MAX_DIST = 128
N_GROUPS = 4
EXPERTS_PER_GROUP = 8
N_EXPERTS = N_GROUPS * EXPERTS_PER_GROUP
EXPERT_FF = 512
RMS_EPS = 1e-6
LN_EPS = 1e-5
NEG_BIG = -1e30

LANES = 128
SUBLANES = 8
VMEM_LIMIT = 48 * 1024 * 1024

PROJ_ROWS = 512
ATTN_TILE = 512
ATTN_CHUNK = 64
LOG2E = math.log2(math.e)
CONV_ROWS = 512
CONV_HALO = 32
SSM_ROWS = 256
MOE_BLOCK = 512


def _params(sem):
    return pltpu.CompilerParams(dimension_semantics=sem, vmem_limit_bytes=VMEM_LIMIT)


def _inproj_kernel(x_ref, g_ref, w_ref, wvt_ref, qq_ref, kk_ref, conv_ref, ssm_ref, vt_ref):
    x = x_ref[...]
    ms = jnp.mean(x * x, axis=-1, keepdims=True)
    h = (x * lax.rsqrt(ms + RMS_EPS) * g_ref[...]).astype(BF16)
    o = 0
    for ref in (qq_ref, kk_ref, conv_ref, ssm_ref):
        n = ref.shape[-1]
        ref[...] = jnp.dot(h, w_ref[:, o:o + n], preferred_element_type=F32).astype(ref.dtype)
        o += n
    vt_ref[0] = lax.dot_general(wvt_ref[...], h, (((1,), (1,)), ((), ())),
                                preferred_element_type=F32).astype(vt_ref.dtype)


def _inproj(x2, g, w, wvt):
    T = x2.shape[0]
    tm = PROJ_ROWS
    widths = (ATTN_WIDTH, ATTN_WIDTH, 2 * CONV_WIDTH, SSM_WIDTH)
    dtypes = (BF16, BF16, F32, F32)
    return pl.pallas_call(
        _inproj_kernel,
        grid=(T // tm,),
        in_specs=[
            pl.BlockSpec((tm, D_MODEL), lambda i: (i, 0)),
            pl.BlockSpec((1, D_MODEL), lambda i: (0, 0)),
            pl.BlockSpec(w.shape, lambda i: (0, 0)),
            pl.BlockSpec(wvt.shape, lambda i: (0, 0)),
        ],
        out_specs=[pl.BlockSpec((tm, n), lambda i: (i, 0)) for n in widths]
        + [pl.BlockSpec((1, ATTN_WIDTH, tm), lambda i: (i, 0, 0))],
        out_shape=[jax.ShapeDtypeStruct((T, n), dt) for n, dt in zip(widths, dtypes)]
        + [jax.ShapeDtypeStruct((T // tm, ATTN_WIDTH, tm), BF16)],
        compiler_params=_params(("arbitrary",)),
        name="inproj",
    )(x2, g, w, wvt)


def _attn_kernel(lam_ref, q_ref, k_ref, vt_ref, bd_ref, bs_ref, g_ref, o_ref,
                 qs_ref, m_ref, l_ref, acc_ref, sa_ref, sb_ref, p_ref, *, out_scale):
    t = ATTN_TILE
    qi = pl.program_id(2)
    q = q_ref[...].astype(F32)
    lane = lax.broadcasted_iota(jnp.int32, q.shape, 1)
    qs_ref[0:t, :] = jnp.where(lane < QK_DIM, q, 0.0).astype(BF16)
    qs_ref[t:2 * t, :] = jnp.where(lane >= QK_DIM, q, 0.0).astype(BF16)
    m_ref[...] = jnp.full(m_ref.shape, NEG_BIG, F32)
    l_ref[...] = jnp.zeros(l_ref.shape, F32)
    acc_ref[...] = jnp.zeros(acc_ref.shape, F32)

    chunks = [(c, c + ATTN_CHUNK) for c in range(0, t, ATTN_CHUNK)]
    fold = lambda a: a.reshape(ATTN_CHUNK // SUBLANES, SUBLANES, 2 * t)

    def scores(j, s_ref):
        k = k_ref[pl.ds(pl.multiple_of(j * t, t), t), :]
        s_ref[...] = lax.dot_general(k, qs_ref[...], (((1,), (1,)), ((), ())), preferred_element_type=F32)

    def accumulate(j, s_ref, bias_ref=None):
        top = jnp.full((SUBLANES, 2 * t), NEG_BIG, F32)
        for lo, hi in chunks:
            s = s_ref[lo:hi, :]
            if bias_ref is not None:
                b = bias_ref[0, lo:hi, :]
                s = s + jnp.concatenate([b, b], axis=1)
                s_ref[lo:hi, :] = s
            top = jnp.maximum(top, jnp.max(fold(s), axis=0))
        m_prev = m_ref[...]
        m_new = jnp.maximum(m_prev, jnp.max(top, axis=0, keepdims=True))
        alpha = jnp.exp2(m_prev - m_new)
        tot = jnp.zeros((SUBLANES, 2 * t), F32)
        for lo, hi in chunks:
            p = jnp.exp2(s_ref[lo:hi, :] - m_new)
            tot = tot + jnp.sum(fold(p), axis=0)
            p_ref[lo:hi, :] = p.astype(BF16)
        l_ref[...] = alpha * l_ref[...] + jnp.sum(tot, axis=0, keepdims=True)
        acc_ref[...] = alpha * acc_ref[...] + jnp.dot(vt_ref[j], p_ref[...], preferred_element_type=F32)
        m_ref[...] = m_new

    nfar = jnp.maximum(qi - 1, 0)
    scores(0, sa_ref)

    def far_pair(i, carry):
        scores(2 * i + 1, sb_ref)
        accumulate(2 * i, sa_ref)
        scores(2 * i + 2, sa_ref)
        accumulate(2 * i + 1, sb_ref)
        return carry

    lax.fori_loop(0, nfar // 2, far_pair, 0)

    @pl.when(qi == 0)
    def _():
        accumulate(qi, sa_ref, bd_ref)

    @pl.when((qi >= 1) & (nfar % 2 == 0))
    def _():
        scores(qi, sb_ref)
        accumulate(qi - 1, sa_ref, bs_ref)
        accumulate(qi, sb_ref, bd_ref)

    @pl.when(nfar % 2 == 1)
    def _():
        scores(qi - 1, sb_ref)
        accumulate(qi - 2, sa_ref)
        scores(qi, sa_ref)
        accumulate(qi - 1, sb_ref, bs_ref)
        accumulate(qi, sa_ref, bd_ref)

    acc = acc_ref[...]
    l = l_ref[...]
    a = acc[:, 0:t] / l[:, 0:t] - lam_ref[0] * (acc[:, t:2 * t] / l[:, t:2 * t])
    ms = jnp.mean(a * a, axis=0, keepdims=True)
    y = a * lax.rsqrt(ms + RMS_EPS) * g_ref[...] * out_scale
    o_ref[...] = jnp.transpose(y).astype(o_ref.dtype)


def _attention(qq, kk, vt, lam, bias_diag, bias_sub, subln_g, out_scale, B, L):
    T = B * L
    t = ATTN_TILE
    nq = L // t
    return pl.pallas_call(
        functools.partial(_attn_kernel, out_scale=out_scale),
        grid=(B, N_HEADS, nq),
        in_specs=[
            pl.BlockSpec(memory_space=pltpu.SMEM),
            pl.BlockSpec((t, LANES), lambda b, h, i: (b * nq + i, h)),
            pl.BlockSpec((L, LANES), lambda b, h, i: (b, h)),
            pl.BlockSpec((nq, V_DIM, t), lambda b, h, i: (b, h, 0)),
            pl.BlockSpec((1, t, t), lambda b, h, i: (h, 0, 0)),
            pl.BlockSpec((1, t, t), lambda b, h, i: (h, 0, 0)),
            pl.BlockSpec((V_DIM, 1), lambda b, h, i: (0, 0)),
        ],
        out_specs=pl.BlockSpec((t, LANES), lambda b, h, i: (b * nq + i, h)),
        out_shape=jax.ShapeDtypeStruct((T, ATTN_WIDTH), BF16),
        scratch_shapes=[
            pltpu.VMEM((2 * t, LANES), BF16),
            pltpu.VMEM((1, 2 * t), F32),
            pltpu.VMEM((1, 2 * t), F32),
            pltpu.VMEM((V_DIM, 2 * t), F32),
            pltpu.VMEM((t, 2 * t), F32),
            pltpu.VMEM((t, 2 * t), F32),
            pltpu.VMEM((t, 2 * t), BF16),
        ],
        compiler_params=_params(("arbitrary", "arbitrary", "arbitrary")),
        name="diff_attn",
    )(lam, qq, kk, vt, bias_diag, bias_sub, subln_g)


def _rel_bucket(rel):
    n = jnp.maximum(rel, 0)
    nf = jnp.maximum(n, 1).astype(F32)
    large = REL_MAX_EXACT + (jnp.log(nf / REL_MAX_EXACT) / math.log(REL_MAX_DIST / REL_MAX_EXACT)
                             * (REL_BUCKETS - REL_MAX_EXACT)).astype(jnp.int32)
    large = jnp.minimum(large, REL_BUCKETS - 1)
    return jnp.where(n < REL_MAX_EXACT, n, large)


def _bias_tiles(rel_table):
    t = ATTN_TILE
    assert t >= REL_MAX_DIST
    far = rel_table[REL_BUCKETS - 1].astype(F32)
    rel_d = jnp.arange(t)[None, :] - jnp.arange(t)[:, None]

    def lookup(bucket):
        out = jnp.zeros((N_HEADS,) + bucket.shape, F32)
        for b in range(REL_BUCKETS):
            out = jnp.where((bucket == b)[None], (rel_table[b].astype(F32) - far)[:, None, None], out)
        return out * LOG2E

    bd = jnp.where((rel_d >= 0)[None], lookup(_rel_bucket(rel_d)), NEG_BIG)
    return bd, lookup(_rel_bucket(rel_d + t))


def _conv_kernel(u_ref, w_ref, b_ref, g_ref, beta_ref, o_ref, h_ref):
    tt = CONV_ROWS
    j = pl.program_id(1)

    @pl.when(j == 0)
    def _():
        h_ref[0:CONV_HALO, :] = jnp.zeros((CONV_HALO, CONV_WIDTH), F32)

    @pl.when(j > 0)
    def _():
        h_ref[0:CONV_HALO, :] = h_ref[tt:tt + CONV_HALO, :]

    u = u_ref[...]
    h_ref[CONV_HALO:CONV_HALO + tt, :] = u[:, 0:CONV_WIDTH] * jax.nn.sigmoid(u[:, CONV_WIDTH:])
    acc = jnp.broadcast_to(b_ref[...], (tt, CONV_WIDTH))
    off = CONV_HALO - (CONV_TAPS - 1)
    for k in range(CONV_TAPS):
        acc = acc + w_ref[k:k + 1, :] * h_ref[off + k:off + k + tt, :]
    mu = jnp.mean(acc, axis=-1, keepdims=True)
    cen = acc - mu
    var = jnp.mean(cen * cen, axis=-1, keepdims=True)
    y = cen * lax.rsqrt(var + LN_EPS) * g_ref[...] + beta_ref[...]
    o_ref[...] = jax.nn.silu(y).astype(o_ref.dtype)


def _conformer_conv(conv_in, w, b, g, beta, B, L):
    T = B * L
    tt = CONV_ROWS
    nt = L // tt
    vec = pl.BlockSpec((1, CONV_WIDTH), lambda bb, j: (0, 0))
    return pl.pallas_call(
        _conv_kernel,
        grid=(B, nt),
        in_specs=[
            pl.BlockSpec((tt, 2 * CONV_WIDTH), lambda bb, j: (bb * nt + j, 0)),
            pl.BlockSpec((CONV_TAPS, CONV_WIDTH), lambda bb, j: (0, 0)),
            vec, vec, vec,
        ],
        out_specs=pl.BlockSpec((tt, CONV_WIDTH), lambda bb, j: (bb * nt + j, 0)),
        out_shape=jax.ShapeDtypeStruct((T, CONV_WIDTH), BF16),
        scratch_shapes=[pltpu.VMEM((tt + CONV_HALO, CONV_WIDTH), F32)],
        compiler_params=_params(("arbitrary", "arbitrary")),
        name="conformer_conv",
    )(conv_in, w, b, g, beta)


def _ssm_kernel(u_ref, wb_ref, apr_ref, api_ref, wc_ref, d_ref, gw_ref, gb_ref, o_ref,
                xs_ref, carry_ref):
    tt = SSM_ROWS
    n = SSM_LANES
    j = pl.program_id(1)

    @pl.when(j == 0)
    def _():
        carry_ref[...] = jnp.zeros(carry_ref.shape, F32)

    u = u_ref[...]
    xs_ref[...] = jnp.dot(u.astype(BF16), wb_ref[...], preferred_element_type=F32)
    apr = apr_ref[...]
    api = api_ref[...]
    row = lax.broadcasted_iota(jnp.int32, (SUBLANES, n), 0)

    def block(r, carry):
        cr, ci = carry
        start = pl.multiple_of(r * SUBLANES, SUBLANES)
        xr = xs_ref[pl.ds(start, SUBLANES), 0:n]
        xi = xs_ref[pl.ds(start, SUBLANES), n:2 * n]
        for shift in (1, 2, 4):
            ar = apr[shift - 1:shift, :]
            ai = api[shift - 1:shift, :]
            sr = jnp.where(row >= shift, pltpu.roll(xr, shift, 0), 0.0)
            si = jnp.where(row >= shift, pltpu.roll(xi, shift, 0), 0.0)
            xr, xi = xr + (ar * sr - ai * si), xi + (ar * si + ai * sr)
        xr, xi = xr + (apr * cr - api * ci), xi + (apr * ci + api * cr)
        xs_ref[pl.ds(start, SUBLANES), 0:n] = xr
        xs_ref[pl.ds(start, SUBLANES), n:2 * n] = xi
        return xr[SUBLANES - 1:SUBLANES, :], xi[SUBLANES - 1:SUBLANES, :]

    cr, ci = lax.fori_loop(0, tt // SUBLANES, block, (carry_ref[0:1, :], carry_ref[1:2, :]))
    carry_ref[0:1, :] = cr
    carry_ref[1:2, :] = ci

    y = jnp.dot(xs_ref[...].astype(BF16), wc_ref[...], preferred_element_type=F32) + u * d_ref[...]
    g = jax.nn.gelu(y)
    z = jnp.dot(g.astype(BF16), gw_ref[...], preferred_element_type=F32) + gb_ref[...]
    o_ref[...] = (g * jax.nn.sigmoid(z)).astype(o_ref.dtype)


def _s5_ssm(ssm_in, wb, apr, api, wc, d, gw, gb, B, L):
    T = B * L
    tt = SSM_ROWS
    nt = L // tt
    const = lambda a: pl.BlockSpec(a.shape, lambda bb, j: (0, 0))
    return pl.pallas_call(
        _ssm_kernel,
        grid=(B, nt),
        in_specs=[pl.BlockSpec((tt, SSM_WIDTH), lambda bb, j: (bb * nt + j, 0)),
                  const(wb), const(apr), const(api), const(wc), const(d), const(gw), const(gb)],
        out_specs=pl.BlockSpec((tt, SSM_WIDTH), lambda bb, j: (bb * nt + j, 0)),
        out_shape=jax.ShapeDtypeStruct((T, SSM_WIDTH), BF16),
        scratch_shapes=[pltpu.VMEM((tt, 2 * SSM_LANES), F32), pltpu.VMEM((SUBLANES, SSM_LANES), F32)],
        compiler_params=_params(("arbitrary", "arbitrary")),
        name="s5_scan",
    )(ssm_in, wb, apr, api, wc, d, gw, gb)


def _ssm_weights(lam_re, lam_im, log_dt, b_re, b_im, c_re, c_im):
    G, P, H = SSM_GROUPS, SSM_STATE, SSM_GROUP
    dt = jnp.exp(log_dt.astype(F32))[:, None]
    lr, li = lam_re.astype(F32), lam_im.astype(F32)
    mag = jnp.exp(lr * dt)
    ar, ai = mag * jnp.cos(li * dt), mag * jnp.sin(li * dt)
    den = lr * lr + li * li
    zr = ((ar - 1.0) * lr + ai * li) / den
    zi = (ai * lr - (ar - 1.0) * li) / den
    bre, bim = b_re.astype(F32), b_im.astype(F32)
    bbr = zr[..., None] * bre - zi[..., None] * bim
    bbi = zr[..., None] * bim + zi[..., None] * bre
    eye = jnp.eye(G, dtype=F32)
    wb = jnp.concatenate([jnp.einsum('gph,gk->ghkp', bbr, eye).reshape(G * H, G * P),
                          jnp.einsum('gph,gk->ghkp', bbi, eye).reshape(G * H, G * P)], axis=1)
    wc = jnp.concatenate([jnp.einsum('ghp,gk->gpkh', c_re.astype(F32), eye).reshape(G * P, G * H),
                          -jnp.einsum('ghp,gk->gpkh', c_im.astype(F32), eye).reshape(G * P, G * H)], axis=0)
    pr, pi = [ar.reshape(1, G * P)], [ai.reshape(1, G * P)]
    for _ in range(SUBLANES - 1):
        pr, pi = (pr + [pr[-1] * pr[0] - pi[-1] * pi[0]], pi + [pr[-1] * pi[0] + pi[-1] * pr[0]])
    return wb.astype(BF16), jnp.concatenate(pr, axis=0), jnp.concatenate(pi, axis=0), wc.astype(BF16)


def _outproj_kernel(x_ref, a_ref, c_ref, s_ref, w_ref, g_ref, wr_ref, br_ref,
                    x1_ref, h2_ref, route_ref, routet_ref, wcnt_ref):
    tm = PROJ_ROWS
    o1, o2 = ATTN_WIDTH, ATTN_WIDTH + CONV_WIDTH
    y = (jnp.dot(a_ref[...], w_ref[0:o1, :], preferred_element_type=F32)
         + jnp.dot(c_ref[...], w_ref[o1:o2, :], preferred_element_type=F32)
         + jnp.dot(s_ref[...], w_ref[o2:, :], preferred_element_type=F32))
    x1 = x_ref[...] + y
    x1_ref[...] = x1
    ms = jnp.mean(x1 * x1, axis=-1, keepdims=True)
    h2 = (x1 * lax.rsqrt(ms + RMS_EPS) * g_ref[...]).astype(BF16)
    h2_ref[...] = h2

    logits = jnp.dot(h2, wr_ref[...], preferred_element_type=F32) + br_ref[...]
    col = lax.broadcasted_iota(jnp.int32, logits.shape, 1)

    def first_max(vals):
        top = jnp.max(vals, axis=-1, keepdims=True)
        return top, jnp.min(jnp.where(vals == top, col, LANES), axis=-1, keepdims=True)

    glog = jnp.where(col < N_GROUPS, logits, NEG_BIG)
    gmax, gidx = first_max(glog)
    gp = 1.0 / jnp.sum(jnp.exp(glog - gmax), axis=-1, keepdims=True)
    lo = N_GROUPS + gidx * EXPERTS_PER_GROUP
    e = jnp.where((col >= lo) & (col < lo + EXPERTS_PER_GROUP), logits, NEG_BIG)
    v1, i1 = first_max(e)
    e = jnp.where(col == i1, NEG_BIG, e)
    v2, i2 = first_max(e)
    ex = jnp.exp(v2 - v1)
    w1 = gp * (1.0 / (1.0 + ex))
    w2 = gp * (ex / (1.0 + ex))

    hit1 = col == i1
    hit2 = col == i2
    onehot = jnp.where(hit1 | hit2, 1.0, 0.0)
    r = lax.broadcasted_iota(jnp.int32, (tm, tm), 0)
    c = lax.broadcasted_iota(jnp.int32, (tm, tm), 1)
    before = jnp.where(r > c, 1.0, 0.0).astype(BF16)
    prior = jnp.dot(before, onehot.astype(BF16), preferred_element_type=F32)
    count = jnp.sum(onehot, axis=0, keepdims=True)
    count = jnp.floor((count + (SUBLANES - 1)) * (1.0 / SUBLANES)) * SUBLANES
    wcnt_ref[0] = count
    run = jnp.broadcast_to(count, (SUBLANES, LANES))
    lane8 = lax.broadcasted_iota(jnp.int32, (SUBLANES, LANES), 1)
    shift = 1
    while shift < LANES:
        run = run + jnp.where(lane8 >= shift, pltpu.roll(run, shift, 1), 0.0)
        shift *= 2
    where_to = prior + (run[0:1, :] - count)
    pos1 = jnp.sum(jnp.where(hit1, where_to, 0.0), axis=-1, keepdims=True)
    pos2 = jnp.sum(jnp.where(hit2, where_to, 0.0), axis=-1, keepdims=True)

    fields = ((i1 - N_GROUPS).astype(F32), (i2 - N_GROUPS).astype(F32), w1, w2, pos1, pos2)
    route = jnp.zeros(logits.shape, F32)
    for k, val in enumerate(fields):
        route = jnp.where(col == k, val, route)
    route_ref[...] = route
    routet_ref[...] = jnp.transpose(route)[0:SUBLANES, :]


def _outproj_route(x2, a, c, s, w_out, g, wr, br):
    T = x2.shape[0]
    tm = PROJ_ROWS
    nt = T // tm
    rows = lambda n: pl.BlockSpec((tm, n), lambda i: (i, 0))
    const = lambda arr: pl.BlockSpec(arr.shape, lambda i: (0, 0))
    return pl.pallas_call(
        _outproj_kernel,
        grid=(nt,),
        in_specs=[rows(D_MODEL), rows(ATTN_WIDTH), rows(CONV_WIDTH), rows(SSM_WIDTH),
                  const(w_out), const(g), const(wr), const(br)],
        out_specs=[rows(D_MODEL), rows(D_MODEL), rows(LANES),
                   pl.BlockSpec((SUBLANES, tm), lambda i: (0, i)),
                   pl.BlockSpec((1, 1, LANES), lambda i: (i, 0, 0))],
        out_shape=[jax.ShapeDtypeStruct((T, D_MODEL), F32), jax.ShapeDtypeStruct((T, D_MODEL), BF16),
                   jax.ShapeDtypeStruct((T, LANES), F32), jax.ShapeDtypeStruct((SUBLANES, T), F32),
                   jax.ShapeDtypeStruct((nt, 1, LANES), F32)],
        compiler_params=_params(("arbitrary",)),
        name="outproj_route",
    )(x2, a, c, s, w_out, g, wr, br)


SLOTS = 2 * PROJ_ROWS + N_EXPERTS * SUBLANES
RUN_MAX = PROJ_ROWS
assert PROJ_ROWS % SUBLANES == 0 and MOE_BLOCK % SUBLANES == 0


def _for_each_piece(n, largest, fn):
    off = jnp.int32(0)
    size = largest
    while size >= SUBLANES:
        take = (n // size) & 1

        @pl.when(take == 1)
        def _(off=off, size=size):
            fn(pl.multiple_of(off, SUBLANES), size)

        off = off + take * size
        size //= 2


def _run_copies(w, n_ref, ls_ref, gs_ref, loc_ref, hbm_ref, sem, to_hbm, act):
    base = w * N_EXPERTS

    def per_expert(e, carry):
        ls = pl.multiple_of(ls_ref[base + e], SUBLANES)
        gs = pl.multiple_of(gs_ref[base + e], SUBLANES)

        def piece(off, size):
            local = loc_ref.at[pl.ds(pl.multiple_of(ls + off, SUBLANES), size), :]
            remote = hbm_ref.at[pl.ds(pl.multiple_of(gs + off, SUBLANES), size), :]
            act(pltpu.make_async_copy(local, remote, sem) if to_hbm
                else pltpu.make_async_copy(remote, local, sem))

        _for_each_piece(n_ref[base + e], RUN_MAX, piece)
        return carry

    lax.fori_loop(0, N_EXPERTS, per_expert, 0)


def _dispatch_kernel(n_ref, ls_ref, gs_ref, pads_ref, padn_ref, nused_ref, h_ref, rt_ref, xs_hbm,
                     loc_ref, zero_ref, sem, zsem, *, n_blocks):
    w = pl.program_id(0)

    @pl.when(w == 0)
    def _():
        zero_ref[...] = jnp.zeros(zero_ref.shape, zero_ref.dtype)

        def fill(act):
            def per_expert(e, carry):
                start = pl.multiple_of(pads_ref[e], SUBLANES)

                def piece(off, size):
                    act(pltpu.make_async_copy(
                        zero_ref.at[pl.ds(0, size), :],
                        xs_hbm.at[pl.ds(pl.multiple_of(start + off, SUBLANES), size), :], zsem))
                _for_each_piece(padn_ref[e], MOE_BLOCK // 2, piece)
                return carry

            def per_block(b, carry):
                first = pl.multiple_of(b * MOE_BLOCK, MOE_BLOCK)
                act(pltpu.make_async_copy(zero_ref, xs_hbm.at[pl.ds(first, MOE_BLOCK), :], zsem))
                return carry

            lax.fori_loop(0, N_EXPERTS, per_expert, 0)
            lax.fori_loop(nused_ref[0], n_blocks, per_block, 0)

        fill(lambda cp: cp.start())
        fill(lambda cp: cp.wait())

    pos1 = rt_ref[4:5, :].astype(jnp.int32)
    pos2 = rt_ref[5:6, :].astype(jnp.int32)
    slot = lax.broadcasted_iota(jnp.int32, (SLOTS, PROJ_ROWS), 0)
    pick = jnp.where((slot == pos1) | (slot == pos2), 1.0, 0.0).astype(BF16)
    loc_ref[...] = jnp.dot(pick, h_ref[...], preferred_element_type=F32)
    _run_copies(w, n_ref, ls_ref, gs_ref, loc_ref, xs_hbm, sem, True, lambda cp: cp.start())
    _run_copies(w, n_ref, ls_ref, gs_ref, loc_ref, xs_hbm, sem, True, lambda cp: cp.wait())


def _dispatch(plan, h2, routet, n_blocks):
    T = h2.shape[0]
    tm = PROJ_ROWS
    return pl.pallas_call(
        functools.partial(_dispatch_kernel, n_blocks=n_blocks),
        grid_spec=pltpu.PrefetchScalarGridSpec(
            num_scalar_prefetch=6,
            grid=(T // tm,),
            in_specs=[pl.BlockSpec((tm, D_MODEL), lambda w, *_: (w, 0)),
                      pl.BlockSpec((SUBLANES, tm), lambda w, *_: (0, w))],
            out_specs=pl.BlockSpec(memory_space=pl.ANY),
            scratch_shapes=[pltpu.VMEM((SLOTS, D_MODEL), F32), pltpu.VMEM((MOE_BLOCK, D_MODEL), F32),
                            pltpu.SemaphoreType.DMA, pltpu.SemaphoreType.DMA],
        ),
        out_shape=jax.ShapeDtypeStruct((n_blocks * MOE_BLOCK, D_MODEL), F32),
        compiler_params=_params(("arbitrary",)),
        name="moe_dispatch",
    )(plan["n"], plan["ls"], plan["gs"], plan["pad_start"], plan["pad_len"], plan["nused"], h2, routet)


def _expert_kernel(blk_e_ref, nused_ref, x_ref, wg_ref, wu_ref, wd_ref, o_ref):
    b = pl.program_id(0)

    @pl.when(b < nused_ref[0])
    def _():
        x = x_ref[...].astype(BF16)
        gate = jnp.dot(x, wg_ref[0], preferred_element_type=F32)
        up = jnp.dot(x, wu_ref[0], preferred_element_type=F32)
        act = (jax.nn.silu(gate) * up).astype(BF16)
        o_ref[...] = jnp.dot(act, wd_ref[0], preferred_element_type=F32)

    @pl.when(b >= nused_ref[0])
    def _():
        o_ref[...] = jnp.zeros(o_ref.shape, o_ref.dtype)


def _experts(plan, xs, wg, wu, wd, n_blocks):
    return pl.pallas_call(
        _expert_kernel,
        grid_spec=pltpu.PrefetchScalarGridSpec(
            num_scalar_prefetch=2,
            grid=(n_blocks,),
            in_specs=[pl.BlockSpec((MOE_BLOCK, D_MODEL), lambda b, be, nu: (b, 0)),
                      pl.BlockSpec((1, D_MODEL, EXPERT_FF), lambda b, be, nu: (be[b], 0, 0)),
                      pl.BlockSpec((1, D_MODEL, EXPERT_FF), lambda b, be, nu: (be[b], 0, 0)),
                      pl.BlockSpec((1, EXPERT_FF, D_MODEL), lambda b, be, nu: (be[b], 0, 0))],
            out_specs=pl.BlockSpec((MOE_BLOCK, D_MODEL), lambda b, be, nu: (b, 0)),
        ),
        out_shape=jax.ShapeDtypeStruct((n_blocks * MOE_BLOCK, D_MODEL), F32),
        compiler_params=_params(("arbitrary",)),
        name="moe_experts",
    )(plan["blk_e"], plan["nused"], xs, wg, wu, wd)


def _combine_kernel(n_ref, ls_ref, gs_ref, tot_ref, yb_hbm, x_ref, route_ref, g_ref, o_ref, loc_ref, sem,
                    *, final_norm):
    w = pl.program_id(0)

    def clear(r, carry):
        loc_ref[pl.ds(pl.multiple_of(r * SUBLANES, SUBLANES), SUBLANES), :] = jnp.zeros((SUBLANES, D_MODEL), F32)
        return carry

    lax.fori_loop(tot_ref[w] // SUBLANES, SLOTS // SUBLANES, clear, 0)
    _run_copies(w, n_ref, ls_ref, gs_ref, loc_ref, yb_hbm, sem, False, lambda cp: cp.start())
    _run_copies(w, n_ref, ls_ref, gs_ref, loc_ref, yb_hbm, sem, False, lambda cp: cp.wait())
    route = route_ref[...]
    pos1 = route[:, 4:5].astype(jnp.int32)
    pos2 = route[:, 5:6].astype(jnp.int32)
    slot = lax.broadcasted_iota(jnp.int32, (PROJ_ROWS, SLOTS), 1)
    sel = jnp.where(slot == pos1, route[:, 2:3], 0.0) + jnp.where(slot == pos2, route[:, 3:4], 0.0)
    sel_hi = sel.astype(BF16)
    sel_lo = (sel - sel_hi.astype(F32)).astype(BF16)
    y = loc_ref[...]
    y_hi = y.astype(BF16)
    y_lo = (y - y_hi.astype(F32)).astype(BF16)
    moe = (jnp.dot(sel_hi, y_hi, preferred_element_type=F32)
           + jnp.dot(sel_lo, y_hi, preferred_element_type=F32)
           + jnp.dot(sel_hi, y_lo, preferred_element_type=F32))
    x = x_ref[...] + moe
    if final_norm:
        ms = jnp.mean(x * x, axis=-1, keepdims=True)
        x = x * lax.rsqrt(ms + RMS_EPS) * g_ref[...]
    o_ref[...] = x


def _combine(plan, yb, x1, route, g, final_norm):
    T = x1.shape[0]
    tm = PROJ_ROWS
    return pl.pallas_call(
        functools.partial(_combine_kernel, final_norm=final_norm),
        grid_spec=pltpu.PrefetchScalarGridSpec(
            num_scalar_prefetch=4,
            grid=(T // tm,),
            in_specs=[pl.BlockSpec(memory_space=pl.ANY),
                      pl.BlockSpec((tm, D_MODEL), lambda w, *_: (w, 0)),
                      pl.BlockSpec((tm, LANES), lambda w, *_: (w, 0)),
                      pl.BlockSpec((1, D_MODEL), lambda w, *_: (0, 0))],
            out_specs=pl.BlockSpec((tm, D_MODEL), lambda w, *_: (w, 0)),
            scratch_shapes=[pltpu.VMEM((SLOTS, D_MODEL), F32), pltpu.SemaphoreType.DMA],
        ),
        out_shape=jax.ShapeDtypeStruct((T, D_MODEL), F32),
        compiler_params=_params(("arbitrary",)),
        name="moe_combine",
    )(plan["n"], plan["ls"], plan["gs"], plan["tot"], yb, x1, route, g)


def _routing_plan(wcnt, T):
    n = wcnt[:, 0, N_GROUPS:N_GROUPS + N_EXPERTS].astype(jnp.int32)
    cnt = jnp.sum(n, axis=0)
    nblk = (cnt + MOE_BLOCK - 1) // MOE_BLOCK
    blk_end = jnp.cumsum(nblk)
    first_row = (blk_end - nblk) * MOE_BLOCK
    ls = jnp.cumsum(n, axis=1) - n
    gs = first_row[None, :] + jnp.cumsum(n, axis=0) - n
    worst_rows = T * 2 + (T // PROJ_ROWS) * N_EXPERTS * SUBLANES
    n_blocks = -(-worst_rows // MOE_BLOCK) + N_EXPERTS
    blk_e = jnp.minimum(jnp.sum(jnp.arange(n_blocks)[:, None] >= blk_end[None, :], axis=1), N_EXPERTS - 1)
    i32 = lambda a: a.reshape(-1).astype(jnp.int32)
    plan = dict(n=i32(n), ls=i32(ls), gs=i32(gs), tot=i32(jnp.sum(n, axis=1)), pad_start=i32(first_row + cnt),
                pad_len=i32(nblk * MOE_BLOCK - cnt), nused=i32(blk_end[-1:]), blk_e=i32(blk_e))
    return plan, n_blocks


def _pack_w_in(w):
    scale = QK_DIM ** -0.5 * LOG2E
    q1, q2, k1, k2 = (w[:, i * QK_COLS:(i + 1) * QK_COLS].reshape(D_MODEL, N_HEADS, QK_DIM) for i in range(4))
    qq = (jnp.concatenate([q1, q2], axis=-1) * scale).reshape(D_MODEL, ATTN_WIDTH)
    kk = jnp.concatenate([k1, k2], axis=-1).reshape(D_MODEL, ATTN_WIDTH)
    v0 = 4 * QK_COLS
    packed = jnp.concatenate([qq, kk, w[:, v0 + ATTN_WIDTH:]], axis=1).astype(BF16)
    return packed, jnp.transpose(w[:, v0:v0 + ATTN_WIDTH]).astype(BF16)


def kernel(x, rel_bias, ln1_g, w_in, lam_q1, lam_k1, lam_q2, lam_k2, subln_g, conv_w, conv_b, conv_ln_g, conv_ln_b, ssm_lam_re, ssm_lam_im, ssm_log_dt, ssm_b_re, ssm_b_im, ssm_c_re, ssm_c_im, ssm_d, ssm_glu_w, ssm_glu_b, w_out, ln2_g, group_router_w, group_router_b, expert_router_w, expert_router_b, w_gate, w_up, w_down, final_g):
    B, L, D = x.shape
    T = B * L
    depth = w_in.shape[0]
    assert D == D_MODEL and L % CONV_ROWS == 0 and L % ATTN_TILE == 0 and L % SSM_ROWS == 0
    assert PROJ_ROWS == ATTN_TILE and T % PROJ_ROWS == 0 and (2 * T) % MOE_BLOCK == 0
    x2 = x.reshape(T, D)
    bias_diag, bias_sub = _bias_tiles(rel_bias)
    row = lambda v: v.astype(F32).reshape(1, -1)
    for l in range(depth):
        lam_init = 0.8 - 0.6 * math.exp(-0.3 * l)
        lam = (jnp.exp(jnp.sum(lam_q1[l].astype(F32) * lam_k1[l].astype(F32)))
               - jnp.exp(jnp.sum(lam_q2[l].astype(F32) * lam_k2[l].astype(F32))) + lam_init).reshape(1)
        qq, kk, conv_in, ssm_in, vt = _inproj(x2, row(ln1_g[l]), *_pack_w_in(w_in[l]))
        a = _attention(qq, kk, vt, lam, bias_diag, bias_sub, subln_g[l].astype(F32).reshape(V_DIM, 1),
                       1.0 - lam_init, B, L)
        c = _conformer_conv(conv_in, conv_w[l], row(conv_b[l]), row(conv_ln_g[l]), row(conv_ln_b[l]), B, L)
        wb, apr, api, wc = _ssm_weights(ssm_lam_re[l], ssm_lam_im[l], ssm_log_dt[l], ssm_b_re[l], ssm_b_im[l],
                                        ssm_c_re[l], ssm_c_im[l])
        s = _s5_ssm(ssm_in, wb, apr, api, wc, row(ssm_d[l]), ssm_glu_w[l].astype(BF16), row(ssm_glu_b[l]), B, L)
        wr = jnp.zeros((D, LANES), F32).at[:, :N_GROUPS].set(group_router_w[l]) \
            .at[:, N_GROUPS:N_GROUPS + N_EXPERTS].set(expert_router_w[l]).astype(BF16)
        br = jnp.zeros((1, LANES), F32).at[0, :N_GROUPS].set(group_router_b[l]) \
            .at[0, N_GROUPS:N_GROUPS + N_EXPERTS].set(expert_router_b[l])
        x1, h2, route, routet, wcnt = _outproj_route(x2, a, c, s, w_out[l].astype(BF16), row(ln2_g[l]), wr, br)
        plan, n_blocks = _routing_plan(wcnt, T)
        xs = _dispatch(plan, h2, routet, n_blocks)
        yb = _experts(plan, xs, w_gate[l].astype(BF16), w_up[l].astype(BF16), w_down[l].astype(BF16), n_blocks)
        x2 = _combine(plan, yb, x1, route, row(final_g), final_norm=(l == depth - 1))
    return x2.reshape(B, L, D)
```

```python
import functools
import math

import jax
import jax.numpy as jnp
from jax import lax
from jax.experimental import pallas as pl
from jax.experimental.pallas import tpu as pltpu

F32 = jnp.float32
BF16 = jnp.bfloat16

D_MODEL = 1024
N_HEADS = 4
QK_DIM = 64
V_DIM = 128
ATTN_WIDTH = N_HEADS * V_DIM
QK_COLS = N_HEADS * QK_DIM
CONV_WIDTH = 256
CONV_TAPS = 31
SSM_WIDTH = 256
SSM_GROUP = 16
SSM_GROUPS = 16
SSM_STATE = 64
SSM_LANES = SSM_GROUPS * SSM_STATE
REL_BUCKETS = 32
REL_MAX_EXACT = 16
REL_MAX_DIST = 128
N_GROUPS = 4
EXPERTS_PER_GROUP = 8
N_EXPERTS = N_GROUPS * EXPERTS_PER_GROUP
EXPERT_FF = 512
RMS_EPS = 1e-6
LN_EPS = 1e-5
NEG_BIG = -1e30

LANES = 128
SUBLANES = 8
VMEM_LIMIT = 48 * 1024 * 1024

PROJ_ROWS = 512
ATTN_TILE = 512
ATTN_CHUNK = 32
ONES_ROWS = 16
LOG2E = math.log2(math.e)
CONV_ROWS = 512
CONV_HALO = 32
SSM_ROWS = 256
SSM_UNROLL = 8
MOE_BLOCK = 512


def _params(sem):
    return pltpu.CompilerParams(dimension_semantics=sem, vmem_limit_bytes=VMEM_LIMIT)


def _inproj_kernel(x_ref, g_ref, w_ref, wvt_ref, qq_ref, kk_ref, conv_ref, ssm_ref, vt_ref):
    x = x_ref[...]
    ms = jnp.mean(x * x, axis=-1, keepdims=True)
    h = (x * lax.rsqrt(ms + RMS_EPS) * g_ref[...]).astype(BF16)
    o = 0
    for ref in (qq_ref, kk_ref, conv_ref, ssm_ref):
        n = ref.shape[-1]
        ref[...] = jnp.dot(h, w_ref[:, o:o + n], preferred_element_type=F32).astype(ref.dtype)
        o += n
    vt_ref[0] = lax.dot_general(wvt_ref[...], h, (((1,), (1,)), ((), ())),
                                preferred_element_type=F32).astype(vt_ref.dtype)


def _inproj(x2, g, w, wvt):
    T = x2.shape[0]
    tm = PROJ_ROWS
    widths = (ATTN_WIDTH, ATTN_WIDTH, 2 * CONV_WIDTH, SSM_WIDTH)
    dtypes = (BF16, BF16, F32, F32)
    return pl.pallas_call(
        _inproj_kernel,
        grid=(T // tm,),
        in_specs=[
            pl.BlockSpec((tm, D_MODEL), lambda i: (i, 0)),
            pl.BlockSpec((1, D_MODEL), lambda i: (0, 0)),
            pl.BlockSpec(w.shape, lambda i: (0, 0)),
            pl.BlockSpec(wvt.shape, lambda i: (0, 0)),
        ],
        out_specs=[pl.BlockSpec((tm, n), lambda i: (i, 0)) for n in widths]
        + [pl.BlockSpec((1, ATTN_WIDTH, tm), lambda i: (i, 0, 0))],
        out_shape=[jax.ShapeDtypeStruct((T, n), dt) for n, dt in zip(widths, dtypes)]
        + [jax.ShapeDtypeStruct((T // tm, ATTN_WIDTH, tm), BF16)],
        compiler_params=_params(("arbitrary",)),
        name="inproj",
    )(x2, g, w, wvt)


def _attn_kernel(lam_ref, q_ref, k_ref, vt_ref, bd_ref, bs_ref, g_ref, o_ref,
                 qs_ref, m_ref, acc_ref, sa_ref, sb_ref, p_ref, *, out_scale):
    t = ATTN_TILE
    qi = pl.program_id(2)
    q = q_ref[...].astype(F32)
    lane = lax.broadcasted_iota(jnp.int32, q.shape, 1)
    qs_ref[0:t, :] = jnp.where(lane < QK_DIM, q, 0.0).astype(BF16)
    qs_ref[t:2 * t, :] = jnp.where(lane >= QK_DIM, q, 0.0).astype(BF16)
    m_ref[...] = jnp.full(m_ref.shape, NEG_BIG, F32)
    acc_ref[...] = jnp.zeros(acc_ref.shape, F32)

    chunks = [(c, c + ATTN_CHUNK) for c in range(0, t, ATTN_CHUNK)]
    fold = lambda a: a.reshape(ATTN_CHUNK // SUBLANES, SUBLANES, 2 * t)

    def scores(j, s_ref):
        k = k_ref[pl.ds(pl.multiple_of(j * t, t), t), :]
        s_ref[...] = lax.dot_general(k, qs_ref[...], (((1,), (1,)), ((), ())), preferred_element_type=F32)

    def accumulate(j, s_ref, bias_ref=None):
        top = jnp.full((SUBLANES, 2 * t), NEG_BIG, F32)
        for lo, hi in chunks:
            s = s_ref[lo:hi, :]
            if bias_ref is not None:
                b = bias_ref[0, lo:hi, :]
                s = s + jnp.concatenate([b, b], axis=1)
                s_ref[lo:hi, :] = s
            top = jnp.maximum(top, jnp.max(fold(s), axis=0))
        m_prev = m_ref[...]
        m_new = jnp.maximum(m_prev, jnp.max(top, axis=0, keepdims=True))
        alpha = jnp.exp2(m_prev - m_new)
        for lo, hi in chunks:
            p_ref[lo:hi, :] = jnp.exp2(s_ref[lo:hi, :] - m_new).astype(BF16)
        lhs = jnp.concatenate([vt_ref[j], jnp.ones((ONES_ROWS, t), BF16)], axis=0)
        acc_ref[...] = alpha * acc_ref[...] + jnp.dot(lhs, p_ref[...], preferred_element_type=F32)
        m_ref[...] = m_new

    nfar = jnp.maximum(qi - 1, 0)
    scores(0, sa_ref)

    def far_pair(i, carry):
        scores(2 * i + 1, sb_ref)
        accumulate(2 * i, sa_ref)
        scores(2 * i + 2, sa_ref)
        accumulate(2 * i + 1, sb_ref)
        return carry

    lax.fori_loop(0, nfar // 2, far_pair, 0)

    @pl.when(qi == 0)
    def _():
        accumulate(qi, sa_ref, bd_ref)

    @pl.when((qi >= 1) & (nfar % 2 == 0))
    def _():
        scores(qi, sb_ref)
        accumulate(qi - 1, sa_ref, bs_ref)
        accumulate(qi, sb_ref, bd_ref)

    @pl.when(nfar % 2 == 1)
    def _():
        scores(qi - 1, sb_ref)
        accumulate(qi - 2, sa_ref)
        scores(qi, sa_ref)
        accumulate(qi - 1, sb_ref, bs_ref)
        accumulate(qi, sa_ref, bd_ref)

    acc = acc_ref[0:V_DIM, :]
    l = acc_ref[V_DIM:V_DIM + 1, :]
    a = acc[:, 0:t] / l[:, 0:t] - lam_ref[0] * (acc[:, t:2 * t] / l[:, t:2 * t])
    ms = jnp.mean(a * a, axis=0, keepdims=True)
    y = a * lax.rsqrt(ms + RMS_EPS) * g_ref[...] * out_scale
    o_ref[...] = jnp.transpose(y).astype(o_ref.dtype)


def _attention(qq, kk, vt, lam, bias_diag, bias_sub, subln_g, out_scale, B, L):
    T = B * L
    t = ATTN_TILE
    nq = L // t
    return pl.pallas_call(
        functools.partial(_attn_kernel, out_scale=out_scale),
        grid=(B, N_HEADS, nq),
        in_specs=[
            pl.BlockSpec(memory_space=pltpu.SMEM),
            pl.BlockSpec((t, LANES), lambda b, h, i: (b * nq + i, h)),
            pl.BlockSpec((L, LANES), lambda b, h, i: (b, h)),
            pl.BlockSpec((nq, V_DIM, t), lambda b, h, i: (b, h, 0)),
            pl.BlockSpec((1, t, t), lambda b, h, i: (h, 0, 0)),
            pl.BlockSpec((1, t, t), lambda b, h, i: (h, 0, 0)),
            pl.BlockSpec((V_DIM, 1), lambda b, h, i: (0, 0)),
        ],
        out_specs=pl.BlockSpec((t, LANES), lambda b, h, i: (b * nq + i, h)),
        out_shape=jax.ShapeDtypeStruct((T, ATTN_WIDTH), BF16),
        scratch_shapes=[
            pltpu.VMEM((2 * t, LANES), BF16),
            pltpu.VMEM((1, 2 * t), F32),
            pltpu.VMEM((V_DIM + ONES_ROWS, 2 * t), F32),
            pltpu.VMEM((t, 2 * t), F32),
            pltpu.VMEM((t, 2 * t), F32),
            pltpu.VMEM((t, 2 * t), BF16),
        ],
        compiler_params=_params(("arbitrary", "arbitrary", "arbitrary")),
        name="diff_attn",
    )(lam, qq, kk, vt, bias_diag, bias_sub, subln_g)


def _rel_bucket(rel):
    n = jnp.maximum(rel, 0)
    nf = jnp.maximum(n, 1).astype(F32)
    large = REL_MAX_EXACT + (jnp.log(nf / REL_MAX_EXACT) / math.log(REL_MAX_DIST / REL_MAX_EXACT)
                             * (REL_BUCKETS - REL_MAX_EXACT)).astype(jnp.int32)
    large = jnp.minimum(large, REL_BUCKETS - 1)
    return jnp.where(n < REL_MAX_EXACT, n, large)


def _bias_tiles(rel_table):
    t = ATTN_TILE
    assert t >= REL_MAX_DIST
    far = rel_table[REL_BUCKETS - 1].astype(F32)
    rel_d = jnp.arange(t)[None, :] - jnp.arange(t)[:, None]

    def lookup(bucket):
        out = jnp.zeros((N_HEADS,) + bucket.shape, F32)
        for b in range(REL_BUCKETS):
            out = jnp.where((bucket == b)[None], (rel_table[b].astype(F32) - far)[:, None, None], out)
        return out * LOG2E

    bd = jnp.where((rel_d >= 0)[None], lookup(_rel_bucket(rel_d)), NEG_BIG)
    return bd, lookup(_rel_bucket(rel_d + t))


def _conv_kernel(u_ref, w_ref, b_ref, g_ref, beta_ref, o_ref, h_ref):
    tt = CONV_ROWS
    j = pl.program_id(1)

    @pl.when(j == 0)
    def _():
        h_ref[0:CONV_HALO, :] = jnp.zeros((CONV_HALO, CONV_WIDTH), F32)

    @pl.when(j > 0)
    def _():
        h_ref[0:CONV_HALO, :] = h_ref[tt:tt + CONV_HALO, :]

    u = u_ref[...]
    h_ref[CONV_HALO:CONV_HALO + tt, :] = u[:, 0:CONV_WIDTH] * jax.nn.sigmoid(u[:, CONV_WIDTH:])
    acc = jnp.broadcast_to(b_ref[...], (tt, CONV_WIDTH))
    off = CONV_HALO - (CONV_TAPS - 1)
    for k in range(CONV_TAPS):
        acc = acc + w_ref[k:k + 1, :] * h_ref[off + k:off + k + tt, :]
    mu = jnp.mean(acc, axis=-1, keepdims=True)
    cen = acc - mu
    var = jnp.mean(cen * cen, axis=-1, keepdims=True)
    y = cen * lax.rsqrt(var + LN_EPS) * g_ref[...] + beta_ref[...]
    o_ref[...] = jax.nn.silu(y).astype(o_ref.dtype)


def _conformer_conv(conv_in, w, b, g, beta, B, L):
    T = B * L
    tt = CONV_ROWS
    nt = L // tt
    vec = pl.BlockSpec((1, CONV_WIDTH), lambda bb, j: (0, 0))
    return pl.pallas_call(
        _conv_kernel,
        grid=(B, nt),
        in_specs=[
            pl.BlockSpec((tt, 2 * CONV_WIDTH), lambda bb, j: (bb * nt + j, 0)),
            pl.BlockSpec((CONV_TAPS, CONV_WIDTH), lambda bb, j: (0, 0)),
            vec, vec, vec,
        ],
        out_specs=pl.BlockSpec((tt, CONV_WIDTH), lambda bb, j: (bb * nt + j, 0)),
        out_shape=jax.ShapeDtypeStruct((T, CONV_WIDTH), BF16),
        scratch_shapes=[pltpu.VMEM((tt + CONV_HALO, CONV_WIDTH), F32)],
        compiler_params=_params(("arbitrary", "arbitrary")),
        name="conformer_conv",
    )(conv_in, w, b, g, beta)


def _ssm_kernel(u_ref, wb_ref, apr_ref, api_ref, wc_ref, d_ref, gw_ref, gb_ref, o_ref,
                xs_ref, carry_ref):
    tt = SSM_ROWS
    n = SSM_LANES
    j = pl.program_id(1)

    @pl.when(j == 0)
    def _():
        carry_ref[...] = jnp.zeros(carry_ref.shape, F32)

    u = u_ref[...]
    xs_ref[...] = jnp.dot(u.astype(BF16), wb_ref[...], preferred_element_type=F32)
    apr = apr_ref[...]
    api = api_ref[...]
    row = lax.broadcasted_iota(jnp.int32, (SUBLANES, n), 0)

    def block(r, carry):
        cr, ci = carry
        start = pl.multiple_of(r * SUBLANES, SUBLANES)
        xr = xs_ref[pl.ds(start, SUBLANES), 0:n]
        xi = xs_ref[pl.ds(start, SUBLANES), n:2 * n]
        for shift in (1, 2, 4):
            ar = apr[shift - 1:shift, :]
            ai = api[shift - 1:shift, :]
            sr = jnp.where(row >= shift, pltpu.roll(xr, shift, 0), 0.0)
            si = jnp.where(row >= shift, pltpu.roll(xi, shift, 0), 0.0)
            xr, xi = xr + (ar * sr - ai * si), xi + (ar * si + ai * sr)
        xr, xi = xr + (apr * cr - api * ci), xi + (apr * ci + api * cr)
        xs_ref[pl.ds(start, SUBLANES), 0:n] = xr
        xs_ref[pl.ds(start, SUBLANES), n:2 * n] = xi
        return xr[SUBLANES - 1:SUBLANES, :], xi[SUBLANES - 1:SUBLANES, :]

    cr, ci = lax.fori_loop(0, tt // SUBLANES, block, (carry_ref[0:1, :], carry_ref[1:2, :]),
                           unroll=SSM_UNROLL)
    carry_ref[0:1, :] = cr
    carry_ref[1:2, :] = ci

    y = jnp.dot(xs_ref[...].astype(BF16), wc_ref[...], preferred_element_type=F32) + u * d_ref[...]
    g = jax.nn.gelu(y)
    z = jnp.dot(g.astype(BF16), gw_ref[...], preferred_element_type=F32) + gb_ref[...]
    o_ref[...] = (g * jax.nn.sigmoid(z)).astype(o_ref.dtype)


def _s5_ssm(ssm_in, wb, apr, api, wc, d, gw, gb, B, L):
    T = B * L
    tt = SSM_ROWS
    nt = L // tt
    const = lambda a: pl.BlockSpec(a.shape, lambda bb, j: (0, 0))
    return pl.pallas_call(
        _ssm_kernel,
        grid=(B, nt),
        in_specs=[pl.BlockSpec((tt, SSM_WIDTH), lambda bb, j: (bb * nt + j, 0)),
                  const(wb), const(apr), const(api), const(wc), const(d), const(gw), const(gb)],
        out_specs=pl.BlockSpec((tt, SSM_WIDTH), lambda bb, j: (bb * nt + j, 0)),
        out_shape=jax.ShapeDtypeStruct((T, SSM_WIDTH), BF16),
        scratch_shapes=[pltpu.VMEM((tt, 2 * SSM_LANES), F32), pltpu.VMEM((SUBLANES, SSM_LANES), F32)],
        compiler_params=_params(("arbitrary", "arbitrary")),
        name="s5_scan",
    )(ssm_in, wb, apr, api, wc, d, gw, gb)


def _ssm_weights(lam_re, lam_im, log_dt, b_re, b_im, c_re, c_im):
    G, P, H = SSM_GROUPS, SSM_STATE, SSM_GROUP
    dt = jnp.exp(log_dt.astype(F32))[:, None]
    lr, li = lam_re.astype(F32), lam_im.astype(F32)
    mag = jnp.exp(lr * dt)
    ar, ai = mag * jnp.cos(li * dt), mag * jnp.sin(li * dt)
    den = lr * lr + li * li
    zr = ((ar - 1.0) * lr + ai * li) / den
    zi = (ai * lr - (ar - 1.0) * li) / den
    bre, bim = b_re.astype(F32), b_im.astype(F32)
    bbr = zr[..., None] * bre - zi[..., None] * bim
    bbi = zr[..., None] * bim + zi[..., None] * bre
    eye = jnp.eye(G, dtype=F32)
    wb = jnp.concatenate([jnp.einsum('gph,gk->ghkp', bbr, eye).reshape(G * H, G * P),
                          jnp.einsum('gph,gk->ghkp', bbi, eye).reshape(G * H, G * P)], axis=1)
    wc = jnp.concatenate([jnp.einsum('ghp,gk->gpkh', c_re.astype(F32), eye).reshape(G * P, G * H),
                          -jnp.einsum('ghp,gk->gpkh', c_im.astype(F32), eye).reshape(G * P, G * H)], axis=0)
    pr, pi = [ar.reshape(1, G * P)], [ai.reshape(1, G * P)]
    for _ in range(SUBLANES - 1):
        pr, pi = (pr + [pr[-1] * pr[0] - pi[-1] * pi[0]], pi + [pr[-1] * pi[0] + pi[-1] * pr[0]])
    return wb.astype(BF16), jnp.concatenate(pr, axis=0), jnp.concatenate(pi, axis=0), wc.astype(BF16)


def _outproj_kernel(x_ref, a_ref, c_ref, s_ref, w_ref, g_ref, wr_ref, br_ref,
                    x1_ref, h2_ref, route_ref, routet_ref, wcnt_ref):
    tm = PROJ_ROWS
    o1, o2 = ATTN_WIDTH, ATTN_WIDTH + CONV_WIDTH
    y = (jnp.dot(a_ref[...], w_ref[0:o1, :], preferred_element_type=F32)
         + jnp.dot(c_ref[...], w_ref[o1:o2, :], preferred_element_type=F32)
         + jnp.dot(s_ref[...], w_ref[o2:, :], preferred_element_type=F32))
    x1 = x_ref[...] + y
    x1_ref[...] = x1
    ms = jnp.mean(x1 * x1, axis=-1, keepdims=True)
    h2 = (x1 * lax.rsqrt(ms + RMS_EPS) * g_ref[...]).astype(BF16)
    h2_ref[...] = h2

    logits = jnp.dot(h2, wr_ref[...], preferred_element_type=F32) + br_ref[...]
    col = lax.broadcasted_iota(jnp.int32, logits.shape, 1)

    def first_max(vals):
        top = jnp.max(vals, axis=-1, keepdims=True)
        return top, jnp.min(jnp.where(vals == top, col, LANES), axis=-1, keepdims=True)

    glog = jnp.where(col < N_GROUPS, logits, NEG_BIG)
    gmax, gidx = first_max(glog)
    gp = 1.0 / jnp.sum(jnp.exp(glog - gmax), axis=-1, keepdims=True)
    lo = N_GROUPS + gidx * EXPERTS_PER_GROUP
    e = jnp.where((col >= lo) & (col < lo + EXPERTS_PER_GROUP), logits, NEG_BIG)
    v1, i1 = first_max(e)
    e = jnp.where(col == i1, NEG_BIG, e)
    v2, i2 = first_max(e)
    ex = jnp.exp(v2 - v1)
    w1 = gp * (1.0 / (1.0 + ex))
    w2 = gp * (ex / (1.0 + ex))

    hit1 = col == i1
    hit2 = col == i2
    onehot = jnp.where(hit1 | hit2, 1.0, 0.0)
    r = lax.broadcasted_iota(jnp.int32, (tm, tm), 0)
    c = lax.broadcasted_iota(jnp.int32, (tm, tm), 1)
    before = jnp.where(r > c, 1.0, 0.0).astype(BF16)
    prior = jnp.dot(before, onehot.astype(BF16), preferred_element_type=F32)
    count = jnp.sum(onehot, axis=0, keepdims=True)
    count = jnp.floor((count + (SUBLANES - 1)) * (1.0 / SUBLANES)) * SUBLANES
    wcnt_ref[0] = count
    run = jnp.broadcast_to(count, (SUBLANES, LANES))
    lane8 = lax.broadcasted_iota(jnp.int32, (SUBLANES, LANES), 1)
    shift = 1
    while shift < LANES:
        run = run + jnp.where(lane8 >= shift, pltpu.roll(run, shift, 1), 0.0)
        shift *= 2
    where_to = prior + (run[0:1, :] - count)
    pos1 = jnp.sum(jnp.where(hit1, where_to, 0.0), axis=-1, keepdims=True)
    pos2 = jnp.sum(jnp.where(hit2, where_to, 0.0), axis=-1, keepdims=True)

    fields = ((i1 - N_GROUPS).astype(F32), (i2 - N_GROUPS).astype(F32), w1, w2, pos1, pos2)
    route = jnp.zeros(logits.shape, F32)
    for k, val in enumerate(fields):
        route = jnp.where(col == k, val, route)
    route_ref[...] = route
    routet_ref[...] = jnp.transpose(route)[0:SUBLANES, :]


def _outproj_route(x2, a, c, s, w_out, g, wr, br):
    T = x2.shape[0]
    tm = PROJ_ROWS
    nt = T // tm
    rows = lambda n: pl.BlockSpec((tm, n), lambda i: (i, 0))
    const = lambda arr: pl.BlockSpec(arr.shape, lambda i: (0, 0))
    return pl.pallas_call(
        _outproj_kernel,
        grid=(nt,),
        in_specs=[rows(D_MODEL), rows(ATTN_WIDTH), rows(CONV_WIDTH), rows(SSM_WIDTH),
                  const(w_out), const(g), const(wr), const(br)],
        out_specs=[rows(D_MODEL), rows(D_MODEL), rows(LANES),
                   pl.BlockSpec((SUBLANES, tm), lambda i: (0, i)),
                   pl.BlockSpec((1, 1, LANES), lambda i: (i, 0, 0))],
        out_shape=[jax.ShapeDtypeStruct((T, D_MODEL), F32), jax.ShapeDtypeStruct((T, D_MODEL), BF16),
                   jax.ShapeDtypeStruct((T, LANES), F32), jax.ShapeDtypeStruct((SUBLANES, T), F32),
                   jax.ShapeDtypeStruct((nt, 1, LANES), F32)],
        compiler_params=_params(("arbitrary",)),
        name="outproj_route",
    )(x2, a, c, s, w_out, g, wr, br)


SLOTS = 2 * PROJ_ROWS + N_EXPERTS * SUBLANES
TILES = SLOTS // SUBLANES
assert PROJ_ROWS % SUBLANES == 0 and MOE_BLOCK % SUBLANES == 0


def _for_each_piece(n, largest, fn):
    off = jnp.int32(0)
    size = largest
    while size >= SUBLANES:
        take = (n // size) & 1

        @pl.when(take == 1)
        def _(off=off, size=size):
            fn(pl.multiple_of(off, SUBLANES), size)

        off = off + take * size
        size //= 2


def _tile_copy(buf_ref, tile, hbm_ref, hbm_tile, sem, to_hbm):
    local = buf_ref.at[pl.ds(pl.multiple_of(tile * SUBLANES, SUBLANES), SUBLANES), :]
    remote = hbm_ref.at[pl.ds(pl.multiple_of(hbm_tile * SUBLANES, SUBLANES), SUBLANES), :]
    return pltpu.make_async_copy(local, remote, sem) if to_hbm else pltpu.make_async_copy(remote, local, sem)


def _start_tiles(count, table_ref, buf_ref, hbm_ref, sem, to_hbm):
    def body(c, carry):
        _tile_copy(buf_ref, c, hbm_ref, table_ref[0, 0, c], sem, to_hbm).start()
        return carry

    lax.fori_loop(0, count, body, 0)


def _wait_tiles(count, buf_ref, hbm_ref, sem, to_hbm):
    def body(c, carry):
        _tile_copy(buf_ref, 0, hbm_ref, 0, sem, to_hbm).wait()
        return carry

    lax.fori_loop(0, count, body, 0)


def _dispatch_kernel(tot_ref, pads_ref, padn_ref, nused_ref, dst_ref, h_ref, rt_ref, xs_hbm,
                     loc_ref, zero_ref, sem, zsem, *, n_blocks):
    w = pl.program_id(0)
    last = pl.num_programs(0) - 1

    @pl.when(w == 0)
    def _():
        zero_ref[...] = jnp.zeros(zero_ref.shape, zero_ref.dtype)

        def fill(act):
            def per_expert(e, carry):
                start = pl.multiple_of(pads_ref[e], SUBLANES)

                def piece(off, size):
                    act(pltpu.make_async_copy(
                        zero_ref.at[pl.ds(0, size), :],
                        xs_hbm.at[pl.ds(pl.multiple_of(start + off, SUBLANES), size), :], zsem))
                _for_each_piece(padn_ref[e], MOE_BLOCK // 2, piece)
                return carry

            def per_block(b, carry):
                first = pl.multiple_of(b * MOE_BLOCK, MOE_BLOCK)
                act(pltpu.make_async_copy(zero_ref, xs_hbm.at[pl.ds(first, MOE_BLOCK), :], zsem))
                return carry

            lax.fori_loop(0, N_EXPERTS, per_expert, 0)
            lax.fori_loop(nused_ref[0], n_blocks, per_block, 0)

        fill(lambda cp: cp.start())
        fill(lambda cp: cp.wait())

    pos1 = rt_ref[4:5, :].astype(jnp.int32)
    pos2 = rt_ref[5:6, :].astype(jnp.int32)
    slot = lax.broadcasted_iota(jnp.int32, (SLOTS, PROJ_ROWS), 0)
    pick = jnp.where((slot == pos1) | (slot == pos2), 1.0, 0.0).astype(BF16)
    buf = loc_ref.at[w % 2]
    buf[...] = jnp.dot(pick, h_ref[...], preferred_element_type=F32)

    @pl.when(w > 0)
    def _():
        _wait_tiles(tot_ref[jnp.maximum(w - 1, 0)] // SUBLANES, buf, xs_hbm, sem, True)

    _start_tiles(tot_ref[w] // SUBLANES, dst_ref, buf, xs_hbm, sem, True)

    @pl.when(w == last)
    def _():
        _wait_tiles(tot_ref[w] // SUBLANES, buf, xs_hbm, sem, True)


def _dispatch(plan, h2, routet, n_blocks):
    T = h2.shape[0]
    tm = PROJ_ROWS
    return pl.pallas_call(
        functools.partial(_dispatch_kernel, n_blocks=n_blocks),
        grid_spec=pltpu.PrefetchScalarGridSpec(
            num_scalar_prefetch=4,
            grid=(T // tm,),
            in_specs=[pl.BlockSpec((1, 1, TILES), lambda w, *_: (w, 0, 0), memory_space=pltpu.SMEM),
                      pl.BlockSpec((tm, D_MODEL), lambda w, *_: (w, 0)),
                      pl.BlockSpec((SUBLANES, tm), lambda w, *_: (0, w))],
            out_specs=pl.BlockSpec(memory_space=pl.ANY),
            scratch_shapes=[pltpu.VMEM((2, SLOTS, D_MODEL), F32), pltpu.VMEM((MOE_BLOCK, D_MODEL), F32),
                            pltpu.SemaphoreType.DMA, pltpu.SemaphoreType.DMA],
        ),
        out_shape=jax.ShapeDtypeStruct((n_blocks * MOE_BLOCK, D_MODEL), F32),
        compiler_params=_params(("arbitrary",)),
        name="moe_dispatch",
    )(plan["tot"], plan["pad_start"], plan["pad_len"], plan["nused"], plan["dst"], h2, routet)


def _expert_kernel(blk_e_ref, nused_ref, x_ref, wg_ref, wu_ref, wd_ref, o_ref):
    b = pl.program_id(0)

    @pl.when(b < nused_ref[0])
    def _():
        x = x_ref[...].astype(BF16)
        gate = jnp.dot(x, wg_ref[0], preferred_element_type=F32)
        up = jnp.dot(x, wu_ref[0], preferred_element_type=F32)
        act = (jax.nn.silu(gate) * up).astype(BF16)
        o_ref[...] = jnp.dot(act, wd_ref[0], preferred_element_type=F32)

    @pl.when(b >= nused_ref[0])
    def _():
        o_ref[...] = jnp.zeros(o_ref.shape, o_ref.dtype)


def _experts(plan, xs, wg, wu, wd, n_blocks):
    return pl.pallas_call(
        _expert_kernel,
        grid_spec=pltpu.PrefetchScalarGridSpec(
            num_scalar_prefetch=2,
            grid=(n_blocks,),
            in_specs=[pl.BlockSpec((MOE_BLOCK, D_MODEL), lambda b, be, nu: (b, 0)),
                      pl.BlockSpec((1, D_MODEL, EXPERT_FF), lambda b, be, nu: (be[b], 0, 0)),
                      pl.BlockSpec((1, D_MODEL, EXPERT_FF), lambda b, be, nu: (be[b], 0, 0)),
                      pl.BlockSpec((1, EXPERT_FF, D_MODEL), lambda b, be, nu: (be[b], 0, 0))],
            out_specs=pl.BlockSpec((MOE_BLOCK, D_MODEL), lambda b, be, nu: (b, 0)),
        ),
        out_shape=jax.ShapeDtypeStruct((n_blocks * MOE_BLOCK, D_MODEL), F32),
        compiler_params=_params(("arbitrary",)),
        name="moe_experts",
    )(plan["blk_e"], plan["nused"], xs, wg, wu, wd)


def _combine_kernel(tot_ref, src_ref, src_next_ref, yb_hbm, x_ref, route_ref, g_ref, o_ref, loc_ref, sems,
                    *, final_norm):
    w = pl.program_id(0)
    last = pl.num_programs(0) - 1

    def fetch(win, table_ref):
        buf = loc_ref.at[win % 2]
        tiles = tot_ref[win] // SUBLANES

        def clear(r, carry):
            buf[pl.ds(pl.multiple_of(r * SUBLANES, SUBLANES), SUBLANES), :] = jnp.zeros((SUBLANES, D_MODEL), F32)
            return carry

        lax.fori_loop(tiles, TILES, clear, 0)
        _start_tiles(tiles, table_ref, buf, yb_hbm, sems.at[win % 2], False)

    @pl.when(w == 0)
    def _():
        fetch(w, src_ref)

    @pl.when(w < last)
    def _():
        fetch(jnp.minimum(w + 1, last), src_next_ref)

    buf = loc_ref.at[w % 2]
    _wait_tiles(tot_ref[w] // SUBLANES, buf, yb_hbm, sems.at[w % 2], False)
    route = route_ref[...]
    pos1 = route[:, 4:5].astype(jnp.int32)
    pos2 = route[:, 5:6].astype(jnp.int32)
    slot = lax.broadcasted_iota(jnp.int32, (PROJ_ROWS, SLOTS), 1)
    sel = jnp.where(slot == pos1, route[:, 2:3], 0.0) + jnp.where(slot == pos2, route[:, 3:4], 0.0)
    sel_hi = sel.astype(BF16)
    sel_lo = (sel - sel_hi.astype(F32)).astype(BF16)
    y = buf[...]
    y_hi = y.astype(BF16)
    y_lo = (y - y_hi.astype(F32)).astype(BF16)
    moe = (jnp.dot(sel_hi, y_hi, preferred_element_type=F32)
           + jnp.dot(sel_lo, y_hi, preferred_element_type=F32)
           + jnp.dot(sel_hi, y_lo, preferred_element_type=F32))
    x = x_ref[...] + moe
    if final_norm:
        ms = jnp.mean(x * x, axis=-1, keepdims=True)
        x = x * lax.rsqrt(ms + RMS_EPS) * g_ref[...]
    o_ref[...] = x


def _combine(plan, yb, x1, route, g, final_norm):
    T = x1.shape[0]
    tm = PROJ_ROWS
    nt = T // tm
    table = lambda shift: pl.BlockSpec((1, 1, TILES), lambda w, *_: (jnp.minimum(w + shift, nt - 1), 0, 0),
                                       memory_space=pltpu.SMEM)
    return pl.pallas_call(
        functools.partial(_combine_kernel, final_norm=final_norm),
        grid_spec=pltpu.PrefetchScalarGridSpec(
            num_scalar_prefetch=1,
            grid=(nt,),
            in_specs=[table(0), table(1),
                      pl.BlockSpec(memory_space=pl.ANY),
                      pl.BlockSpec((tm, D_MODEL), lambda w, *_: (w, 0)),
                      pl.BlockSpec((tm, LANES), lambda w, *_: (w, 0)),
                      pl.BlockSpec((1, D_MODEL), lambda w, *_: (0, 0))],
            out_specs=pl.BlockSpec((tm, D_MODEL), lambda w, *_: (w, 0)),
            scratch_shapes=[pltpu.VMEM((2, SLOTS, D_MODEL), F32), pltpu.SemaphoreType.DMA((2,))],
        ),
        out_shape=jax.ShapeDtypeStruct((T, D_MODEL), F32),
        compiler_params=_params(("arbitrary",)),
        name="moe_combine",
    )(plan["tot"], plan["dst"], plan["dst"], yb, x1, route, g)


def _routing_plan(wcnt, T):
    n = wcnt[:, 0, N_GROUPS:N_GROUPS + N_EXPERTS].astype(jnp.int32)
    cnt = jnp.sum(n, axis=0)
    nblk = (cnt + MOE_BLOCK - 1) // MOE_BLOCK
    blk_end = jnp.cumsum(nblk)
    first_row = (blk_end - nblk) * MOE_BLOCK
    ls = jnp.cumsum(n, axis=1) - n
    gs = first_row[None, :] + jnp.cumsum(n, axis=0) - n
    worst_rows = T * 2 + (T // PROJ_ROWS) * N_EXPERTS * SUBLANES
    n_blocks = -(-worst_rows // MOE_BLOCK) + N_EXPERTS
    blk_e = jnp.minimum(jnp.sum(jnp.arange(n_blocks)[:, None] >= blk_end[None, :], axis=1), N_EXPERTS - 1)
    tile = jnp.arange(TILES)[None, :, None]
    lt, nt8, gt = (a[:, None, :] // SUBLANES for a in (ls, n, gs))
    dst = jnp.sum(jnp.where((tile >= lt) & (tile < lt + nt8), gt + tile - lt, 0), axis=2)
    i32 = lambda a: a.reshape(-1).astype(jnp.int32)
    plan = dict(tot=i32(jnp.sum(n, axis=1)), pad_start=i32(first_row + cnt), pad_len=i32(nblk * MOE_BLOCK - cnt),
                nused=i32(blk_end[-1:]), blk_e=i32(blk_e),
                dst=dst.astype(jnp.int32).reshape(n.shape[0], 1, TILES))
    return plan, n_blocks


def _pack_w_in(w):
    scale = QK_DIM ** -0.5 * LOG2E
    q1, q2, k1, k2 = (w[:, i * QK_COLS:(i + 1) * QK_COLS].reshape(D_MODEL, N_HEADS, QK_DIM) for i in range(4))
    qq = (jnp.concatenate([q1, q2], axis=-1) * scale).reshape(D_MODEL, ATTN_WIDTH)
    kk = jnp.concatenate([k1, k2], axis=-1).reshape(D_MODEL, ATTN_WIDTH)
    v0 = 4 * QK_COLS
    packed = jnp.concatenate([qq, kk, w[:, v0 + ATTN_WIDTH:]], axis=1).astype(BF16)
    return packed, jnp.transpose(w[:, v0:v0 + ATTN_WIDTH]).astype(BF16)


def kernel(x, rel_bias, ln1_g, w_in, lam_q1, lam_k1, lam_q2, lam_k2, subln_g, conv_w, conv_b, conv_ln_g, conv_ln_b, ssm_lam_re, ssm_lam_im, ssm_log_dt, ssm_b_re, ssm_b_im, ssm_c_re, ssm_c_im, ssm_d, ssm_glu_w, ssm_glu_b, w_out, ln2_g, group_router_w, group_router_b, expert_router_w, expert_router_b, w_gate, w_up, w_down, final_g):
    B, L, D = x.shape
    T = B * L
    depth = w_in.shape[0]
    assert D == D_MODEL and L % CONV_ROWS == 0 and L % ATTN_TILE == 0 and L % SSM_ROWS == 0
    assert PROJ_ROWS == ATTN_TILE and T % PROJ_ROWS == 0 and (2 * T) % MOE_BLOCK == 0
    x2 = x.reshape(T, D)
    bias_diag, bias_sub = _bias_tiles(rel_bias)
    row = lambda v: v.astype(F32).reshape(1, -1)
    for l in range(depth):
        lam_init = 0.8 - 0.6 * math.exp(-0.3 * l)
        lam = (jnp.exp(jnp.sum(lam_q1[l].astype(F32) * lam_k1[l].astype(F32)))
               - jnp.exp(jnp.sum(lam_q2[l].astype(F32) * lam_k2[l].astype(F32))) + lam_init).reshape(1)
        qq, kk, conv_in, ssm_in, vt = _inproj(x2, row(ln1_g[l]), *_pack_w_in(w_in[l]))
        a = _attention(qq, kk, vt, lam, bias_diag, bias_sub, subln_g[l].astype(F32).reshape(V_DIM, 1),
                       1.0 - lam_init, B, L)
        c = _conformer_conv(conv_in, conv_w[l], row(conv_b[l]), row(conv_ln_g[l]), row(conv_ln_b[l]), B, L)
        wb, apr, api, wc = _ssm_weights(ssm_lam_re[l], ssm_lam_im[l], ssm_log_dt[l], ssm_b_re[l], ssm_b_im[l],
                                        ssm_c_re[l], ssm_c_im[l])
        s = _s5_ssm(ssm_in, wb, apr, api, wc, row(ssm_d[l]), ssm_glu_w[l].astype(BF16), row(ssm_glu_b[l]), B, L)
        wr = jnp.zeros((D, LANES), F32).at[:, :N_GROUPS].set(group_router_w[l]) \
            .at[:, N_GROUPS:N_GROUPS + N_EXPERTS].set(expert_router_w[l]).astype(BF16)
        br = jnp.zeros((1, LANES), F32).at[0, :N_GROUPS].set(group_router_b[l]) \
            .at[0, N_GROUPS:N_GROUPS + N_EXPERTS].set(expert_router_b[l])
        x1, h2, route, routet, wcnt = _outproj_route(x2, a, c, s, w_out[l].astype(BF16), row(ln2_g[l]), wr, br)
        plan, n_blocks = _routing_plan(wcnt, T)
        xs = _dispatch(plan, h2, routet, n_blocks)
        yb = _experts(plan, xs, w_gate[l].astype(BF16), w_up[l].astype(BF16), w_down[l].astype(BF16), n_blocks)
        x2 = _combine(plan, yb, x1, route, row(final_g), final_norm=(l == depth - 1))
    return x2.reshape(B, L, D)
```

```python
import functools
import math

import jax
import jax.numpy as jnp
from jax import lax
from jax.experimental import pallas as pl
from jax.experimental.pallas import tpu as pltpu

F32 = jnp.float32
BF16 = jnp.bfloat16

D_MODEL = 1024
N_HEADS = 4
QK_DIM = 64
V_DIM = 128
ATTN_WIDTH = N_HEADS * V_DIM
QK_COLS = N_HEADS * QK_DIM
CONV_WIDTH = 256
CONV_TAPS = 31
SSM_WIDTH = 256
SSM_GROUP = 16
SSM_GROUPS = 16
SSM_STATE = 64
SSM_LANES = SSM_GROUPS * SSM_STATE
REL_BUCKETS = 32
REL_MAX_EXACT = 16
REL_MAX_DIST = 128
N_GROUPS = 4
EXPERTS_PER_GROUP = 8
N_EXPERTS = N_GROUPS * EXPERTS_PER_GROUP
EXPERT_FF = 512
RMS_EPS = 1e-6
LN_EPS = 1e-5
NEG_BIG = -1e30

LANES = 128
SUBLANES = 8
VMEM_LIMIT = 48 * 1024 * 1024

PROJ_ROWS = 512
ATTN_TILE = 512
ATTN_CHUNK = 32
ONES_ROWS = 16
LOG2E = math.log2(math.e)
CONV_ROWS = 512
CONV_HALO = 32
SSM_ROWS = 256
SSM_UNROLL = 8
SSM_SHIFTS = (1, 2, 4)
MOE_BLOCK = 512


def _params(sem):
    return pltpu.CompilerParams(dimension_semantics=sem, vmem_limit_bytes=VMEM_LIMIT)


def _inproj_kernel(x_ref, g_ref, w_ref, wvt_ref, qq_ref, kk_ref, conv_ref, ssm_ref, vt_ref):
    x = x_ref[...]
    ms = jnp.mean(x * x, axis=-1, keepdims=True)
    h = (x * lax.rsqrt(ms + RMS_EPS) * g_ref[...]).astype(BF16)
    o = 0
    for ref in (qq_ref, kk_ref, conv_ref, ssm_ref):
        n = ref.shape[-1]
        ref[...] = jnp.dot(h, w_ref[:, o:o + n], preferred_element_type=F32).astype(ref.dtype)
        o += n
    vt_ref[0] = lax.dot_general(wvt_ref[...], h, (((1,), (1,)), ((), ())),
                                preferred_element_type=F32).astype(vt_ref.dtype)


def _inproj(x2, g, w, wvt):
    T = x2.shape[0]
    tm = PROJ_ROWS
    widths = (ATTN_WIDTH, ATTN_WIDTH, 2 * CONV_WIDTH, SSM_WIDTH)
    dtypes = (BF16, BF16, F32, F32)
    return pl.pallas_call(
        _inproj_kernel,
        grid=(T // tm,),
        in_specs=[
            pl.BlockSpec((tm, D_MODEL), lambda i: (i, 0)),
            pl.BlockSpec((1, D_MODEL), lambda i: (0, 0)),
            pl.BlockSpec(w.shape, lambda i: (0, 0)),
            pl.BlockSpec(wvt.shape, lambda i: (0, 0)),
        ],
        out_specs=[pl.BlockSpec((tm, n), lambda i: (i, 0)) for n in widths]
        + [pl.BlockSpec((1, ATTN_WIDTH, tm), lambda i: (i, 0, 0))],
        out_shape=[jax.ShapeDtypeStruct((T, n), dt) for n, dt in zip(widths, dtypes)]
        + [jax.ShapeDtypeStruct((T // tm, ATTN_WIDTH, tm), BF16)],
        compiler_params=_params(("arbitrary",)),
        name="inproj",
    )(x2, g, w, wvt)


def _attn_kernel(lam_ref, q_ref, k_ref, vt_ref, bd_ref, bs_ref, g_ref, o_ref,
                 qs_ref, m_ref, acc_ref, sa_ref, sb_ref, pa_ref, pb_ref, aa_ref, ab_ref, *, out_scale):
    t = ATTN_TILE
    qi = pl.program_id(2)
    q = q_ref[...].astype(F32)
    lane = lax.broadcasted_iota(jnp.int32, q.shape, 1)
    qs_ref[0:t, :] = jnp.where(lane < QK_DIM, q, 0.0).astype(BF16)
    qs_ref[t:2 * t, :] = jnp.where(lane >= QK_DIM, q, 0.0).astype(BF16)
    m_ref[...] = jnp.full(m_ref.shape, NEG_BIG, F32)
    acc_ref[...] = jnp.zeros(acc_ref.shape, F32)

    chunks = [(c, c + ATTN_CHUNK) for c in range(0, t, ATTN_CHUNK)]
    fold = lambda a: a.reshape(ATTN_CHUNK // SUBLANES, SUBLANES, 2 * t)

    def scores(j, s_ref):
        k = k_ref[pl.ds(pl.multiple_of(j * t, t), t), :]
        s_ref[...] = lax.dot_general(k, qs_ref[...], (((1,), (1,)), ((), ())), preferred_element_type=F32)

    def softmax(s_ref, p_ref, a_ref, bias_ref=None):
        top = jnp.full((SUBLANES, 2 * t), NEG_BIG, F32)
        for lo, hi in chunks:
            s = s_ref[lo:hi, :]
            if bias_ref is not None:
                b = bias_ref[0, lo:hi, :]
                s = s + jnp.concatenate([b, b], axis=1)
                s_ref[lo:hi, :] = s
            top = jnp.maximum(top, jnp.max(fold(s), axis=0))
        m_prev = m_ref[...]
        m_new = jnp.maximum(m_prev, jnp.max(top, axis=0, keepdims=True))
        a_ref[...] = jnp.exp2(m_prev - m_new)
        for lo, hi in chunks:
            p_ref[lo:hi, :] = jnp.exp2(s_ref[lo:hi, :] - m_new).astype(BF16)
        m_ref[...] = m_new

    def values(j, p_ref, a_ref):
        lhs = jnp.concatenate([vt_ref[jnp.maximum(j, 0)], jnp.ones((ONES_ROWS, t), BF16)], axis=0)
        acc_ref[...] = a_ref[...] * acc_ref[...] + jnp.dot(lhs, p_ref[...], preferred_element_type=F32)

    A = (sa_ref, pa_ref, aa_ref)
    B = (sb_ref, pb_ref, ab_ref)
    pb_ref[...] = jnp.zeros(pb_ref.shape, BF16)
    ab_ref[...] = jnp.ones(ab_ref.shape, F32)
    nfar = jnp.maximum(qi - 1, 0)
    scores(0, sa_ref)

    def far_pair(i, carry):
        k = 2 * i
        scores(k + 1, sb_ref)
        softmax(*A)
        values(k - 1, pb_ref, ab_ref)
        scores(k + 2, sa_ref)
        softmax(*B)
        values(k, pa_ref, aa_ref)
        return carry

    lax.fori_loop(0, nfar // 2, far_pair, 0)

    @pl.when(qi == 0)
    def _():
        softmax(*A, bd_ref)
        values(qi, pa_ref, aa_ref)

    @pl.when((qi >= 1) & (nfar % 2 == 0))
    def _():
        scores(qi, sb_ref)
        softmax(*A, bs_ref)
        values(qi - 2, pb_ref, ab_ref)
        softmax(*B, bd_ref)
        values(qi - 1, pa_ref, aa_ref)
        values(qi, pb_ref, ab_ref)

    @pl.when(nfar % 2 == 1)
    def _():
        scores(qi - 1, sb_ref)
        softmax(*A)
        values(qi - 3, pb_ref, ab_ref)
        scores(qi, sa_ref)
        softmax(*B, bs_ref)
        values(qi - 2, pa_ref, aa_ref)
        softmax(*A, bd_ref)
        values(qi - 1, pb_ref, ab_ref)
        values(qi, pa_ref, aa_ref)

    acc = acc_ref[0:V_DIM, :]
    l = acc_ref[V_DIM:V_DIM + 1, :]
    a = acc[:, 0:t] / l[:, 0:t] - lam_ref[0] * (acc[:, t:2 * t] / l[:, t:2 * t])
    ms = jnp.mean(a * a, axis=0, keepdims=True)
    y = a * lax.rsqrt(ms + RMS_EPS) * g_ref[...] * out_scale
    o_ref[...] = jnp.transpose(y).astype(o_ref.dtype)


def _attention(qq, kk, vt, lam, bias_diag, bias_sub, subln_g, out_scale, B, L):
    T = B * L
    t = ATTN_TILE
    nq = L // t
    return pl.pallas_call(
        functools.partial(_attn_kernel, out_scale=out_scale),
        grid=(B, N_HEADS, nq),
        in_specs=[
            pl.BlockSpec(memory_space=pltpu.SMEM),
            pl.BlockSpec((t, LANES), lambda b, h, i: (b * nq + i, h)),
            pl.BlockSpec((L, LANES), lambda b, h, i: (b, h)),
            pl.BlockSpec((nq, V_DIM, t), lambda b, h, i: (b, h, 0)),
            pl.BlockSpec((1, t, t), lambda b, h, i: (h, 0, 0)),
            pl.BlockSpec((1, t, t), lambda b, h, i: (h, 0, 0)),
            pl.BlockSpec((V_DIM, 1), lambda b, h, i: (0, 0)),
        ],
        out_specs=pl.BlockSpec((t, LANES), lambda b, h, i: (b * nq + i, h)),
        out_shape=jax.ShapeDtypeStruct((T, ATTN_WIDTH), BF16),
        scratch_shapes=[
            pltpu.VMEM((2 * t, LANES), BF16),
            pltpu.VMEM((1, 2 * t), F32),
            pltpu.VMEM((V_DIM + ONES_ROWS, 2 * t), F32),
            pltpu.VMEM((t, 2 * t), F32),
            pltpu.VMEM((t, 2 * t), F32),
            pltpu.VMEM((t, 2 * t), BF16),
            pltpu.VMEM((t, 2 * t), BF16),
            pltpu.VMEM((1, 2 * t), F32),
            pltpu.VMEM((1, 2 * t), F32),
        ],
        compiler_params=_params(("arbitrary", "arbitrary", "arbitrary")),
        name="diff_attn",
    )(lam, qq, kk, vt, bias_diag, bias_sub, subln_g)


def _rel_bucket(rel):
    n = jnp.maximum(rel, 0)
    nf = jnp.maximum(n, 1).astype(F32)
    large = REL_MAX_EXACT + (jnp.log(nf / REL_MAX_EXACT) / math.log(REL_MAX_DIST / REL_MAX_EXACT)
                             * (REL_BUCKETS - REL_MAX_EXACT)).astype(jnp.int32)
    large = jnp.minimum(large, REL_BUCKETS - 1)
    return jnp.where(n < REL_MAX_EXACT, n, large)


def _bias_tiles(rel_table):
    t = ATTN_TILE
    assert t >= REL_MAX_DIST
    far = rel_table[REL_BUCKETS - 1].astype(F32)
    rel_d = jnp.arange(t)[None, :] - jnp.arange(t)[:, None]

    def lookup(bucket):
        out = jnp.zeros((N_HEADS,) + bucket.shape, F32)
        for b in range(REL_BUCKETS):
            out = jnp.where((bucket == b)[None], (rel_table[b].astype(F32) - far)[:, None, None], out)
        return out * LOG2E

    bd = jnp.where((rel_d >= 0)[None], lookup(_rel_bucket(rel_d)), NEG_BIG)
    return bd, lookup(_rel_bucket(rel_d + t))


def _conv_kernel(u_ref, w_ref, b_ref, g_ref, beta_ref, o_ref, h_ref):
    tt = CONV_ROWS
    j = pl.program_id(1)

    @pl.when(j == 0)
    def _():
        h_ref[0:CONV_HALO, :] = jnp.zeros((CONV_HALO, CONV_WIDTH), F32)

    @pl.when(j > 0)
    def _():
        h_ref[0:CONV_HALO, :] = h_ref[tt:tt + CONV_HALO, :]

    u = u_ref[...]
    h_ref[CONV_HALO:CONV_HALO + tt, :] = u[:, 0:CONV_WIDTH] * jax.nn.sigmoid(u[:, CONV_WIDTH:])
    acc = jnp.broadcast_to(b_ref[...], (tt, CONV_WIDTH))
    off = CONV_HALO - (CONV_TAPS - 1)
    for k in range(CONV_TAPS):
        acc = acc + w_ref[k:k + 1, :] * h_ref[off + k:off + k + tt, :]
    mu = jnp.mean(acc, axis=-1, keepdims=True)
    cen = acc - mu
    var = jnp.mean(cen * cen, axis=-1, keepdims=True)
    y = cen * lax.rsqrt(var + LN_EPS) * g_ref[...] + beta_ref[...]
    o_ref[...] = jax.nn.silu(y).astype(o_ref.dtype)


def _conformer_conv(conv_in, w, b, g, beta, B, L):
    T = B * L
    tt = CONV_ROWS
    nt = L // tt
    vec = pl.BlockSpec((1, CONV_WIDTH), lambda bb, j: (0, 0))
    return pl.pallas_call(
        _conv_kernel,
        grid=(B, nt),
        in_specs=[
            pl.BlockSpec((tt, 2 * CONV_WIDTH), lambda bb, j: (bb * nt + j, 0)),
            pl.BlockSpec((CONV_TAPS, CONV_WIDTH), lambda bb, j: (0, 0)),
            vec, vec, vec,
        ],
        out_specs=pl.BlockSpec((tt, CONV_WIDTH), lambda bb, j: (bb * nt + j, 0)),
        out_shape=jax.ShapeDtypeStruct((T, CONV_WIDTH), BF16),
        scratch_shapes=[pltpu.VMEM((tt + CONV_HALO, CONV_WIDTH), F32)],
        compiler_params=_params(("arbitrary", "arbitrary")),
        name="conformer_conv",
    )(conv_in, w, b, g, beta)


def _ssm_kernel(u_ref, wb_ref, apr_ref, api_ref, wc_ref, d_ref, gw_ref, gb_ref, o_ref,
                xs_ref, carry_ref):
    tt = SSM_ROWS
    n = SSM_LANES
    j = pl.program_id(1)

    @pl.when(j == 0)
    def _():
        carry_ref[...] = jnp.zeros(carry_ref.shape, F32)

    u = u_ref[...]
    xs_ref[...] = jnp.dot(u.astype(BF16), wb_ref[...], preferred_element_type=F32)
    apr = apr_ref[0:SUBLANES, :]
    api = api_ref[0:SUBLANES, :]

    def block(r, carry):
        cr, ci = carry
        start = pl.multiple_of(r * SUBLANES, SUBLANES)
        xr = xs_ref[pl.ds(start, SUBLANES), 0:n]
        xi = xs_ref[pl.ds(start, SUBLANES), n:2 * n]
        for k, shift in enumerate(SSM_SHIFTS):
            ar = apr_ref[(k + 1) * SUBLANES:(k + 2) * SUBLANES, :]
            ai = api_ref[(k + 1) * SUBLANES:(k + 2) * SUBLANES, :]
            sr = pltpu.roll(xr, shift, 0)
            si = pltpu.roll(xi, shift, 0)
            xr, xi = xr + (ar * sr - ai * si), xi + (ar * si + ai * sr)
        xr, xi = xr + (apr * cr - api * ci), xi + (apr * ci + api * cr)
        xs_ref[pl.ds(start, SUBLANES), 0:n] = xr
        xs_ref[pl.ds(start, SUBLANES), n:2 * n] = xi
        return xr[SUBLANES - 1:SUBLANES, :], xi[SUBLANES - 1:SUBLANES, :]

    cr, ci = lax.fori_loop(0, tt // SUBLANES, block, (carry_ref[0:1, :], carry_ref[1:2, :]),
                           unroll=SSM_UNROLL)
    carry_ref[0:1, :] = cr
    carry_ref[1:2, :] = ci

    y = jnp.dot(xs_ref[...].astype(BF16), wc_ref[...], preferred_element_type=F32) + u * d_ref[...]
    g = jax.nn.gelu(y)
    z = jnp.dot(g.astype(BF16), gw_ref[...], preferred_element_type=F32) + gb_ref[...]
    o_ref[...] = (g * jax.nn.sigmoid(z)).astype(o_ref.dtype)


def _s5_ssm(ssm_in, wb, apr, api, wc, d, gw, gb, B, L):
    T = B * L
    tt = SSM_ROWS
    nt = L // tt
    const = lambda a: pl.BlockSpec(a.shape, lambda bb, j: (0, 0))
    return pl.pallas_call(
        _ssm_kernel,
        grid=(B, nt),
        in_specs=[pl.BlockSpec((tt, SSM_WIDTH), lambda bb, j: (bb * nt + j, 0)),
                  const(wb), const(apr), const(api), const(wc), const(d), const(gw), const(gb)],
        out_specs=pl.BlockSpec((tt, SSM_WIDTH), lambda bb, j: (bb * nt + j, 0)),
        out_shape=jax.ShapeDtypeStruct((T, SSM_WIDTH), BF16),
        scratch_shapes=[pltpu.VMEM((tt, 2 * SSM_LANES), F32), pltpu.VMEM((SUBLANES, SSM_LANES), F32)],
        compiler_params=_params(("arbitrary", "arbitrary")),
        name="s5_scan",
    )(ssm_in, wb, apr, api, wc, d, gw, gb)


def _ssm_weights(lam_re, lam_im, log_dt, b_re, b_im, c_re, c_im):
    G, P, H = SSM_GROUPS, SSM_STATE, SSM_GROUP
    dt = jnp.exp(log_dt.astype(F32))[:, None]
    lr, li = lam_re.astype(F32), lam_im.astype(F32)
    mag = jnp.exp(lr * dt)
    ar, ai = mag * jnp.cos(li * dt), mag * jnp.sin(li * dt)
    den = lr * lr + li * li
    zr = ((ar - 1.0) * lr + ai * li) / den
    zi = (ai * lr - (ar - 1.0) * li) / den
    bre, bim = b_re.astype(F32), b_im.astype(F32)
    bbr = zr[..., None] * bre - zi[..., None] * bim
    bbi = zr[..., None] * bim + zi[..., None] * bre
    eye = jnp.eye(G, dtype=F32)
    wb = jnp.concatenate([jnp.einsum('gph,gk->ghkp', bbr, eye).reshape(G * H, G * P),
                          jnp.einsum('gph,gk->ghkp', bbi, eye).reshape(G * H, G * P)], axis=1)
    wc = jnp.concatenate([jnp.einsum('ghp,gk->gpkh', c_re.astype(F32), eye).reshape(G * P, G * H),
                          -jnp.einsum('ghp,gk->gpkh', c_im.astype(F32), eye).reshape(G * P, G * H)], axis=0)
    pr, pi = [ar.reshape(1, G * P)], [ai.reshape(1, G * P)]
    for _ in range(SUBLANES - 1):
        pr, pi = (pr + [pr[-1] * pr[0] - pi[-1] * pi[0]], pi + [pr[-1] * pi[0] + pi[-1] * pr[0]])
    rows = jnp.arange(SUBLANES)[:, None]
    tr, ti = list(pr), list(pi)
    for shift in SSM_SHIFTS:
        tr.append(jnp.where(rows >= shift, pr[shift - 1], 0.0))
        ti.append(jnp.where(rows >= shift, pi[shift - 1], 0.0))
    return wb.astype(BF16), jnp.concatenate(tr, axis=0), jnp.concatenate(ti, axis=0), wc.astype(BF16)


def _outproj_kernel(x_ref, a_ref, c_ref, s_ref, w_ref, g_ref, wr_ref, br_ref,
                    x1_ref, h2_ref, route_ref, routet_ref, wcnt_ref):
    tm = PROJ_ROWS
    o1, o2 = ATTN_WIDTH, ATTN_WIDTH + CONV_WIDTH
    y = (jnp.dot(a_ref[...], w_ref[0:o1, :], preferred_element_type=F32)
         + jnp.dot(c_ref[...], w_ref[o1:o2, :], preferred_element_type=F32)
         + jnp.dot(s_ref[...], w_ref[o2:, :], preferred_element_type=F32))
    x1 = x_ref[...] + y
    x1_ref[...] = x1
    ms = jnp.mean(x1 * x1, axis=-1, keepdims=True)
    h2 = (x1 * lax.rsqrt(ms + RMS_EPS) * g_ref[...]).astype(BF16)
    h2_ref[...] = h2

    logits = jnp.dot(h2, wr_ref[...], preferred_element_type=F32) + br_ref[...]
    col = lax.broadcasted_iota(jnp.int32, logits.shape, 1)

    def first_max(vals):
        top = jnp.max(vals, axis=-1, keepdims=True)
        return top, jnp.min(jnp.where(vals == top, col, LANES), axis=-1, keepdims=True)

    glog = jnp.where(col < N_GROUPS, logits, NEG_BIG)
    gmax, gidx = first_max(glog)
    gp = 1.0 / jnp.sum(jnp.exp(glog - gmax), axis=-1, keepdims=True)
    lo = N_GROUPS + gidx * EXPERTS_PER_GROUP
    e = jnp.where((col >= lo) & (col < lo + EXPERTS_PER_GROUP), logits, NEG_BIG)
    v1, i1 = first_max(e)
    e = jnp.where(col == i1, NEG_BIG, e)
    v2, i2 = first_max(e)
    ex = jnp.exp(v2 - v1)
    w1 = gp * (1.0 / (1.0 + ex))
    w2 = gp * (ex / (1.0 + ex))

    hit1 = col == i1
    hit2 = col == i2
    onehot = jnp.where(hit1 | hit2, 1.0, 0.0)
    r = lax.broadcasted_iota(jnp.int32, (tm, tm), 0)
    c = lax.broadcasted_iota(jnp.int32, (tm, tm), 1)
    before = jnp.where(r > c, 1.0, 0.0).astype(BF16)
    prior = jnp.dot(before, onehot.astype(BF16), preferred_element_type=F32)
    count = jnp.sum(onehot, axis=0, keepdims=True)
    count = jnp.floor((count + (SUBLANES - 1)) * (1.0 / SUBLANES)) * SUBLANES
    wcnt_ref[0] = count
    run = jnp.broadcast_to(count, (SUBLANES, LANES))
    lane8 = lax.broadcasted_iota(jnp.int32, (SUBLANES, LANES), 1)
    shift = 1
    while shift < LANES:
        run = run + jnp.where(lane8 >= shift, pltpu.roll(run, shift, 1), 0.0)
        shift *= 2
    where_to = prior + (run[0:1, :] - count)
    pos1 = jnp.sum(jnp.where(hit1, where_to, 0.0), axis=-1, keepdims=True)
    pos2 = jnp.sum(jnp.where(hit2, where_to, 0.0), axis=-1, keepdims=True)

    fields = ((i1 - N_GROUPS).astype(F32), (i2 - N_GROUPS).astype(F32), w1, w2, pos1, pos2)
    route = jnp.zeros(logits.shape, F32)
    for k, val in enumerate(fields):
        route = jnp.where(col == k, val, route)
    route_ref[...] = route
    routet_ref[...] = jnp.transpose(route)[0:SUBLANES, :]


def _outproj_route(x2, a, c, s, w_out, g, wr, br):
    T = x2.shape[0]
    tm = PROJ_ROWS
    nt = T // tm
    rows = lambda n: pl.BlockSpec((tm, n), lambda i: (i, 0))
    const = lambda arr: pl.BlockSpec(arr.shape, lambda i: (0, 0))
    return pl.pallas_call(
        _outproj_kernel,
        grid=(nt,),
        in_specs=[rows(D_MODEL), rows(ATTN_WIDTH), rows(CONV_WIDTH), rows(SSM_WIDTH),
                  const(w_out), const(g), const(wr), const(br)],
        out_specs=[rows(D_MODEL), rows(D_MODEL), rows(LANES),
                   pl.BlockSpec((SUBLANES, tm), lambda i: (0, i)),
                   pl.BlockSpec((1, 1, LANES), lambda i: (i, 0, 0))],
        out_shape=[jax.ShapeDtypeStruct((T, D_MODEL), F32), jax.ShapeDtypeStruct((T, D_MODEL), BF16),
                   jax.ShapeDtypeStruct((T, LANES), F32), jax.ShapeDtypeStruct((SUBLANES, T), F32),
                   jax.ShapeDtypeStruct((nt, 1, LANES), F32)],
        compiler_params=_params(("arbitrary",)),
        name="outproj_route",
    )(x2, a, c, s, w_out, g, wr, br)


SLOTS = 2 * PROJ_ROWS + N_EXPERTS * SUBLANES
TILES = SLOTS // SUBLANES
assert PROJ_ROWS % SUBLANES == 0 and MOE_BLOCK % SUBLANES == 0


def _for_each_piece(n, largest, fn):
    off = jnp.int32(0)
    size = largest
    while size >= SUBLANES:
        take = (n // size) & 1

        @pl.when(take == 1)
        def _(off=off, size=size):
            fn(pl.multiple_of(off, SUBLANES), size)

        off = off + take * size
        size //= 2


def _tile_copy(buf_ref, tile, hbm_ref, hbm_tile, sem, to_hbm):
    local = buf_ref.at[pl.ds(pl.multiple_of(tile * SUBLANES, SUBLANES), SUBLANES), :]
    remote = hbm_ref.at[pl.ds(pl.multiple_of(hbm_tile * SUBLANES, SUBLANES), SUBLANES), :]
    return pltpu.make_async_copy(local, remote, sem) if to_hbm else pltpu.make_async_copy(remote, local, sem)


def _start_tiles(count, table_ref, buf_ref, hbm_ref, sem, to_hbm):
    def body(c, carry):
        _tile_copy(buf_ref, c, hbm_ref, table_ref[0, 0, c], sem, to_hbm).start()
        return carry

    lax.fori_loop(0, count, body, 0)


def _wait_tiles(count, buf_ref, hbm_ref, sem, to_hbm):
    def body(c, carry):
        _tile_copy(buf_ref, 0, hbm_ref, 0, sem, to_hbm).wait()
        return carry

    lax.fori_loop(0, count, body, 0)


def _dispatch_kernel(tot_ref, pads_ref, padn_ref, nused_ref, dst_ref, h_ref, rt_ref, xs_hbm,
                     loc_ref, zero_ref, sem, zsem, *, n_blocks):
    w = pl.program_id(0)
    last = pl.num_programs(0) - 1

    @pl.when(w == 0)
    def _():
        zero_ref[...] = jnp.zeros(zero_ref.shape, zero_ref.dtype)

        def fill(act):
            def per_expert(e, carry):
                start = pl.multiple_of(pads_ref[e], SUBLANES)

                def piece(off, size):
                    act(pltpu.make_async_copy(
                        zero_ref.at[pl.ds(0, size), :],
                        xs_hbm.at[pl.ds(pl.multiple_of(start + off, SUBLANES), size), :], zsem))
                _for_each_piece(padn_ref[e], MOE_BLOCK // 2, piece)
                return carry

            def per_block(b, carry):
                first = pl.multiple_of(b * MOE_BLOCK, MOE_BLOCK)
                act(pltpu.make_async_copy(zero_ref, xs_hbm.at[pl.ds(first, MOE_BLOCK), :], zsem))
                return carry

            lax.fori_loop(0, N_EXPERTS, per_expert, 0)
            lax.fori_loop(nused_ref[0], n_blocks, per_block, 0)

        fill(lambda cp: cp.start())
        fill(lambda cp: cp.wait())

    pos1 = rt_ref[4:5, :].astype(jnp.int32)
    pos2 = rt_ref[5:6, :].astype(jnp.int32)
    slot = lax.broadcasted_iota(jnp.int32, (SLOTS, PROJ_ROWS), 0)
    pick = jnp.where((slot == pos1) | (slot == pos2), 1.0, 0.0).astype(BF16)
    buf = loc_ref.at[w % 2]
    buf[...] = jnp.dot(pick, h_ref[...], preferred_element_type=F32)

    @pl.when(w > 0)
    def _():
        _wait_tiles(tot_ref[jnp.maximum(w - 1, 0)] // SUBLANES, buf, xs_hbm, sem, True)

    _start_tiles(tot_ref[w] // SUBLANES, dst_ref, buf, xs_hbm, sem, True)

    @pl.when(w == last)
    def _():
        _wait_tiles(tot_ref[w] // SUBLANES, buf, xs_hbm, sem, True)


def _dispatch(plan, h2, routet, n_blocks):
    T = h2.shape[0]
    tm = PROJ_ROWS
    return pl.pallas_call(
        functools.partial(_dispatch_kernel, n_blocks=n_blocks),
        grid_spec=pltpu.PrefetchScalarGridSpec(
            num_scalar_prefetch=4,
            grid=(T // tm,),
            in_specs=[pl.BlockSpec((1, 1, TILES), lambda w, *_: (w, 0, 0), memory_space=pltpu.SMEM),
                      pl.BlockSpec((tm, D_MODEL), lambda w, *_: (w, 0)),
                      pl.BlockSpec((SUBLANES, tm), lambda w, *_: (0, w))],
            out_specs=pl.BlockSpec(memory_space=pl.ANY),
            scratch_shapes=[pltpu.VMEM((2, SLOTS, D_MODEL), F32), pltpu.VMEM((MOE_BLOCK, D_MODEL), F32),
                            pltpu.SemaphoreType.DMA, pltpu.SemaphoreType.DMA],
        ),
        out_shape=jax.ShapeDtypeStruct((n_blocks * MOE_BLOCK, D_MODEL), F32),
        compiler_params=_params(("arbitrary",)),
        name="moe_dispatch",
    )(plan["tot"], plan["pad_start"], plan["pad_len"], plan["nused"], plan["dst"], h2, routet)


def _expert_kernel(blk_e_ref, nused_ref, x_ref, wg_ref, wu_ref, wd_ref, o_ref):
    b = pl.program_id(0)

    @pl.when(b < nused_ref[0])
    def _():
        x = x_ref[...].astype(BF16)
        gate = jnp.dot(x, wg_ref[0, 0].astype(BF16), preferred_element_type=F32)
        up = jnp.dot(x, wu_ref[0, 0].astype(BF16), preferred_element_type=F32)
        act = (jax.nn.silu(gate) * up).astype(BF16)
        o_ref[...] = jnp.dot(act, wd_ref[0, 0].astype(BF16), preferred_element_type=F32)

    @pl.when(b >= nused_ref[0])
    def _():
        o_ref[...] = jnp.zeros(o_ref.shape, o_ref.dtype)


def _experts(plan, xs, wg, wu, wd, layer, n_blocks):
    return pl.pallas_call(
        _expert_kernel,
        grid_spec=pltpu.PrefetchScalarGridSpec(
            num_scalar_prefetch=2,
            grid=(n_blocks,),
            in_specs=[pl.BlockSpec((MOE_BLOCK, D_MODEL), lambda b, be, nu: (b, 0)),
                      pl.BlockSpec((1, 1, D_MODEL, EXPERT_FF), lambda b, be, nu: (layer, be[b], 0, 0)),
                      pl.BlockSpec((1, 1, D_MODEL, EXPERT_FF), lambda b, be, nu: (layer, be[b], 0, 0)),
                      pl.BlockSpec((1, 1, EXPERT_FF, D_MODEL), lambda b, be, nu: (layer, be[b], 0, 0))],
            out_specs=pl.BlockSpec((MOE_BLOCK, D_MODEL), lambda b, be, nu: (b, 0)),
        ),
        out_shape=jax.ShapeDtypeStruct((n_blocks * MOE_BLOCK, D_MODEL), F32),
        compiler_params=_params(("arbitrary",)),
        name="moe_experts",
    )(plan["blk_e"], plan["nused"], xs, wg, wu, wd)


def _combine_kernel(tot_ref, src_ref, src_next_ref, yb_hbm, x_ref, route_ref, g_ref, o_ref, loc_ref, sems,
                    *, final_norm):
    w = pl.program_id(0)
    last = pl.num_programs(0) - 1

    def fetch(win, table_ref):
        buf = loc_ref.at[win % 2]
        tiles = tot_ref[win] // SUBLANES

        def clear(r, carry):
            buf[pl.ds(pl.multiple_of(r * SUBLANES, SUBLANES), SUBLANES), :] = jnp.zeros((SUBLANES, D_MODEL), F32)
            return carry

        lax.fori_loop(tiles, TILES, clear, 0)
        _start_tiles(tiles, table_ref, buf, yb_hbm, sems.at[win % 2], False)

    @pl.when(w == 0)
    def _():
        fetch(w, src_ref)

    @pl.when(w < last)
    def _():
        fetch(jnp.minimum(w + 1, last), src_next_ref)

    buf = loc_ref.at[w % 2]
    _wait_tiles(tot_ref[w] // SUBLANES, buf, yb_hbm, sems.at[w % 2], False)
    route = route_ref[...]
    pos1 = route[:, 4:5].astype(jnp.int32)
    pos2 = route[:, 5:6].astype(jnp.int32)
    slot = lax.broadcasted_iota(jnp.int32, (PROJ_ROWS, SLOTS), 1)
    sel = jnp.where(slot == pos1, route[:, 2:3], 0.0) + jnp.where(slot == pos2, route[:, 3:4], 0.0)
    sel_hi = sel.astype(BF16)
    sel_lo = (sel - sel_hi.astype(F32)).astype(BF16)
    y = buf[...]
    y_hi = y.astype(BF16)
    y_lo = (y - y_hi.astype(F32)).astype(BF16)
    moe = (jnp.dot(sel_hi, y_hi, preferred_element_type=F32)
           + jnp.dot(sel_lo, y_hi, preferred_element_type=F32)
           + jnp.dot(sel_hi, y_lo, preferred_element_type=F32))
    x = x_ref[...] + moe
    if final_norm:
        ms = jnp.mean(x * x, axis=-1, keepdims=True)
        x = x * lax.rsqrt(ms + RMS_EPS) * g_ref[...]
    o_ref[...] = x


def _combine(plan, yb, x1, route, g, final_norm):
    T = x1.shape[0]
    tm = PROJ_ROWS
    nt = T // tm
    table = lambda shift: pl.BlockSpec((1, 1, TILES), lambda w, *_: (jnp.minimum(w + shift, nt - 1), 0, 0),
                                       memory_space=pltpu.SMEM)
    return pl.pallas_call(
        functools.partial(_combine_kernel, final_norm=final_norm),
        grid_spec=pltpu.PrefetchScalarGridSpec(
            num_scalar_prefetch=1,
            grid=(nt,),
            in_specs=[table(0), table(1),
                      pl.BlockSpec(memory_space=pl.ANY),
                      pl.BlockSpec((tm, D_MODEL), lambda w, *_: (w, 0)),
                      pl.BlockSpec((tm, LANES), lambda w, *_: (w, 0)),
                      pl.BlockSpec((1, D_MODEL), lambda w, *_: (0, 0))],
            out_specs=pl.BlockSpec((tm, D_MODEL), lambda w, *_: (w, 0)),
            scratch_shapes=[pltpu.VMEM((2, SLOTS, D_MODEL), F32), pltpu.SemaphoreType.DMA((2,))],
        ),
        out_shape=jax.ShapeDtypeStruct((T, D_MODEL), F32),
        compiler_params=_params(("arbitrary",)),
        name="moe_combine",
    )(plan["tot"], plan["dst"], plan["dst"], yb, x1, route, g)


def _routing_plan(wcnt, T):
    n = wcnt[:, 0, N_GROUPS:N_GROUPS + N_EXPERTS].astype(jnp.int32)
    cnt = jnp.sum(n, axis=0)
    nblk = (cnt + MOE_BLOCK - 1) // MOE_BLOCK
    blk_end = jnp.cumsum(nblk)
    first_row = (blk_end - nblk) * MOE_BLOCK
    ls = jnp.cumsum(n, axis=1) - n
    gs = first_row[None, :] + jnp.cumsum(n, axis=0) - n
    worst_rows = T * 2 + (T // PROJ_ROWS) * N_EXPERTS * SUBLANES
    n_blocks = -(-worst_rows // MOE_BLOCK) + N_EXPERTS
    blk_e = jnp.minimum(jnp.sum(jnp.arange(n_blocks)[:, None] >= blk_end[None, :], axis=1), N_EXPERTS - 1)
    tile = jnp.arange(TILES)[None, :, None]
    lt, nt8, gt = (a[:, None, :] // SUBLANES for a in (ls, n, gs))
    dst = jnp.sum(jnp.where((tile >= lt) & (tile < lt + nt8), gt + tile - lt, 0), axis=2)
    i32 = lambda a: a.reshape(-1).astype(jnp.int32)
    plan = dict(tot=i32(jnp.sum(n, axis=1)), pad_start=i32(first_row + cnt), pad_len=i32(nblk * MOE_BLOCK - cnt),
                nused=i32(blk_end[-1:]), blk_e=i32(blk_e),
                dst=dst.astype(jnp.int32).reshape(n.shape[0], 1, TILES))
    return plan, n_blocks


def _pack_w_in(w):
    scale = QK_DIM ** -0.5 * LOG2E
    q1, q2, k1, k2 = (w[:, i * QK_COLS:(i + 1) * QK_COLS].reshape(D_MODEL, N_HEADS, QK_DIM) for i in range(4))
    qq = (jnp.concatenate([q1, q2], axis=-1) * scale).reshape(D_MODEL, ATTN_WIDTH)
    kk = jnp.concatenate([k1, k2], axis=-1).reshape(D_MODEL, ATTN_WIDTH)
    v0 = 4 * QK_COLS
    packed = jnp.concatenate([qq, kk, w[:, v0 + ATTN_WIDTH:]], axis=1).astype(BF16)
    return packed, jnp.transpose(w[:, v0:v0 + ATTN_WIDTH]).astype(BF16)


def kernel(x, rel_bias, ln1_g, w_in, lam_q1, lam_k1, lam_q2, lam_k2, subln_g, conv_w, conv_b, conv_ln_g, conv_ln_b, ssm_lam_re, ssm_lam_im, ssm_log_dt, ssm_b_re, ssm_b_im, ssm_c_re, ssm_c_im, ssm_d, ssm_glu_w, ssm_glu_b, w_out, ln2_g, group_router_w, group_router_b, expert_router_w, expert_router_b, w_gate, w_up, w_down, final_g):
    B, L, D = x.shape
    T = B * L
    depth = w_in.shape[0]
    assert D == D_MODEL and L % CONV_ROWS == 0 and L % ATTN_TILE == 0 and L % SSM_ROWS == 0
    assert PROJ_ROWS == ATTN_TILE and T % PROJ_ROWS == 0 and (2 * T) % MOE_BLOCK == 0
    x2 = x.reshape(T, D)
    bias_diag, bias_sub = _bias_tiles(rel_bias)
    row = lambda v: v.astype(F32).reshape(1, -1)
    for l in range(depth):
        lam_init = 0.8 - 0.6 * math.exp(-0.3 * l)
        lam = (jnp.exp(jnp.sum(lam_q1[l].astype(F32) * lam_k1[l].astype(F32)))
               - jnp.exp(jnp.sum(lam_q2[l].astype(F32) * lam_k2[l].astype(F32))) + lam_init).reshape(1)
        qq, kk, conv_in, ssm_in, vt = _inproj(x2, row(ln1_g[l]), *_pack_w_in(w_in[l]))
        a = _attention(qq, kk, vt, lam, bias_diag, bias_sub, subln_g[l].astype(F32).reshape(V_DIM, 1),
                       1.0 - lam_init, B, L)
        c = _conformer_conv(conv_in, conv_w[l], row(conv_b[l]), row(conv_ln_g[l]), row(conv_ln_b[l]), B, L)
        wb, apr, api, wc = _ssm_weights(ssm_lam_re[l], ssm_lam_im[l], ssm_log_dt[l], ssm_b_re[l], ssm_b_im[l],
                                        ssm_c_re[l], ssm_c_im[l])
        s = _s5_ssm(ssm_in, wb, apr, api, wc, row(ssm_d[l]), ssm_glu_w[l].astype(BF16), row(ssm_glu_b[l]), B, L)
        wr = jnp.zeros((D, LANES), F32).at[:, :N_GROUPS].set(group_router_w[l]) \
            .at[:, N_GROUPS:N_GROUPS + N_EXPERTS].set(expert_router_w[l]).astype(BF16)
        br = jnp.zeros((1, LANES), F32).at[0, :N_GROUPS].set(group_router_b[l]) \
            .at[0, N_GROUPS:N_GROUPS + N_EXPERTS].set(expert_router_b[l])
        x1, h2, route, routet, wcnt = _outproj_route(x2, a, c, s, w_out[l].astype(BF16), row(ln2_g[l]), wr, br)
        plan, n_blocks = _routing_plan(wcnt, T)
        xs = _dispatch(plan, h2, routet, n_blocks)
        yb = _experts(plan, xs, w_gate, w_up, w_down, l, n_blocks)
        x2 = _combine(plan, yb, x1, route, row(final_g), final_norm=(l == depth - 1))
    return x2.reshape(B, L, D)
```

```python
import functools
import math

import jax
import jax.numpy as jnp
from jax import lax
from jax.experimental import pallas as pl
from jax.experimental.pallas import tpu as pltpu

F32 = jnp.float32
BF16 = jnp.bfloat16

D_MODEL = 1024
N_HEADS = 4
QK_DIM = 64
V_DIM = 128
ATTN_WIDTH = N_HEADS * V_DIM
QK_COLS = N_HEADS * QK_DIM
CONV_WIDTH = 256
CONV_TAPS = 31
SSM_WIDTH = 256
SSM_GROUP = 16
SSM_GROUPS = 16
SSM_STATE = 64
SSM_LANES = SSM_GROUPS * SSM_STATE
REL_BUCKETS = 32
REL_MAX_EXACT = 16
REL_MAX_DIST = 128
N_GROUPS = 4
EXPERTS_PER_GROUP = 8
N_EXPERTS = N_GROUPS * EXPERTS_PER_GROUP
EXPERT_FF = 512
RMS_EPS = 1e-6
LN_EPS = 1e-5
NEG_BIG = -1e30

LANES = 128
SUBLANES = 8
VMEM_LIMIT = 48 * 1024 * 1024

PROJ_ROWS = 512
ATTN_TILE = 512
ATTN_CHUNK = 32
ONES_ROWS = 16
LOG2E = math.log2(math.e)
CONV_ROWS = 512
CONV_HALO = 32
SSM_CHUNK = 16
SSM_TILE = 128
SSM_SHIFTS = (1, 2, 4)
MOE_BLOCK = 512


def _params(sem):
    return pltpu.CompilerParams(dimension_semantics=sem, vmem_limit_bytes=VMEM_LIMIT)


def _inproj_kernel(x_ref, g_ref, w_ref, wvt_ref, qq_ref, kk_ref, conv_ref, ssm_ref, vt_ref):
    x = x_ref[...]
    ms = jnp.mean(x * x, axis=-1, keepdims=True)
    h = (x * lax.rsqrt(ms + RMS_EPS) * g_ref[...]).astype(BF16)
    o = 0
    for ref in (qq_ref, kk_ref, conv_ref, ssm_ref):
        n = ref.shape[-1]
        ref[...] = jnp.dot(h, w_ref[:, o:o + n], preferred_element_type=F32).astype(ref.dtype)
        o += n
    vt_ref[0] = lax.dot_general(wvt_ref[...], h, (((1,), (1,)), ((), ())),
                                preferred_element_type=F32).astype(vt_ref.dtype)


def _inproj(x2, g, w, wvt):
    T = x2.shape[0]
    tm = PROJ_ROWS
    widths = (ATTN_WIDTH, ATTN_WIDTH, 2 * CONV_WIDTH, SSM_WIDTH)
    dtypes = (BF16, BF16, F32, F32)
    return pl.pallas_call(
        _inproj_kernel,
        grid=(T // tm,),
        in_specs=[
            pl.BlockSpec((tm, D_MODEL), lambda i: (i, 0)),
            pl.BlockSpec((1, D_MODEL), lambda i: (0, 0)),
            pl.BlockSpec(w.shape, lambda i: (0, 0)),
            pl.BlockSpec(wvt.shape, lambda i: (0, 0)),
        ],
        out_specs=[pl.BlockSpec((tm, n), lambda i: (i, 0)) for n in widths]
        + [pl.BlockSpec((1, ATTN_WIDTH, tm), lambda i: (i, 0, 0))],
        out_shape=[jax.ShapeDtypeStruct((T, n), dt) for n, dt in zip(widths, dtypes)]
        + [jax.ShapeDtypeStruct((T // tm, ATTN_WIDTH, tm), BF16)],
        compiler_params=_params(("arbitrary",)),
        name="inproj",
    )(x2, g, w, wvt)


def _attn_kernel(lam_ref, q_ref, k_ref, vt_ref, bd_ref, bs_ref, g_ref, o_ref,
                 qs_ref, m_ref, acc_ref, sa_ref, sb_ref, pa_ref, pb_ref, aa_ref, ab_ref, *, out_scale):
    t = ATTN_TILE
    qi = pl.program_id(2)
    q = q_ref[...].astype(F32)
    lane = lax.broadcasted_iota(jnp.int32, q.shape, 1)
    qs_ref[0:t, :] = jnp.where(lane < QK_DIM, q, 0.0).astype(BF16)
    qs_ref[t:2 * t, :] = jnp.where(lane >= QK_DIM, q, 0.0).astype(BF16)
    m_ref[...] = jnp.full(m_ref.shape, NEG_BIG, F32)
    acc_ref[...] = jnp.zeros(acc_ref.shape, F32)

    chunks = [(c, c + ATTN_CHUNK) for c in range(0, t, ATTN_CHUNK)]
    fold = lambda a: a.reshape(ATTN_CHUNK // SUBLANES, SUBLANES, 2 * t)

    def scores(j, s_ref):
        k = k_ref[pl.ds(pl.multiple_of(j * t, t), t), :]
        s_ref[...] = lax.dot_general(k, qs_ref[...], (((1,), (1,)), ((), ())), preferred_element_type=F32)

    def softmax(s_ref, p_ref, a_ref, bias_ref=None):
        top = jnp.full((SUBLANES, 2 * t), NEG_BIG, F32)
        for lo, hi in chunks:
            s = s_ref[lo:hi, :]
            if bias_ref is not None:
                b = bias_ref[0, lo:hi, :]
                s = s + jnp.concatenate([b, b], axis=1)
                s_ref[lo:hi, :] = s
            top = jnp.maximum(top, jnp.max(fold(s), axis=0))
        m_prev = m_ref[...]
        m_new = jnp.maximum(m_prev, jnp.max(top, axis=0, keepdims=True))
        a_ref[...] = jnp.exp2(m_prev - m_new)
        for lo, hi in chunks:
            p_ref[lo:hi, :] = jnp.exp2(s_ref[lo:hi, :] - m_new).astype(BF16)
        m_ref[...] = m_new

    def values(j, p_ref, a_ref):
        lhs = jnp.concatenate([vt_ref[jnp.maximum(j, 0)], jnp.ones((ONES_ROWS, t), BF16)], axis=0)
        acc_ref[...] = a_ref[...] * acc_ref[...] + jnp.dot(lhs, p_ref[...], preferred_element_type=F32)

    A = (sa_ref, pa_ref, aa_ref)
    B = (sb_ref, pb_ref, ab_ref)
    pb_ref[...] = jnp.zeros(pb_ref.shape, BF16)
    ab_ref[...] = jnp.ones(ab_ref.shape, F32)
    nfar = jnp.maximum(qi - 1, 0)
    scores(0, sa_ref)

    def far_pair(i, carry):
        k = 2 * i
        scores(k + 1, sb_ref)
        softmax(*A)
        values(k - 1, pb_ref, ab_ref)
        scores(k + 2, sa_ref)
        softmax(*B)
        values(k, pa_ref, aa_ref)
        return carry

    lax.fori_loop(0, nfar // 2, far_pair, 0)

    @pl.when(qi == 0)
    def _():
        softmax(*A, bd_ref)
        values(qi, pa_ref, aa_ref)

    @pl.when((qi >= 1) & (nfar % 2 == 0))
    def _():
        scores(qi, sb_ref)
        softmax(*A, bs_ref)
        values(qi - 2, pb_ref, ab_ref)
        softmax(*B, bd_ref)
        values(qi - 1, pa_ref, aa_ref)
        values(qi, pb_ref, ab_ref)

    @pl.when(nfar % 2 == 1)
    def _():
        scores(qi - 1, sb_ref)
        softmax(*A)
        values(qi - 3, pb_ref, ab_ref)
        scores(qi, sa_ref)
        softmax(*B, bs_ref)
        values(qi - 2, pa_ref, aa_ref)
        softmax(*A, bd_ref)
        values(qi - 1, pb_ref, ab_ref)
        values(qi, pa_ref, aa_ref)

    acc = acc_ref[0:V_DIM, :]
    l = acc_ref[V_DIM:V_DIM + 1, :]
    a = acc[:, 0:t] / l[:, 0:t] - lam_ref[0] * (acc[:, t:2 * t] / l[:, t:2 * t])
    ms = jnp.mean(a * a, axis=0, keepdims=True)
    y = a * lax.rsqrt(ms + RMS_EPS) * g_ref[...] * out_scale
    o_ref[...] = jnp.transpose(y).astype(o_ref.dtype)


def _attention(qq, kk, vt, lam, bias_diag, bias_sub, subln_g, out_scale, B, L):
    T = B * L
    t = ATTN_TILE
    nq = L // t
    return pl.pallas_call(
        functools.partial(_attn_kernel, out_scale=out_scale),
        grid=(B, N_HEADS, nq),
        in_specs=[
            pl.BlockSpec(memory_space=pltpu.SMEM),
            pl.BlockSpec((t, LANES), lambda b, h, i: (b * nq + i, h)),
            pl.BlockSpec((L, LANES), lambda b, h, i: (b, h)),
            pl.BlockSpec((nq, V_DIM, t), lambda b, h, i: (b, h, 0)),
            pl.BlockSpec((1, t, t), lambda b, h, i: (h, 0, 0)),
            pl.BlockSpec((1, t, t), lambda b, h, i: (h, 0, 0)),
            pl.BlockSpec((V_DIM, 1), lambda b, h, i: (0, 0)),
        ],
        out_specs=pl.BlockSpec((t, LANES), lambda b, h, i: (b * nq + i, h)),
        out_shape=jax.ShapeDtypeStruct((T, ATTN_WIDTH), BF16),
        scratch_shapes=[
            pltpu.VMEM((2 * t, LANES), BF16),
            pltpu.VMEM((1, 2 * t), F32),
            pltpu.VMEM((V_DIM + ONES_ROWS, 2 * t), F32),
            pltpu.VMEM((t, 2 * t), F32),
            pltpu.VMEM((t, 2 * t), F32),
            pltpu.VMEM((t, 2 * t), BF16),
            pltpu.VMEM((t, 2 * t), BF16),
            pltpu.VMEM((1, 2 * t), F32),
            pltpu.VMEM((1, 2 * t), F32),
        ],
        compiler_params=_params(("arbitrary", "arbitrary", "arbitrary")),
        name="diff_attn",
    )(lam, qq, kk, vt, bias_diag, bias_sub, subln_g)


def _rel_bucket(rel):
    n = jnp.maximum(rel, 0)
    nf = jnp.maximum(n, 1).astype(F32)
    large = REL_MAX_EXACT + (jnp.log(nf / REL_MAX_EXACT) / math.log(REL_MAX_DIST / REL_MAX_EXACT)
                             * (REL_BUCKETS - REL_MAX_EXACT)).astype(jnp.int32)
    large = jnp.minimum(large, REL_BUCKETS - 1)
    return jnp.where(n < REL_MAX_EXACT, n, large)


def _bias_tiles(rel_table):
    t = ATTN_TILE
    assert t >= REL_MAX_DIST
    far = rel_table[REL_BUCKETS - 1].astype(F32)
    rel_d = jnp.arange(t)[None, :] - jnp.arange(t)[:, None]

    def lookup(bucket):
        out = jnp.zeros((N_HEADS,) + bucket.shape, F32)
        for b in range(REL_BUCKETS):
            out = jnp.where((bucket == b)[None], (rel_table[b].astype(F32) - far)[:, None, None], out)
        return out * LOG2E

    bd = jnp.where((rel_d >= 0)[None], lookup(_rel_bucket(rel_d)), NEG_BIG)
    return bd, lookup(_rel_bucket(rel_d + t))


def _conv_kernel(u_ref, w_ref, b_ref, g_ref, beta_ref, o_ref, h_ref):
    tt = CONV_ROWS
    j = pl.program_id(1)

    @pl.when(j == 0)
    def _():
        h_ref[0:CONV_HALO, :] = jnp.zeros((CONV_HALO, CONV_WIDTH), F32)

    @pl.when(j > 0)
    def _():
        h_ref[0:CONV_HALO, :] = h_ref[tt:tt + CONV_HALO, :]

    u = u_ref[...]
    h_ref[CONV_HALO:CONV_HALO + tt, :] = u[:, 0:CONV_WIDTH] * jax.nn.sigmoid(u[:, CONV_WIDTH:])
    acc = jnp.broadcast_to(b_ref[...], (tt, CONV_WIDTH))
    off = CONV_HALO - (CONV_TAPS - 1)
    for k in range(CONV_TAPS):
        acc = acc + w_ref[k:k + 1, :] * h_ref[off + k:off + k + tt, :]
    mu = jnp.mean(acc, axis=-1, keepdims=True)
    cen = acc - mu
    var = jnp.mean(cen * cen, axis=-1, keepdims=True)
    y = cen * lax.rsqrt(var + LN_EPS) * g_ref[...] + beta_ref[...]
    o_ref[...] = jax.nn.silu(y).astype(o_ref.dtype)


def _conformer_conv(conv_in, w, b, g, beta, B, L):
    T = B * L
    tt = CONV_ROWS
    nt = L // tt
    vec = pl.BlockSpec((1, CONV_WIDTH), lambda bb, j: (0, 0))
    return pl.pallas_call(
        _conv_kernel,
        grid=(B, nt),
        in_specs=[
            pl.BlockSpec((tt, 2 * CONV_WIDTH), lambda bb, j: (bb * nt + j, 0)),
            pl.BlockSpec((CONV_TAPS, CONV_WIDTH), lambda bb, j: (0, 0)),
            vec, vec, vec,
        ],
        out_specs=pl.BlockSpec((tt, CONV_WIDTH), lambda bb, j: (bb * nt + j, 0)),
        out_shape=jax.ShapeDtypeStruct((T, CONV_WIDTH), BF16),
        scratch_shapes=[pltpu.VMEM((tt + CONV_HALO, CONV_WIDTH), F32)],
        compiler_params=_params(("arbitrary", "arbitrary")),
        name="conformer_conv",
    )(conv_in, w, b, g, beta)


def _ssm_kernel(u_ref, er_ref, ei_ref, tt_ref, fr_ref, fi_ref, apr_ref, api_ref, y_ref, xr_ref, xi_ref):
    cc = SSM_TILE
    n = SSM_LANES
    pair = 2 * SSM_CHUNK * SSM_GROUP
    j = pl.program_id(1)

    @pl.when(j == 0)
    def _():
        xr_ref[0:SUBLANES, :] = jnp.zeros((SUBLANES, n), F32)
        xi_ref[0:SUBLANES, :] = jnp.zeros((SUBLANES, n), F32)

    @pl.when(j > 0)
    def _():
        xr_ref[0:SUBLANES, :] = xr_ref[cc:cc + SUBLANES, :]
        xi_ref[0:SUBLANES, :] = xi_ref[cc:cc + SUBLANES, :]

    for q in range(SSM_GROUPS // 2):
        u = u_ref[:, q * pair:(q + 1) * pair]
        xr_ref[SUBLANES:SUBLANES + cc, q * LANES:(q + 1) * LANES] = jnp.dot(u, er_ref[q], preferred_element_type=F32)
        xi_ref[SUBLANES:SUBLANES + cc, q * LANES:(q + 1) * LANES] = jnp.dot(u, ei_ref[q], preferred_element_type=F32)

    apr = apr_ref[0:SUBLANES, :]
    api = api_ref[0:SUBLANES, :]

    def block(r, carry):
        cr, ci = carry
        start = pl.multiple_of((r + 1) * SUBLANES, SUBLANES)
        xr = xr_ref[pl.ds(start, SUBLANES), :]
        xi = xi_ref[pl.ds(start, SUBLANES), :]
        for k, shift in enumerate(SSM_SHIFTS):
            ar = apr_ref[(k + 1) * SUBLANES:(k + 2) * SUBLANES, :]
            ai = api_ref[(k + 1) * SUBLANES:(k + 2) * SUBLANES, :]
            sr = pltpu.roll(xr, shift, 0)
            si = pltpu.roll(xi, shift, 0)
            xr, xi = xr + (ar * sr - ai * si), xi + (ar * si + ai * sr)
        xr, xi = xr + (apr * cr - api * ci), xi + (apr * ci + api * cr)
        xr_ref[pl.ds(start, SUBLANES), :] = xr
        xi_ref[pl.ds(start, SUBLANES), :] = xi
        return xr[SUBLANES - 1:SUBLANES, :], xi[SUBLANES - 1:SUBLANES, :]

    first = (xr_ref[SUBLANES - 1:SUBLANES, :], xi_ref[SUBLANES - 1:SUBLANES, :])
    lax.fori_loop(0, cc // SUBLANES, block, first, unroll=True)

    prev_r = xr_ref[SUBLANES - 1:SUBLANES - 1 + cc, :].astype(BF16)
    prev_i = xi_ref[SUBLANES - 1:SUBLANES - 1 + cc, :].astype(BF16)
    half = pair // 2
    for q in range(SSM_GROUPS // 2):
        y = (jnp.dot(prev_r[:, q * LANES:(q + 1) * LANES], fr_ref[q], preferred_element_type=F32)
             + jnp.dot(prev_i[:, q * LANES:(q + 1) * LANES], fi_ref[q], preferred_element_type=F32))
        for g in (2 * q, 2 * q + 1):
            lo = g * half
            intra = jnp.dot(u_ref[:, lo:lo + half], tt_ref[g], preferred_element_type=F32)
            y_ref[:, lo:lo + half] = intra + y[:, lo - q * pair:lo - q * pair + half]


def _s5_scan(u2, w, B, L):
    rows = u2.shape[0]
    cc = SSM_TILE
    nt = (L // SSM_CHUNK) // cc
    const = lambda a: pl.BlockSpec(a.shape, lambda bb, j: (0,) * a.ndim)
    names = ("er", "ei", "tt", "fr", "fi", "apr", "api")
    return pl.pallas_call(
        _ssm_kernel,
        grid=(B, nt),
        in_specs=[pl.BlockSpec((cc, u2.shape[1]), lambda bb, j: (bb * nt + j, 0))] + [const(w[k]) for k in names],
        out_specs=pl.BlockSpec((cc, u2.shape[1]), lambda bb, j: (bb * nt + j, 0)),
        out_shape=jax.ShapeDtypeStruct((rows, u2.shape[1]), F32),
        scratch_shapes=[pltpu.VMEM((cc + 2 * SUBLANES, SSM_LANES), F32),
                        pltpu.VMEM((cc + 2 * SUBLANES, SSM_LANES), F32)],
        compiler_params=_params(("arbitrary", "arbitrary")),
        name="s5_scan",
    )(u2, *(w[k] for k in names))


def _ssm_gate_kernel(y_ref, u_ref, d_ref, gw_ref, gb_ref, o_ref):
    y = y_ref[...] + u_ref[...] * d_ref[...]
    g = jax.nn.gelu(y)
    z = jnp.dot(g.astype(BF16), gw_ref[...], preferred_element_type=F32) + gb_ref[...]
    o_ref[...] = (g * jax.nn.sigmoid(z)).astype(o_ref.dtype)


def _ssm_gate(y, u, d, gw, gb):
    T = y.shape[0]
    tm = PROJ_ROWS
    rows = pl.BlockSpec((tm, SSM_WIDTH), lambda i: (i, 0))
    const = lambda a: pl.BlockSpec(a.shape, lambda i: (0, 0))
    return pl.pallas_call(
        _ssm_gate_kernel,
        grid=(T // tm,),
        in_specs=[rows, rows, const(d), const(gw), const(gb)],
        out_specs=rows,
        out_shape=jax.ShapeDtypeStruct((T, SSM_WIDTH), BF16),
        compiler_params=_params(("arbitrary",)),
        name="s5_gate",
    )(y, u, d, gw, gb)


def _s5_ssm(ssm_in, w, d, gw, gb, B, L):
    T = B * L
    G, H, Q = SSM_GROUPS, SSM_GROUP, SSM_CHUNK
    to_chunks = lambda a: a.reshape(T // Q, Q, G, H).transpose(0, 2, 1, 3).reshape(T // Q, G * Q * H)
    y2 = _s5_scan(to_chunks(ssm_in).astype(BF16), w, B, L)
    y = y2.reshape(T // Q, G, Q, H).transpose(0, 2, 1, 3).reshape(T, G * H)
    return _ssm_gate(y, ssm_in, d, gw, gb)


def _ssm_weights(lam_re, lam_im, log_dt, b_re, b_im, c_re, c_im):
    G, P, H, Q = SSM_GROUPS, SSM_STATE, SSM_GROUP, SSM_CHUNK
    dt = jnp.exp(log_dt.astype(F32))[:, None]
    lr, li = lam_re.astype(F32), lam_im.astype(F32)
    mag = jnp.exp(lr * dt)
    ar, ai = mag * jnp.cos(li * dt), mag * jnp.sin(li * dt)
    den = lr * lr + li * li
    zr = ((ar - 1.0) * lr + ai * li) / den
    zi = (ai * lr - (ar - 1.0) * li) / den
    bre, bim = b_re.astype(F32), b_im.astype(F32)
    bbr = zr[..., None] * bre - zi[..., None] * bim
    bbi = zr[..., None] * bim + zi[..., None] * bre
    cr, ci = c_re.astype(F32), c_im.astype(F32)

    def powers(xr, xi, count):
        pr, pi = [xr], [xi]
        for _ in range(count - 1):
            pr, pi = pr + [pr[-1] * xr - pi[-1] * xi], pi + [pr[-1] * xi + pi[-1] * xr]
        return pr, pi

    pr, pi = powers(ar, ai, Q)
    pw_r = jnp.stack([jnp.ones_like(ar)] + pr)
    pw_i = jnp.stack([jnp.zeros_like(ai)] + pi)
    ab_r = pw_r[..., None] * bbr - pw_i[..., None] * bbi
    ab_i = pw_r[..., None] * bbi + pw_i[..., None] * bbr
    ca_r = cr[None] * pw_r[:, :, None, :] - ci[None] * pw_i[:, :, None, :]
    ca_i = cr[None] * pw_i[:, :, None, :] + ci[None] * pw_r[:, :, None, :]
    lag = jnp.einsum('gop,dgpi->dgoi', cr, ab_r[:Q]) - jnp.einsum('gop,dgpi->dgoi', ci, ab_i[:Q])
    place = jnp.stack([jnp.eye(Q, k=d, dtype=F32) for d in range(Q)])
    tt = jnp.einsum('dst,dgoi->gsito', place, lag).reshape(G, Q * H, Q * H)
    rev = slice(Q - 1, None, -1)
    e_r = jnp.transpose(ab_r[:Q][rev], (1, 0, 3, 2)).reshape(G, Q * H, P)
    e_i = jnp.transpose(ab_i[:Q][rev], (1, 0, 3, 2)).reshape(G, Q * H, P)
    f_r = jnp.transpose(ca_r[1:], (1, 3, 0, 2)).reshape(G, P, Q * H)
    f_i = -jnp.transpose(ca_i[1:], (1, 3, 0, 2)).reshape(G, P, Q * H)

    def pair_rows(m):
        z = jnp.zeros_like(m[0::2])
        return jnp.concatenate([jnp.concatenate([m[0::2], z], axis=2), jnp.concatenate([z, m[1::2]], axis=2)], axis=1)

    aq_r, aq_i = pr[Q - 1].reshape(1, G * P), pi[Q - 1].reshape(1, G * P)
    sr, si = powers(aq_r, aq_i, SUBLANES)
    rows = jnp.arange(SUBLANES)[:, None]
    tr, ti = list(sr), list(si)
    for shift in SSM_SHIFTS:
        tr.append(jnp.where(rows >= shift, sr[shift - 1], 0.0))
        ti.append(jnp.where(rows >= shift, si[shift - 1], 0.0))
    return dict(er=pair_rows(e_r).astype(BF16), ei=pair_rows(e_i).astype(BF16), tt=tt.astype(BF16),
                fr=pair_rows(f_r).astype(BF16), fi=pair_rows(f_i).astype(BF16),
                apr=jnp.concatenate(tr, axis=0), api=jnp.concatenate(ti, axis=0))


def _outproj_kernel(x_ref, a_ref, c_ref, s_ref, w_ref, g_ref, wr_ref, br_ref,
                    x1_ref, h2_ref, route_ref, routet_ref, wcnt_ref):
    tm = PROJ_ROWS
    o1, o2 = ATTN_WIDTH, ATTN_WIDTH + CONV_WIDTH
    y = (jnp.dot(a_ref[...], w_ref[0:o1, :], preferred_element_type=F32)
         + jnp.dot(c_ref[...], w_ref[o1:o2, :], preferred_element_type=F32)
         + jnp.dot(s_ref[...], w_ref[o2:, :], preferred_element_type=F32))
    x1 = x_ref[...] + y
    x1_ref[...] = x1
    ms = jnp.mean(x1 * x1, axis=-1, keepdims=True)
    h2 = (x1 * lax.rsqrt(ms + RMS_EPS) * g_ref[...]).astype(BF16)
    h2_ref[...] = h2

    logits = jnp.dot(h2, wr_ref[...], preferred_element_type=F32) + br_ref[...]
    col = lax.broadcasted_iota(jnp.int32, logits.shape, 1)

    def first_max(vals):
        top = jnp.max(vals, axis=-1, keepdims=True)
        return top, jnp.min(jnp.where(vals == top, col, LANES), axis=-1, keepdims=True)

    glog = jnp.where(col < N_GROUPS, logits, NEG_BIG)
    gmax, gidx = first_max(glog)
    gp = 1.0 / jnp.sum(jnp.exp(glog - gmax), axis=-1, keepdims=True)
    lo = N_GROUPS + gidx * EXPERTS_PER_GROUP
    e = jnp.where((col >= lo) & (col < lo + EXPERTS_PER_GROUP), logits, NEG_BIG)
    v1, i1 = first_max(e)
    e = jnp.where(col == i1, NEG_BIG, e)
    v2, i2 = first_max(e)
    ex = jnp.exp(v2 - v1)
    w1 = gp * (1.0 / (1.0 + ex))
    w2 = gp * (ex / (1.0 + ex))

    hit1 = col == i1
    hit2 = col == i2
    onehot = jnp.where(hit1 | hit2, 1.0, 0.0)
    r = lax.broadcasted_iota(jnp.int32, (tm, tm), 0)
    c = lax.broadcasted_iota(jnp.int32, (tm, tm), 1)
    before = jnp.where(r > c, 1.0, 0.0).astype(BF16)
    prior = jnp.dot(before, onehot.astype(BF16), preferred_element_type=F32)
    count = jnp.sum(onehot, axis=0, keepdims=True)
    count = jnp.floor((count + (SUBLANES - 1)) * (1.0 / SUBLANES)) * SUBLANES
    wcnt_ref[0] = count
    run = jnp.broadcast_to(count, (SUBLANES, LANES))
    lane8 = lax.broadcasted_iota(jnp.int32, (SUBLANES, LANES), 1)
    shift = 1
    while shift < LANES:
        run = run + jnp.where(lane8 >= shift, pltpu.roll(run, shift, 1), 0.0)
        shift *= 2
    where_to = prior + (run[0:1, :] - count)
    pos1 = jnp.sum(jnp.where(hit1, where_to, 0.0), axis=-1, keepdims=True)
    pos2 = jnp.sum(jnp.where(hit2, where_to, 0.0), axis=-1, keepdims=True)

    fields = ((i1 - N_GROUPS).astype(F32), (i2 - N_GROUPS).astype(F32), w1, w2, pos1, pos2)
    route = jnp.zeros(logits.shape, F32)
    for k, val in enumerate(fields):
        route = jnp.where(col == k, val, route)
    route_ref[...] = route
    routet_ref[...] = jnp.transpose(route)[0:SUBLANES, :]


def _outproj_route(x2, a, c, s, w_out, g, wr, br):
    T = x2.shape[0]
    tm = PROJ_ROWS
    nt = T // tm
    rows = lambda n: pl.BlockSpec((tm, n), lambda i: (i, 0))
    const = lambda arr: pl.BlockSpec(arr.shape, lambda i: (0, 0))
    return pl.pallas_call(
        _outproj_kernel,
        grid=(nt,),
        in_specs=[rows(D_MODEL), rows(ATTN_WIDTH), rows(CONV_WIDTH), rows(SSM_WIDTH),
                  const(w_out), const(g), const(wr), const(br)],
        out_specs=[rows(D_MODEL), rows(D_MODEL), rows(LANES),
                   pl.BlockSpec((SUBLANES, tm), lambda i: (0, i)),
                   pl.BlockSpec((1, 1, LANES), lambda i: (i, 0, 0))],
        out_shape=[jax.ShapeDtypeStruct((T, D_MODEL), F32), jax.ShapeDtypeStruct((T, D_MODEL), BF16),
                   jax.ShapeDtypeStruct((T, LANES), F32), jax.ShapeDtypeStruct((SUBLANES, T), F32),
                   jax.ShapeDtypeStruct((nt, 1, LANES), F32)],
        compiler_params=_params(("arbitrary",)),
        name="outproj_route",
    )(x2, a, c, s, w_out, g, wr, br)


SLOTS = 2 * PROJ_ROWS + N_EXPERTS * SUBLANES
TILES = SLOTS // SUBLANES
assert PROJ_ROWS % SUBLANES == 0 and MOE_BLOCK % SUBLANES == 0


def _for_each_piece(n, largest, fn):
    off = jnp.int32(0)
    size = largest
    while size >= SUBLANES:
        take = (n // size) & 1

        @pl.when(take == 1)
        def _(off=off, size=size):
            fn(pl.multiple_of(off, SUBLANES), size)

        off = off + take * size
        size //= 2


def _tile_copy(buf_ref, tile, hbm_ref, hbm_tile, sem, to_hbm):
    local = buf_ref.at[pl.ds(pl.multiple_of(tile * SUBLANES, SUBLANES), SUBLANES), :]
    remote = hbm_ref.at[pl.ds(pl.multiple_of(hbm_tile * SUBLANES, SUBLANES), SUBLANES), :]
    return pltpu.make_async_copy(local, remote, sem) if to_hbm else pltpu.make_async_copy(remote, local, sem)


def _start_tiles(count, table_ref, buf_ref, hbm_ref, sem, to_hbm):
    def body(c, carry):
        _tile_copy(buf_ref, c, hbm_ref, table_ref[0, 0, c], sem, to_hbm).start()
        return carry

    lax.fori_loop(0, count, body, 0)


def _wait_tiles(count, buf_ref, hbm_ref, sem, to_hbm):
    def body(c, carry):
        _tile_copy(buf_ref, 0, hbm_ref, 0, sem, to_hbm).wait()
        return carry

    lax.fori_loop(0, count, body, 0)


def _dispatch_kernel(tot_ref, pads_ref, padn_ref, nused_ref, dst_ref, h_ref, rt_ref, xs_hbm,
                     loc_ref, zero_ref, sem, zsem, *, n_blocks):
    w = pl.program_id(0)
    last = pl.num_programs(0) - 1

    @pl.when(w == 0)
    def _():
        zero_ref[...] = jnp.zeros(zero_ref.shape, zero_ref.dtype)

        def fill(act):
            def per_expert(e, carry):
                start = pl.multiple_of(pads_ref[e], SUBLANES)

                def piece(off, size):
                    act(pltpu.make_async_copy(
                        zero_ref.at[pl.ds(0, size), :],
                        xs_hbm.at[pl.ds(pl.multiple_of(start + off, SUBLANES), size), :], zsem))
                _for_each_piece(padn_ref[e], MOE_BLOCK // 2, piece)
                return carry

            def per_block(b, carry):
                first = pl.multiple_of(b * MOE_BLOCK, MOE_BLOCK)
                act(pltpu.make_async_copy(zero_ref, xs_hbm.at[pl.ds(first, MOE_BLOCK), :], zsem))
                return carry

            lax.fori_loop(0, N_EXPERTS, per_expert, 0)
            lax.fori_loop(nused_ref[0], n_blocks, per_block, 0)

        fill(lambda cp: cp.start())
        fill(lambda cp: cp.wait())

    pos1 = rt_ref[4:5, :].astype(jnp.int32)
    pos2 = rt_ref[5:6, :].astype(jnp.int32)
    slot = lax.broadcasted_iota(jnp.int32, (SLOTS, PROJ_ROWS), 0)
    pick = jnp.where((slot == pos1) | (slot == pos2), 1.0, 0.0).astype(BF16)
    buf = loc_ref.at[w % 2]
    buf[...] = jnp.dot(pick, h_ref[...], preferred_element_type=F32)

    @pl.when(w > 0)
    def _():
        _wait_tiles(tot_ref[jnp.maximum(w - 1, 0)] // SUBLANES, buf, xs_hbm, sem, True)

    _start_tiles(tot_ref[w] // SUBLANES, dst_ref, buf, xs_hbm, sem, True)

    @pl.when(w == last)
    def _():
        _wait_tiles(tot_ref[w] // SUBLANES, buf, xs_hbm, sem, True)


def _dispatch(plan, h2, routet, n_blocks):
    T = h2.shape[0]
    tm = PROJ_ROWS
    return pl.pallas_call(
        functools.partial(_dispatch_kernel, n_blocks=n_blocks),
        grid_spec=pltpu.PrefetchScalarGridSpec(
            num_scalar_prefetch=4,
            grid=(T // tm,),
            in_specs=[pl.BlockSpec((1, 1, TILES), lambda w, *_: (w, 0, 0), memory_space=pltpu.SMEM),
                      pl.BlockSpec((tm, D_MODEL), lambda w, *_: (w, 0)),
                      pl.BlockSpec((SUBLANES, tm), lambda w, *_: (0, w))],
            out_specs=pl.BlockSpec(memory_space=pl.ANY),
            scratch_shapes=[pltpu.VMEM((2, SLOTS, D_MODEL), F32), pltpu.VMEM((MOE_BLOCK, D_MODEL), F32),
                            pltpu.SemaphoreType.DMA, pltpu.SemaphoreType.DMA],
        ),
        out_shape=jax.ShapeDtypeStruct((n_blocks * MOE_BLOCK, D_MODEL), F32),
        compiler_params=_params(("arbitrary",)),
        name="moe_dispatch",
    )(plan["tot"], plan["pad_start"], plan["pad_len"], plan["nused"], plan["dst"], h2, routet)


def _expert_kernel(blk_e_ref, nused_ref, x_ref, wg_ref, wu_ref, wd_ref, o_ref):
    b = pl.program_id(0)

    @pl.when(b < nused_ref[0])
    def _():
        x = x_ref[...].astype(BF16)
        gate = jnp.dot(x, wg_ref[0, 0].astype(BF16), preferred_element_type=F32)
        up = jnp.dot(x, wu_ref[0, 0].astype(BF16), preferred_element_type=F32)
        act = (jax.nn.silu(gate) * up).astype(BF16)
        o_ref[...] = jnp.dot(act, wd_ref[0, 0].astype(BF16), preferred_element_type=F32)

    @pl.when(b >= nused_ref[0])
    def _():
        o_ref[...] = jnp.zeros(o_ref.shape, o_ref.dtype)


def _experts(plan, xs, wg, wu, wd, layer, n_blocks):
    return pl.pallas_call(
        _expert_kernel,
        grid_spec=pltpu.PrefetchScalarGridSpec(
            num_scalar_prefetch=2,
            grid=(n_blocks,),
            in_specs=[pl.BlockSpec((MOE_BLOCK, D_MODEL), lambda b, be, nu: (b, 0)),
                      pl.BlockSpec((1, 1, D_MODEL, EXPERT_FF), lambda b, be, nu: (layer, be[b], 0, 0)),
                      pl.BlockSpec((1, 1, D_MODEL, EXPERT_FF), lambda b, be, nu: (layer, be[b], 0, 0)),
                      pl.BlockSpec((1, 1, EXPERT_FF, D_MODEL), lambda b, be, nu: (layer, be[b], 0, 0))],
            out_specs=pl.BlockSpec((MOE_BLOCK, D_MODEL), lambda b, be, nu: (b, 0)),
        ),
        out_shape=jax.ShapeDtypeStruct((n_blocks * MOE_BLOCK, D_MODEL), F32),
        compiler_params=_params(("arbitrary",)),
        name="moe_experts",
    )(plan["blk_e"], plan["nused"], xs, wg, wu, wd)


def _combine_kernel(tot_ref, src_ref, src_next_ref, yb_hbm, x_ref, route_ref, g_ref, o_ref, loc_ref, sems,
                    *, final_norm):
    w = pl.program_id(0)
    last = pl.num_programs(0) - 1

    def fetch(win, table_ref):
        buf = loc_ref.at[win % 2]
        tiles = tot_ref[win] // SUBLANES

        def clear(r, carry):
            buf[pl.ds(pl.multiple_of(r * SUBLANES, SUBLANES), SUBLANES), :] = jnp.zeros((SUBLANES, D_MODEL), F32)
            return carry

        lax.fori_loop(tiles, TILES, clear, 0)
        _start_tiles(tiles, table_ref, buf, yb_hbm, sems.at[win % 2], False)

    @pl.when(w == 0)
    def _():
        fetch(w, src_ref)

    @pl.when(w < last)
    def _():
        fetch(jnp.minimum(w + 1, last), src_next_ref)

    buf = loc_ref.at[w % 2]
    _wait_tiles(tot_ref[w] // SUBLANES, buf, yb_hbm, sems.at[w % 2], False)
    route = route_ref[...]
    pos1 = route[:, 4:5].astype(jnp.int32)
    pos2 = route[:, 5:6].astype(jnp.int32)
    slot = lax.broadcasted_iota(jnp.int32, (PROJ_ROWS, SLOTS), 1)
    sel = jnp.where(slot == pos1, route[:, 2:3], 0.0) + jnp.where(slot == pos2, route[:, 3:4], 0.0)
    sel_hi = sel.astype(BF16)
    sel_lo = (sel - sel_hi.astype(F32)).astype(BF16)
    y = buf[...]
    y_hi = y.astype(BF16)
    y_lo = (y - y_hi.astype(F32)).astype(BF16)
    moe = (jnp.dot(sel_hi, y_hi, preferred_element_type=F32)
           + jnp.dot(sel_lo, y_hi, preferred_element_type=F32)
           + jnp.dot(sel_hi, y_lo, preferred_element_type=F32))
    x = x_ref[...] + moe
    if final_norm:
        ms = jnp.mean(x * x, axis=-1, keepdims=True)
        x = x * lax.rsqrt(ms + RMS_EPS) * g_ref[...]
    o_ref[...] = x


def _combine(plan, yb, x1, route, g, final_norm):
    T = x1.shape[0]
    tm = PROJ_ROWS
    nt = T // tm
    table = lambda shift: pl.BlockSpec((1, 1, TILES), lambda w, *_: (jnp.minimum(w + shift, nt - 1), 0, 0),
                                       memory_space=pltpu.SMEM)
    return pl.pallas_call(
        functools.partial(_combine_kernel, final_norm=final_norm),
        grid_spec=pltpu.PrefetchScalarGridSpec(
            num_scalar_prefetch=1,
            grid=(nt,),
            in_specs=[table(0), table(1),
                      pl.BlockSpec(memory_space=pl.ANY),
                      pl.BlockSpec((tm, D_MODEL), lambda w, *_: (w, 0)),
                      pl.BlockSpec((tm, LANES), lambda w, *_: (w, 0)),
                      pl.BlockSpec((1, D_MODEL), lambda w, *_: (0, 0))],
            out_specs=pl.BlockSpec((tm, D_MODEL), lambda w, *_: (w, 0)),
            scratch_shapes=[pltpu.VMEM((2, SLOTS, D_MODEL), F32), pltpu.SemaphoreType.DMA((2,))],
        ),
        out_shape=jax.ShapeDtypeStruct((T, D_MODEL), F32),
        compiler_params=_params(("arbitrary",)),
        name="moe_combine",
    )(plan["tot"], plan["dst"], plan["dst"], yb, x1, route, g)


def _routing_plan(wcnt, T):
    n = wcnt[:, 0, N_GROUPS:N_GROUPS + N_EXPERTS].astype(jnp.int32)
    cnt = jnp.sum(n, axis=0)
    nblk = (cnt + MOE_BLOCK - 1) // MOE_BLOCK
    blk_end = jnp.cumsum(nblk)
    first_row = (blk_end - nblk) * MOE_BLOCK
    ls = jnp.cumsum(n, axis=1) - n
    gs = first_row[None, :] + jnp.cumsum(n, axis=0) - n
    worst_rows = T * 2 + (T // PROJ_ROWS) * N_EXPERTS * SUBLANES
    n_blocks = -(-worst_rows // MOE_BLOCK) + N_EXPERTS
    blk_e = jnp.minimum(jnp.sum(jnp.arange(n_blocks)[:, None] >= blk_end[None, :], axis=1), N_EXPERTS - 1)
    tile = jnp.arange(TILES)[None, :, None]
    lt, nt8, gt = (a[:, None, :] // SUBLANES for a in (ls, n, gs))
    dst = jnp.sum(jnp.where((tile >= lt) & (tile < lt + nt8), gt + tile - lt, 0), axis=2)
    i32 = lambda a: a.reshape(-1).astype(jnp.int32)
    plan = dict(tot=i32(jnp.sum(n, axis=1)), pad_start=i32(first_row + cnt), pad_len=i32(nblk * MOE_BLOCK - cnt),
                nused=i32(blk_end[-1:]), blk_e=i32(blk_e),
                dst=dst.astype(jnp.int32).reshape(n.shape[0], 1, TILES))
    return plan, n_blocks


def _pack_w_in(w):
    scale = QK_DIM ** -0.5 * LOG2E
    q1, q2, k1, k2 = (w[:, i * QK_COLS:(i + 1) * QK_COLS].reshape(D_MODEL, N_HEADS, QK_DIM) for i in range(4))
    qq = (jnp.concatenate([q1, q2], axis=-1) * scale).reshape(D_MODEL, ATTN_WIDTH)
    kk = jnp.concatenate([k1, k2], axis=-1).reshape(D_MODEL, ATTN_WIDTH)
    v0 = 4 * QK_COLS
    packed = jnp.concatenate([qq, kk, w[:, v0 + ATTN_WIDTH:]], axis=1).astype(BF16)
    return packed, jnp.transpose(w[:, v0:v0 + ATTN_WIDTH]).astype(BF16)


def kernel(x, rel_bias, ln1_g, w_in, lam_q1, lam_k1, lam_q2, lam_k2, subln_g, conv_w, conv_b, conv_ln_g, conv_ln_b, ssm_lam_re, ssm_lam_im, ssm_log_dt, ssm_b_re, ssm_b_im, ssm_c_re, ssm_c_im, ssm_d, ssm_glu_w, ssm_glu_b, w_out, ln2_g, group_router_w, group_router_b, expert_router_w, expert_router_b, w_gate, w_up, w_down, final_g):
    B, L, D = x.shape
    T = B * L
    depth = w_in.shape[0]
    assert D == D_MODEL and L % CONV_ROWS == 0 and L % ATTN_TILE == 0 and L % (SSM_CHUNK * SSM_TILE) == 0
    assert PROJ_ROWS == ATTN_TILE and T % PROJ_ROWS == 0 and (2 * T) % MOE_BLOCK == 0
    x2 = x.reshape(T, D)
    bias_diag, bias_sub = _bias_tiles(rel_bias)
    row = lambda v: v.astype(F32).reshape(1, -1)
    for l in range(depth):
        lam_init = 0.8 - 0.6 * math.exp(-0.3 * l)
        lam = (jnp.exp(jnp.sum(lam_q1[l].astype(F32) * lam_k1[l].astype(F32)))
               - jnp.exp(jnp.sum(lam_q2[l].astype(F32) * lam_k2[l].astype(F32))) + lam_init).reshape(1)
        qq, kk, conv_in, ssm_in, vt = _inproj(x2, row(ln1_g[l]), *_pack_w_in(w_in[l]))
        a = _attention(qq, kk, vt, lam, bias_diag, bias_sub, subln_g[l].astype(F32).reshape(V_DIM, 1),
                       1.0 - lam_init, B, L)
        c = _conformer_conv(conv_in, conv_w[l], row(conv_b[l]), row(conv_ln_g[l]), row(conv_ln_b[l]), B, L)
        ssm_w = _ssm_weights(ssm_lam_re[l], ssm_lam_im[l], ssm_log_dt[l], ssm_b_re[l], ssm_b_im[l],
                             ssm_c_re[l], ssm_c_im[l])
        s = _s5_ssm(ssm_in, ssm_w, row(ssm_d[l]), ssm_glu_w[l].astype(BF16), row(ssm_glu_b[l]), B, L)
        wr = jnp.zeros((D, LANES), F32).at[:, :N_GROUPS].set(group_router_w[l]) \
            .at[:, N_GROUPS:N_GROUPS + N_EXPERTS].set(expert_router_w[l]).astype(BF16)
        br = jnp.zeros((1, LANES), F32).at[0, :N_GROUPS].set(group_router_b[l]) \
            .at[0, N_GROUPS:N_GROUPS + N_EXPERTS].set(expert_router_b[l])
        x1, h2, route, routet, wcnt = _outproj_route(x2, a, c, s, w_out[l].astype(BF16), row(ln2_g[l]), wr, br)
        plan, n_blocks = _routing_plan(wcnt, T)
        xs = _dispatch(plan, h2, routet, n_blocks)
        yb = _experts(plan, xs, w_gate, w_up, w_down, l, n_blocks)
        x2 = _combine(plan, yb, x1, route, row(final_g), final_norm=(l == depth - 1))
    return x2.reshape(B, L, D)
```

```python
import functools
import math

import jax
import jax.numpy as jnp
from jax import lax
from jax.experimental import pallas as pl
from jax.experimental.pallas import tpu as pltpu

F32 = jnp.float32
BF16 = jnp.bfloat16

D_MODEL = 1024
N_HEADS = 4
QK_DIM = 64
V_DIM = 128
ATTN_WIDTH = N_HEADS * V_DIM
QK_COLS = N_HEADS * QK_DIM
CONV_WIDTH = 256
CONV_TAPS = 31
SSM_WIDTH = 256
SSM_GROUP = 16
SSM_GROUPS = 16
SSM_STATE = 64
SSM_LANES = SSM_GROUPS * SSM_STATE
REL_BUCKETS = 32
REL_MAX_EXACT = 16
REL_MAX_DIST = 128
N_GROUPS = 4
EXPERTS_PER_GROUP = 8
N_EXPERTS = N_GROUPS * EXPERTS_PER_GROUP
EXPERT_FF = 512
RMS_EPS = 1e-6
LN_EPS = 1e-5
NEG_BIG = -1e30

LANES = 128
SUBLANES = 8
VMEM_LIMIT = 48 * 1024 * 1024

PROJ_ROWS = 512
ATTN_TILE = 512
ATTN_CHUNK = 32
ONES_ROWS = 16
ATTN_HEADS_PER_STEP = 2
LOG2E = math.log2(math.e)
CONV_ROWS = 512
CONV_HALO = 32
SSM_ROWS = 256
SSM_UNROLL = 8
SSM_SHIFTS = (1, 2, 4)
MOE_BLOCK = 512


def _params(sem):
    return pltpu.CompilerParams(dimension_semantics=sem, vmem_limit_bytes=VMEM_LIMIT)


def _inproj_kernel(x_ref, g_ref, w_ref, wvt_ref, qq_ref, kk_ref, conv_ref, ssm_ref, vt_ref):
    x = x_ref[...]
    ms = jnp.mean(x * x, axis=-1, keepdims=True)
    h = (x * lax.rsqrt(ms + RMS_EPS) * g_ref[...]).astype(BF16)
    o = 0
    for ref in (qq_ref, kk_ref, conv_ref, ssm_ref):
        n = ref.shape[-1]
        ref[...] = jnp.dot(h, w_ref[:, o:o + n], preferred_element_type=F32).astype(ref.dtype)
        o += n
    vt_ref[0] = lax.dot_general(wvt_ref[...], h, (((1,), (1,)), ((), ())),
                                preferred_element_type=F32).astype(vt_ref.dtype)


def _inproj(x2, g, w, wvt):
    T = x2.shape[0]
    tm = PROJ_ROWS
    widths = (ATTN_WIDTH, ATTN_WIDTH, 2 * CONV_WIDTH, SSM_WIDTH)
    dtypes = (BF16, BF16, F32, F32)
    return pl.pallas_call(
        _inproj_kernel,
        grid=(T // tm,),
        in_specs=[
            pl.BlockSpec((tm, D_MODEL), lambda i: (i, 0)),
            pl.BlockSpec((1, D_MODEL), lambda i: (0, 0)),
            pl.BlockSpec(w.shape, lambda i: (0, 0)),
            pl.BlockSpec(wvt.shape, lambda i: (0, 0)),
        ],
        out_specs=[pl.BlockSpec((tm, n), lambda i: (i, 0)) for n in widths]
        + [pl.BlockSpec((1, ATTN_WIDTH, tm), lambda i: (i, 0, 0))],
        out_shape=[jax.ShapeDtypeStruct((T, n), dt) for n, dt in zip(widths, dtypes)]
        + [jax.ShapeDtypeStruct((T // tm, ATTN_WIDTH, tm), BF16)],
        compiler_params=_params(("arbitrary",)),
        name="inproj",
    )(x2, g, w, wvt)


def _attn_kernel(lam_ref, q_ref, k_ref, vt_ref, bd_ref, bs_ref, g_ref, o_ref, *scratch, out_scale):
    t = ATTN_TILE
    nh = ATTN_HEADS_PER_STEP
    qi = pl.program_id(2)
    per_head = len(scratch) // nh
    heads = [scratch[h * per_head:(h + 1) * per_head] for h in range(nh)]
    cols = lambda h: slice(h * LANES, (h + 1) * LANES)

    for h, (qs_ref, m_ref, acc_ref, sa_ref, sb_ref, pa_ref, pb_ref, aa_ref, ab_ref) in enumerate(heads):
        q = q_ref[:, cols(h)].astype(F32)
        lane = lax.broadcasted_iota(jnp.int32, q.shape, 1)
        qs_ref[0:t, :] = jnp.where(lane < QK_DIM, q, 0.0).astype(BF16)
        qs_ref[t:2 * t, :] = jnp.where(lane >= QK_DIM, q, 0.0).astype(BF16)
        m_ref[...] = jnp.full(m_ref.shape, NEG_BIG, F32)
        acc_ref[...] = jnp.zeros(acc_ref.shape, F32)
        pb_ref[...] = jnp.zeros(pb_ref.shape, BF16)
        ab_ref[...] = jnp.ones(ab_ref.shape, F32)

    chunks = [(c, c + ATTN_CHUNK) for c in range(0, t, ATTN_CHUNK)]
    fold = lambda a: a.reshape(ATTN_CHUNK // SUBLANES, SUBLANES, 2 * t)

    def scores(j, which):
        for h, refs in enumerate(heads):
            k = k_ref[pl.ds(pl.multiple_of(j * t, t), t), cols(h)]
            refs[3 + which][...] = lax.dot_general(k, refs[0][...], (((1,), (1,)), ((), ())),
                                                   preferred_element_type=F32)

    def softmax(which, bias_ref=None):
        for h, refs in enumerate(heads):
            m_ref, s_ref, p_ref, a_ref = refs[1], refs[3 + which], refs[5 + which], refs[7 + which]
            top = jnp.full((SUBLANES, 2 * t), NEG_BIG, F32)
            for lo, hi in chunks:
                s = s_ref[lo:hi, :]
                if bias_ref is not None:
                    b = bias_ref[h, lo:hi, :]
                    s = s + jnp.concatenate([b, b], axis=1)
                    s_ref[lo:hi, :] = s
                top = jnp.maximum(top, jnp.max(fold(s), axis=0))
            m_prev = m_ref[...]
            m_new = jnp.maximum(m_prev, jnp.max(top, axis=0, keepdims=True))
            a_ref[...] = jnp.exp2(m_prev - m_new)
            for lo, hi in chunks:
                p_ref[lo:hi, :] = jnp.exp2(s_ref[lo:hi, :] - m_new).astype(BF16)
            m_ref[...] = m_new

    def values(j, which):
        for h, refs in enumerate(heads):
            acc_ref, p_ref, a_ref = refs[2], refs[5 + which], refs[7 + which]
            lhs = jnp.concatenate([vt_ref[jnp.maximum(j, 0), cols(h), :], jnp.ones((ONES_ROWS, t), BF16)], axis=0)
            acc_ref[...] = a_ref[...] * acc_ref[...] + jnp.dot(lhs, p_ref[...], preferred_element_type=F32)

    SET_A, SET_B = 0, 1
    nfar = jnp.maximum(qi - 1, 0)
    scores(0, SET_A)

    def far_pair(i, carry):
        k = 2 * i
        scores(k + 1, SET_B)
        softmax(SET_A)
        values(k - 1, SET_B)
        scores(k + 2, SET_A)
        softmax(SET_B)
        values(k, SET_A)
        return carry

    lax.fori_loop(0, nfar // 2, far_pair, 0)

    @pl.when(qi == 0)
    def _():
        softmax(SET_A, bd_ref)
        values(qi, SET_A)

    @pl.when((qi >= 1) & (nfar % 2 == 0))
    def _():
        scores(qi, SET_B)
        softmax(SET_A, bs_ref)
        values(qi - 2, SET_B)
        softmax(SET_B, bd_ref)
        values(qi - 1, SET_A)
        values(qi, SET_B)

    @pl.when(nfar % 2 == 1)
    def _():
        scores(qi - 1, SET_B)
        softmax(SET_A)
        values(qi - 3, SET_B)
        scores(qi, SET_A)
        softmax(SET_B, bs_ref)
        values(qi - 2, SET_A)
        softmax(SET_A, bd_ref)
        values(qi - 1, SET_B)
        values(qi, SET_A)

    for h, refs in enumerate(heads):
        acc_ref = refs[2]
        acc = acc_ref[0:V_DIM, :]
        l = acc_ref[V_DIM:V_DIM + 1, :]
        a = acc[:, 0:t] / l[:, 0:t] - lam_ref[0] * (acc[:, t:2 * t] / l[:, t:2 * t])
        ms = jnp.mean(a * a, axis=0, keepdims=True)
        y = a * lax.rsqrt(ms + RMS_EPS) * g_ref[...] * out_scale
        o_ref[:, cols(h)] = jnp.transpose(y).astype(o_ref.dtype)


def _attention(qq, kk, vt, lam, bias_diag, bias_sub, subln_g, out_scale, B, L):
    T = B * L
    t = ATTN_TILE
    nh = ATTN_HEADS_PER_STEP
    nq = L // t
    once = pl.Buffered(1)
    head_scratch = [
        pltpu.VMEM((2 * t, LANES), BF16),
        pltpu.VMEM((1, 2 * t), F32),
        pltpu.VMEM((V_DIM + ONES_ROWS, 2 * t), F32),
        pltpu.VMEM((t, 2 * t), F32),
        pltpu.VMEM((t, 2 * t), F32),
        pltpu.VMEM((t, 2 * t), BF16),
        pltpu.VMEM((t, 2 * t), BF16),
        pltpu.VMEM((1, 2 * t), F32),
        pltpu.VMEM((1, 2 * t), F32),
    ]
    return pl.pallas_call(
        functools.partial(_attn_kernel, out_scale=out_scale),
        grid=(B, N_HEADS // nh, nq),
        in_specs=[
            pl.BlockSpec(memory_space=pltpu.SMEM),
            pl.BlockSpec((t, nh * LANES), lambda b, h, i: (b * nq + i, h)),
            pl.BlockSpec((L, nh * LANES), lambda b, h, i: (b, h), pipeline_mode=once),
            pl.BlockSpec((nq, nh * V_DIM, t), lambda b, h, i: (b, h, 0), pipeline_mode=once),
            pl.BlockSpec((nh, t, t), lambda b, h, i: (h, 0, 0), pipeline_mode=once),
            pl.BlockSpec((nh, t, t), lambda b, h, i: (h, 0, 0), pipeline_mode=once),
            pl.BlockSpec((V_DIM, 1), lambda b, h, i: (0, 0)),
        ],
        out_specs=pl.BlockSpec((t, nh * LANES), lambda b, h, i: (b * nq + i, h)),
        out_shape=jax.ShapeDtypeStruct((T, ATTN_WIDTH), BF16),
        scratch_shapes=head_scratch * nh,
        compiler_params=_params(("arbitrary", "arbitrary", "arbitrary")),
        name="diff_attn",
    )(lam, qq, kk, vt, bias_diag, bias_sub, subln_g)


def _rel_bucket(rel):
    n = jnp.maximum(rel, 0)
    nf = jnp.maximum(n, 1).astype(F32)
    large = REL_MAX_EXACT + (jnp.log(nf / REL_MAX_EXACT) / math.log(REL_MAX_DIST / REL_MAX_EXACT)
                             * (REL_BUCKETS - REL_MAX_EXACT)).astype(jnp.int32)
    large = jnp.minimum(large, REL_BUCKETS - 1)
    return jnp.where(n < REL_MAX_EXACT, n, large)


def _bias_tiles(rel_table):
    t = ATTN_TILE
    assert t >= REL_MAX_DIST
    far = rel_table[REL_BUCKETS - 1].astype(F32)
    rel_d = jnp.arange(t)[None, :] - jnp.arange(t)[:, None]

    def lookup(bucket):
        out = jnp.zeros((N_HEADS,) + bucket.shape, F32)
        for b in range(REL_BUCKETS):
            out = jnp.where((bucket == b)[None], (rel_table[b].astype(F32) - far)[:, None, None], out)
        return out * LOG2E

    bd = jnp.where((rel_d >= 0)[None], lookup(_rel_bucket(rel_d)), NEG_BIG)
    return bd, lookup(_rel_bucket(rel_d + t))


def _conv_kernel(u_ref, w_ref, b_ref, g_ref, beta_ref, o_ref, h_ref):
    tt = CONV_ROWS
    j = pl.program_id(1)

    @pl.when(j == 0)
    def _():
        h_ref[0:CONV_HALO, :] = jnp.zeros((CONV_HALO, CONV_WIDTH), F32)

    @pl.when(j > 0)
    def _():
        h_ref[0:CONV_HALO, :] = h_ref[tt:tt + CONV_HALO, :]

    u = u_ref[...]
    h_ref[CONV_HALO:CONV_HALO + tt, :] = u[:, 0:CONV_WIDTH] * jax.nn.sigmoid(u[:, CONV_WIDTH:])
    acc = jnp.broadcast_to(b_ref[...], (tt, CONV_WIDTH))
    off = CONV_HALO - (CONV_TAPS - 1)
    for k in range(CONV_TAPS):
        acc = acc + w_ref[k:k + 1, :] * h_ref[off + k:off + k + tt, :]
    mu = jnp.mean(acc, axis=-1, keepdims=True)
    cen = acc - mu
    var = jnp.mean(cen * cen, axis=-1, keepdims=True)
    y = cen * lax.rsqrt(var + LN_EPS) * g_ref[...] + beta_ref[...]
    o_ref[...] = jax.nn.silu(y).astype(o_ref.dtype)


def _conformer_conv(conv_in, w, b, g, beta, B, L):
    T = B * L
    tt = CONV_ROWS
    nt = L // tt
    vec = pl.BlockSpec((1, CONV_WIDTH), lambda bb, j: (0, 0))
    return pl.pallas_call(
        _conv_kernel,
        grid=(B, nt),
        in_specs=[
            pl.BlockSpec((tt, 2 * CONV_WIDTH), lambda bb, j: (bb * nt + j, 0)),
            pl.BlockSpec((CONV_TAPS, CONV_WIDTH), lambda bb, j: (0, 0)),
            vec, vec, vec,
        ],
        out_specs=pl.BlockSpec((tt, CONV_WIDTH), lambda bb, j: (bb * nt + j, 0)),
        out_shape=jax.ShapeDtypeStruct((T, CONV_WIDTH), BF16),
        scratch_shapes=[pltpu.VMEM((tt + CONV_HALO, CONV_WIDTH), F32)],
        compiler_params=_params(("arbitrary", "arbitrary")),
        name="conformer_conv",
    )(conv_in, w, b, g, beta)


def _ssm_kernel(u_ref, wb_ref, apr_ref, api_ref, wc_ref, d_ref, gw_ref, gb_ref, o_ref,
                xs_ref, carry_ref):
    tt = SSM_ROWS
    n = SSM_LANES
    j = pl.program_id(1)

    @pl.when(j == 0)
    def _():
        carry_ref[...] = jnp.zeros(carry_ref.shape, F32)

    u = u_ref[...]
    xs_ref[...] = jnp.dot(u.astype(BF16), wb_ref[...], preferred_element_type=F32)
    apr = apr_ref[0:SUBLANES, :]
    api = api_ref[0:SUBLANES, :]

    def block(r, carry):
        cr, ci = carry
        start = pl.multiple_of(r * SUBLANES, SUBLANES)
        xr = xs_ref[pl.ds(start, SUBLANES), 0:n]
        xi = xs_ref[pl.ds(start, SUBLANES), n:2 * n]
        for k, shift in enumerate(SSM_SHIFTS):
            ar = apr_ref[(k + 1) * SUBLANES:(k + 2) * SUBLANES, :]
            ai = api_ref[(k + 1) * SUBLANES:(k + 2) * SUBLANES, :]
            sr = pltpu.roll(xr, shift, 0)
            si = pltpu.roll(xi, shift, 0)
            xr, xi = xr + (ar * sr - ai * si), xi + (ar * si + ai * sr)
        xr, xi = xr + (apr * cr - api * ci), xi + (apr * ci + api * cr)
        xs_ref[pl.ds(start, SUBLANES), 0:n] = xr
        xs_ref[pl.ds(start, SUBLANES), n:2 * n] = xi
        return xr[SUBLANES - 1:SUBLANES, :], xi[SUBLANES - 1:SUBLANES, :]

    cr, ci = lax.fori_loop(0, tt // SUBLANES, block, (carry_ref[0:1, :], carry_ref[1:2, :]),
                           unroll=SSM_UNROLL)
    carry_ref[0:1, :] = cr
    carry_ref[1:2, :] = ci

    y = jnp.dot(xs_ref[...].astype(BF16), wc_ref[...], preferred_element_type=F32) + u * d_ref[...]
    g = jax.nn.gelu(y)
    z = jnp.dot(g.astype(BF16), gw_ref[...], preferred_element_type=F32) + gb_ref[...]
    o_ref[...] = (g * jax.nn.sigmoid(z)).astype(o_ref.dtype)


def _s5_ssm(ssm_in, wb, apr, api, wc, d, gw, gb, B, L):
    T = B * L
    tt = SSM_ROWS
    nt = L // tt
    const = lambda a: pl.BlockSpec(a.shape, lambda bb, j: (0, 0))
    return pl.pallas_call(
        _ssm_kernel,
        grid=(B, nt),
        in_specs=[pl.BlockSpec((tt, SSM_WIDTH), lambda bb, j: (bb * nt + j, 0)),
                  const(wb), const(apr), const(api), const(wc), const(d), const(gw), const(gb)],
        out_specs=pl.BlockSpec((tt, SSM_WIDTH), lambda bb, j: (bb * nt + j, 0)),
        out_shape=jax.ShapeDtypeStruct((T, SSM_WIDTH), BF16),
        scratch_shapes=[pltpu.VMEM((tt, 2 * SSM_LANES), F32), pltpu.VMEM((SUBLANES, SSM_LANES), F32)],
        compiler_params=_params(("arbitrary", "arbitrary")),
        name="s5_scan",
    )(ssm_in, wb, apr, api, wc, d, gw, gb)


def _ssm_weights(lam_re, lam_im, log_dt, b_re, b_im, c_re, c_im):
    G, P, H = SSM_GROUPS, SSM_STATE, SSM_GROUP
    dt = jnp.exp(log_dt.astype(F32))[:, None]
    lr, li = lam_re.astype(F32), lam_im.astype(F32)
    mag = jnp.exp(lr * dt)
    ar, ai = mag * jnp.cos(li * dt), mag * jnp.sin(li * dt)
    den = lr * lr + li * li
    zr = ((ar - 1.0) * lr + ai * li) / den
    zi = (ai * lr - (ar - 1.0) * li) / den
    bre, bim = b_re.astype(F32), b_im.astype(F32)
    bbr = zr[..., None] * bre - zi[..., None] * bim
    bbi = zr[..., None] * bim + zi[..., None] * bre
    eye = jnp.eye(G, dtype=F32)
    wb = jnp.concatenate([jnp.einsum('gph,gk->ghkp', bbr, eye).reshape(G * H, G * P),
                          jnp.einsum('gph,gk->ghkp', bbi, eye).reshape(G * H, G * P)], axis=1)
    wc = jnp.concatenate([jnp.einsum('ghp,gk->gpkh', c_re.astype(F32), eye).reshape(G * P, G * H),
                          -jnp.einsum('ghp,gk->gpkh', c_im.astype(F32), eye).reshape(G * P, G * H)], axis=0)
    pr, pi = [ar.reshape(1, G * P)], [ai.reshape(1, G * P)]
    for _ in range(SUBLANES - 1):
        pr, pi = (pr + [pr[-1] * pr[0] - pi[-1] * pi[0]], pi + [pr[-1] * pi[0] + pi[-1] * pr[0]])
    rows = jnp.arange(SUBLANES)[:, None]
    tr, ti = list(pr), list(pi)
    for shift in SSM_SHIFTS:
        tr.append(jnp.where(rows >= shift, pr[shift - 1], 0.0))
        ti.append(jnp.where(rows >= shift, pi[shift - 1], 0.0))
    return wb.astype(BF16), jnp.concatenate(tr, axis=0), jnp.concatenate(ti, axis=0), wc.astype(BF16)


def _outproj_kernel(x_ref, a_ref, c_ref, s_ref, w_ref, g_ref, wr_ref, br_ref,
                    x1_ref, h2_ref, route_ref, routet_ref, wcnt_ref):
    tm = PROJ_ROWS
    o1, o2 = ATTN_WIDTH, ATTN_WIDTH + CONV_WIDTH
    y = (jnp.dot(a_ref[...], w_ref[0:o1, :], preferred_element_type=F32)
         + jnp.dot(c_ref[...], w_ref[o1:o2, :], preferred_element_type=F32)
         + jnp.dot(s_ref[...], w_ref[o2:, :], preferred_element_type=F32))
    x1 = x_ref[...] + y
    x1_ref[...] = x1
    ms = jnp.mean(x1 * x1, axis=-1, keepdims=True)
    h2 = (x1 * lax.rsqrt(ms + RMS_EPS) * g_ref[...]).astype(BF16)
    h2_ref[...] = h2

    logits = jnp.dot(h2, wr_ref[...], preferred_element_type=F32) + br_ref[...]
    col = lax.broadcasted_iota(jnp.int32, logits.shape, 1)

    def first_max(vals):
        top = jnp.max(vals, axis=-1, keepdims=True)
        return top, jnp.min(jnp.where(vals == top, col, LANES), axis=-1, keepdims=True)

    glog = jnp.where(col < N_GROUPS, logits, NEG_BIG)
    gmax, gidx = first_max(glog)
    gp = 1.0 / jnp.sum(jnp.exp(glog - gmax), axis=-1, keepdims=True)
    lo = N_GROUPS + gidx * EXPERTS_PER_GROUP
    e = jnp.where((col >= lo) & (col < lo + EXPERTS_PER_GROUP), logits, NEG_BIG)
    v1, i1 = first_max(e)
    e = jnp.where(col == i1, NEG_BIG, e)
    v2, i2 = first_max(e)
    ex = jnp.exp(v2 - v1)
    w1 = gp * (1.0 / (1.0 + ex))
    w2 = gp * (ex / (1.0 + ex))

    hit1 = col == i1
    hit2 = col == i2
    onehot = jnp.where(hit1 | hit2, 1.0, 0.0)
    r = lax.broadcasted_iota(jnp.int32, (tm, tm), 0)
    c = lax.broadcasted_iota(jnp.int32, (tm, tm), 1)
    before = jnp.where(r > c, 1.0, 0.0).astype(BF16)
    prior = jnp.dot(before, onehot.astype(BF16), preferred_element_type=F32)
    count = jnp.sum(onehot, axis=0, keepdims=True)
    count = jnp.floor((count + (SUBLANES - 1)) * (1.0 / SUBLANES)) * SUBLANES
    wcnt_ref[0] = count
    run = jnp.broadcast_to(count, (SUBLANES, LANES))
    lane8 = lax.broadcasted_iota(jnp.int32, (SUBLANES, LANES), 1)
    shift = 1
    while shift < LANES:
        run = run + jnp.where(lane8 >= shift, pltpu.roll(run, shift, 1), 0.0)
        shift *= 2
    where_to = prior + (run[0:1, :] - count)
    pos1 = jnp.sum(jnp.where(hit1, where_to, 0.0), axis=-1, keepdims=True)
    pos2 = jnp.sum(jnp.where(hit2, where_to, 0.0), axis=-1, keepdims=True)

    fields = ((i1 - N_GROUPS).astype(F32), (i2 - N_GROUPS).astype(F32), w1, w2, pos1, pos2)
    route = jnp.zeros(logits.shape, F32)
    for k, val in enumerate(fields):
        route = jnp.where(col == k, val, route)
    route_ref[...] = route
    routet_ref[...] = jnp.transpose(route)[0:SUBLANES, :]


def _outproj_route(x2, a, c, s, w_out, g, wr, br):
    T = x2.shape[0]
    tm = PROJ_ROWS
    nt = T // tm
    rows = lambda n: pl.BlockSpec((tm, n), lambda i: (i, 0))
    const = lambda arr: pl.BlockSpec(arr.shape, lambda i: (0, 0))
    return pl.pallas_call(
        _outproj_kernel,
        grid=(nt,),
        in_specs=[rows(D_MODEL), rows(ATTN_WIDTH), rows(CONV_WIDTH), rows(SSM_WIDTH),
                  const(w_out), const(g), const(wr), const(br)],
        out_specs=[rows(D_MODEL), rows(D_MODEL), rows(LANES),
                   pl.BlockSpec((SUBLANES, tm), lambda i: (0, i)),
                   pl.BlockSpec((1, 1, LANES), lambda i: (i, 0, 0))],
        out_shape=[jax.ShapeDtypeStruct((T, D_MODEL), F32), jax.ShapeDtypeStruct((T, D_MODEL), BF16),
                   jax.ShapeDtypeStruct((T, LANES), F32), jax.ShapeDtypeStruct((SUBLANES, T), F32),
                   jax.ShapeDtypeStruct((nt, 1, LANES), F32)],
        compiler_params=_params(("arbitrary",)),
        name="outproj_route",
    )(x2, a, c, s, w_out, g, wr, br)


SLOTS = 2 * PROJ_ROWS + N_EXPERTS * SUBLANES
TILES = SLOTS // SUBLANES
assert PROJ_ROWS % SUBLANES == 0 and MOE_BLOCK % SUBLANES == 0


def _for_each_piece(n, largest, fn):
    off = jnp.int32(0)
    size = largest
    while size >= SUBLANES:
        take = (n // size) & 1

        @pl.when(take == 1)
        def _(off=off, size=size):
            fn(pl.multiple_of(off, SUBLANES), size)

        off = off + take * size
        size //= 2


def _tile_copy(buf_ref, tile, hbm_ref, hbm_tile, sem, to_hbm):
    local = buf_ref.at[pl.ds(pl.multiple_of(tile * SUBLANES, SUBLANES), SUBLANES), :]
    remote = hbm_ref.at[pl.ds(pl.multiple_of(hbm_tile * SUBLANES, SUBLANES), SUBLANES), :]
    return pltpu.make_async_copy(local, remote, sem) if to_hbm else pltpu.make_async_copy(remote, local, sem)


def _start_tiles(count, table_ref, buf_ref, hbm_ref, sem, to_hbm):
    def body(c, carry):
        _tile_copy(buf_ref, c, hbm_ref, table_ref[0, 0, c], sem, to_hbm).start()
        return carry

    lax.fori_loop(0, count, body, 0)


def _wait_tiles(count, buf_ref, hbm_ref, sem, to_hbm):
    def body(c, carry):
        _tile_copy(buf_ref, 0, hbm_ref, 0, sem, to_hbm).wait()
        return carry

    lax.fori_loop(0, count, body, 0)


def _dispatch_kernel(tot_ref, pads_ref, padn_ref, nused_ref, dst_ref, h_ref, rt_ref, xs_hbm,
                     loc_ref, zero_ref, sem, zsem, *, n_blocks):
    w = pl.program_id(0)
    last = pl.num_programs(0) - 1

    @pl.when(w == 0)
    def _():
        zero_ref[...] = jnp.zeros(zero_ref.shape, zero_ref.dtype)

        def fill(act):
            def per_expert(e, carry):
                start = pl.multiple_of(pads_ref[e], SUBLANES)

                def piece(off, size):
                    act(pltpu.make_async_copy(
                        zero_ref.at[pl.ds(0, size), :],
                        xs_hbm.at[pl.ds(pl.multiple_of(start + off, SUBLANES), size), :], zsem))
                _for_each_piece(padn_ref[e], MOE_BLOCK // 2, piece)
                return carry

            def per_block(b, carry):
                first = pl.multiple_of(b * MOE_BLOCK, MOE_BLOCK)
                act(pltpu.make_async_copy(zero_ref, xs_hbm.at[pl.ds(first, MOE_BLOCK), :], zsem))
                return carry

            lax.fori_loop(0, N_EXPERTS, per_expert, 0)
            lax.fori_loop(nused_ref[0], n_blocks, per_block, 0)

        fill(lambda cp: cp.start())
        fill(lambda cp: cp.wait())

    pos1 = rt_ref[4:5, :].astype(jnp.int32)
    pos2 = rt_ref[5:6, :].astype(jnp.int32)
    slot = lax.broadcasted_iota(jnp.int32, (SLOTS, PROJ_ROWS), 0)
    pick = jnp.where((slot == pos1) | (slot == pos2), 1.0, 0.0).astype(BF16)
    buf = loc_ref.at[w % 2]
    buf[...] = jnp.dot(pick, h_ref[...], preferred_element_type=F32)

    @pl.when(w > 0)
    def _():
        _wait_tiles(tot_ref[jnp.maximum(w - 1, 0)] // SUBLANES, buf, xs_hbm, sem, True)

    _start_tiles(tot_ref[w] // SUBLANES, dst_ref, buf, xs_hbm, sem, True)

    @pl.when(w == last)
    def _():
        _wait_tiles(tot_ref[w] // SUBLANES, buf, xs_hbm, sem, True)


def _dispatch(plan, h2, routet, n_blocks):
    T = h2.shape[0]
    tm = PROJ_ROWS
    return pl.pallas_call(
        functools.partial(_dispatch_kernel, n_blocks=n_blocks),
        grid_spec=pltpu.PrefetchScalarGridSpec(
            num_scalar_prefetch=4,
            grid=(T // tm,),
            in_specs=[pl.BlockSpec((1, 1, TILES), lambda w, *_: (w, 0, 0), memory_space=pltpu.SMEM),
                      pl.BlockSpec((tm, D_MODEL), lambda w, *_: (w, 0)),
                      pl.BlockSpec((SUBLANES, tm), lambda w, *_: (0, w))],
            out_specs=pl.BlockSpec(memory_space=pl.ANY),
            scratch_shapes=[pltpu.VMEM((2, SLOTS, D_MODEL), F32), pltpu.VMEM((MOE_BLOCK, D_MODEL), F32),
                            pltpu.SemaphoreType.DMA, pltpu.SemaphoreType.DMA],
        ),
        out_shape=jax.ShapeDtypeStruct((n_blocks * MOE_BLOCK, D_MODEL), F32),
        compiler_params=_params(("arbitrary",)),
        name="moe_dispatch",
    )(plan["tot"], plan["pad_start"], plan["pad_len"], plan["nused"], plan["dst"], h2, routet)


def _expert_kernel(blk_e_ref, nused_ref, x_ref, wg_ref, wu_ref, wd_ref, o_ref):
    b = pl.program_id(0)

    @pl.when(b < nused_ref[0])
    def _():
        x = x_ref[...].astype(BF16)
        gate = jnp.dot(x, wg_ref[0, 0].astype(BF16), preferred_element_type=F32)
        up = jnp.dot(x, wu_ref[0, 0].astype(BF16), preferred_element_type=F32)
        act = (jax.nn.silu(gate) * up).astype(BF16)
        o_ref[...] = jnp.dot(act, wd_ref[0, 0].astype(BF16), preferred_element_type=F32)

    @pl.when(b >= nused_ref[0])
    def _():
        o_ref[...] = jnp.zeros(o_ref.shape, o_ref.dtype)


def _experts(plan, xs, wg, wu, wd, layer, n_blocks):
    return pl.pallas_call(
        _expert_kernel,
        grid_spec=pltpu.PrefetchScalarGridSpec(
            num_scalar_prefetch=2,
            grid=(n_blocks,),
            in_specs=[pl.BlockSpec((MOE_BLOCK, D_MODEL), lambda b, be, nu: (b, 0)),
                      pl.BlockSpec((1, 1, D_MODEL, EXPERT_FF), lambda b, be, nu: (layer, be[b], 0, 0)),
                      pl.BlockSpec((1, 1, D_MODEL, EXPERT_FF), lambda b, be, nu: (layer, be[b], 0, 0)),
                      pl.BlockSpec((1, 1, EXPERT_FF, D_MODEL), lambda b, be, nu: (layer, be[b], 0, 0))],
            out_specs=pl.BlockSpec((MOE_BLOCK, D_MODEL), lambda b, be, nu: (b, 0)),
        ),
        out_shape=jax.ShapeDtypeStruct((n_blocks * MOE_BLOCK, D_MODEL), F32),
        compiler_params=_params(("arbitrary",)),
        name="moe_experts",
    )(plan["blk_e"], plan["nused"], xs, wg, wu, wd)


def _combine_kernel(tot_ref, src_ref, src_next_ref, yb_hbm, x_ref, route_ref, g_ref, o_ref, loc_ref, sems,
                    *, final_norm):
    w = pl.program_id(0)
    last = pl.num_programs(0) - 1

    def fetch(win, table_ref):
        buf = loc_ref.at[win % 2]
        tiles = tot_ref[win] // SUBLANES

        def clear(r, carry):
            buf[pl.ds(pl.multiple_of(r * SUBLANES, SUBLANES), SUBLANES), :] = jnp.zeros((SUBLANES, D_MODEL), F32)
            return carry

        lax.fori_loop(tiles, TILES, clear, 0)
        _start_tiles(tiles, table_ref, buf, yb_hbm, sems.at[win % 2], False)

    @pl.when(w == 0)
    def _():
        fetch(w, src_ref)

    @pl.when(w < last)
    def _():
        fetch(jnp.minimum(w + 1, last), src_next_ref)

    buf = loc_ref.at[w % 2]
    _wait_tiles(tot_ref[w] // SUBLANES, buf, yb_hbm, sems.at[w % 2], False)
    route = route_ref[...]
    pos1 = route[:, 4:5].astype(jnp.int32)
    pos2 = route[:, 5:6].astype(jnp.int32)
    slot = lax.broadcasted_iota(jnp.int32, (PROJ_ROWS, SLOTS), 1)
    sel = jnp.where(slot == pos1, route[:, 2:3], 0.0) + jnp.where(slot == pos2, route[:, 3:4], 0.0)
    sel_hi = sel.astype(BF16)
    sel_lo = (sel - sel_hi.astype(F32)).astype(BF16)
    y = buf[...]
    y_hi = y.astype(BF16)
    y_lo = (y - y_hi.astype(F32)).astype(BF16)
    moe = (jnp.dot(sel_hi, y_hi, preferred_element_type=F32)
           + jnp.dot(sel_lo, y_hi, preferred_element_type=F32)
           + jnp.dot(sel_hi, y_lo, preferred_element_type=F32))
    x = x_ref[...] + moe
    if final_norm:
        ms = jnp.mean(x * x, axis=-1, keepdims=True)
        x = x * lax.rsqrt(ms + RMS_EPS) * g_ref[...]
    o_ref[...] = x


def _combine(plan, yb, x1, route, g, final_norm):
    T = x1.shape[0]
    tm = PROJ_ROWS
    nt = T // tm
    table = lambda shift: pl.BlockSpec((1, 1, TILES), lambda w, *_: (jnp.minimum(w + shift, nt - 1), 0, 0),
                                       memory_space=pltpu.SMEM)
    return pl.pallas_call(
        functools.partial(_combine_kernel, final_norm=final_norm),
        grid_spec=pltpu.PrefetchScalarGridSpec(
            num_scalar_prefetch=1,
            grid=(nt,),
            in_specs=[table(0), table(1),
                      pl.BlockSpec(memory_space=pl.ANY),
                      pl.BlockSpec((tm, D_MODEL), lambda w, *_: (w, 0)),
                      pl.BlockSpec((tm, LANES), lambda w, *_: (w, 0)),
                      pl.BlockSpec((1, D_MODEL), lambda w, *_: (0, 0))],
            out_specs=pl.BlockSpec((tm, D_MODEL), lambda w, *_: (w, 0)),
            scratch_shapes=[pltpu.VMEM((2, SLOTS, D_MODEL), F32), pltpu.SemaphoreType.DMA((2,))],
        ),
        out_shape=jax.ShapeDtypeStruct((T, D_MODEL), F32),
        compiler_params=_params(("arbitrary",)),
        name="moe_combine",
    )(plan["tot"], plan["dst"], plan["dst"], yb, x1, route, g)


def _routing_plan(wcnt, T):
    n = wcnt[:, 0, N_GROUPS:N_GROUPS + N_EXPERTS].astype(jnp.int32)
    cnt = jnp.sum(n, axis=0)
    nblk = (cnt + MOE_BLOCK - 1) // MOE_BLOCK
    blk_end = jnp.cumsum(nblk)
    first_row = (blk_end - nblk) * MOE_BLOCK
    ls = jnp.cumsum(n, axis=1) - n
    gs = first_row[None, :] + jnp.cumsum(n, axis=0) - n
    worst_rows = T * 2 + (T // PROJ_ROWS) * N_EXPERTS * SUBLANES
    n_blocks = -(-worst_rows // MOE_BLOCK) + N_EXPERTS
    blk_e = jnp.minimum(jnp.sum(jnp.arange(n_blocks)[:, None] >= blk_end[None, :], axis=1), N_EXPERTS - 1)
    tile = jnp.arange(TILES)[None, :, None]
    lt, nt8, gt = (a[:, None, :] // SUBLANES for a in (ls, n, gs))
    dst = jnp.sum(jnp.where((tile >= lt) & (tile < lt + nt8), gt + tile - lt, 0), axis=2)
    i32 = lambda a: a.reshape(-1).astype(jnp.int32)
    plan = dict(tot=i32(jnp.sum(n, axis=1)), pad_start=i32(first_row + cnt), pad_len=i32(nblk * MOE_BLOCK - cnt),
                nused=i32(blk_end[-1:]), blk_e=i32(blk_e),
                dst=dst.astype(jnp.int32).reshape(n.shape[0], 1, TILES))
    return plan, n_blocks


def _pack_w_in(w):
    scale = QK_DIM ** -0.5 * LOG2E
    q1, q2, k1, k2 = (w[:, i * QK_COLS:(i + 1) * QK_COLS].reshape(D_MODEL, N_HEADS, QK_DIM) for i in range(4))
    qq = (jnp.concatenate([q1, q2], axis=-1) * scale).reshape(D_MODEL, ATTN_WIDTH)
    kk = jnp.concatenate([k1, k2], axis=-1).reshape(D_MODEL, ATTN_WIDTH)
    v0 = 4 * QK_COLS
    packed = jnp.concatenate([qq, kk, w[:, v0 + ATTN_WIDTH:]], axis=1).astype(BF16)
    return packed, jnp.transpose(w[:, v0:v0 + ATTN_WIDTH]).astype(BF16)


def kernel(x, rel_bias, ln1_g, w_in, lam_q1, lam_k1, lam_q2, lam_k2, subln_g, conv_w, conv_b, conv_ln_g, conv_ln_b, ssm_lam_re, ssm_lam_im, ssm_log_dt, ssm_b_re, ssm_b_im, ssm_c_re, ssm_c_im, ssm_d, ssm_glu_w, ssm_glu_b, w_out, ln2_g, group_router_w, group_router_b, expert_router_w, expert_router_b, w_gate, w_up, w_down, final_g):
    B, L, D = x.shape
    T = B * L
    depth = w_in.shape[0]
    assert D == D_MODEL and L % CONV_ROWS == 0 and L % ATTN_TILE == 0 and L % SSM_ROWS == 0
    assert PROJ_ROWS == ATTN_TILE and T % PROJ_ROWS == 0 and (2 * T) % MOE_BLOCK == 0
    x2 = x.reshape(T, D)
    bias_diag, bias_sub = _bias_tiles(rel_bias)
    row = lambda v: v.astype(F32).reshape(1, -1)
    for l in range(depth):
        lam_init = 0.8 - 0.6 * math.exp(-0.3 * l)
        lam = (jnp.exp(jnp.sum(lam_q1[l].astype(F32) * lam_k1[l].astype(F32)))
               - jnp.exp(jnp.sum(lam_q2[l].astype(F32) * lam_k2[l].astype(F32))) + lam_init).reshape(1)
        qq, kk, conv_in, ssm_in, vt = _inproj(x2, row(ln1_g[l]), *_pack_w_in(w_in[l]))
        a = _attention(qq, kk, vt, lam, bias_diag, bias_sub, subln_g[l].astype(F32).reshape(V_DIM, 1),
                       1.0 - lam_init, B, L)
        c = _conformer_conv(conv_in, conv_w[l], row(conv_b[l]), row(conv_ln_g[l]), row(conv_ln_b[l]), B, L)
        wb, apr, api, wc = _ssm_weights(ssm_lam_re[l], ssm_lam_im[l], ssm_log_dt[l], ssm_b_re[l], ssm_b_im[l],
                                        ssm_c_re[l], ssm_c_im[l])
        s = _s5_ssm(ssm_in, wb, apr, api, wc, row(ssm_d[l]), ssm_glu_w[l].astype(BF16), row(ssm_glu_b[l]), B, L)
        wr = jnp.zeros((D, LANES), F32).at[:, :N_GROUPS].set(group_router_w[l]) \
            .at[:, N_GROUPS:N_GROUPS + N_EXPERTS].set(expert_router_w[l]).astype(BF16)
        br = jnp.zeros((1, LANES), F32).at[0, :N_GROUPS].set(group_router_b[l]) \
            .at[0, N_GROUPS:N_GROUPS + N_EXPERTS].set(expert_router_b[l])
        x1, h2, route, routet, wcnt = _outproj_route(x2, a, c, s, w_out[l].astype(BF16), row(ln2_g[l]), wr, br)
        plan, n_blocks = _routing_plan(wcnt, T)
        xs = _dispatch(plan, h2, routet, n_blocks)
        yb = _experts(plan, xs, w_gate, w_up, w_down, l, n_blocks)
        x2 = _combine(plan, yb, x1, route, row(final_g), final_norm=(l == depth - 1))
    return x2.reshape(B, L, D)
```

```python
import functools
import math

import jax
import jax.numpy as jnp
from jax import lax
from jax.experimental import pallas as pl
from jax.experimental.pallas import tpu as pltpu

F32 = jnp.float32
BF16 = jnp.bfloat16

D_MODEL = 1024
N_HEADS = 4
QK_DIM = 64
V_DIM = 128
ATTN_WIDTH = N_HEADS * V_DIM
QK_COLS = N_HEADS * QK_DIM
CONV_WIDTH = 256
CONV_TAPS = 31
SSM_WIDTH = 256
SSM_GROUP = 16
SSM_GROUPS = 16
SSM_STATE = 64
SSM_LANES = SSM_GROUPS * SSM_STATE
REL_BUCKETS = 32
REL_MAX_EXACT = 16
REL_MAX_DIST = 128
N_GROUPS = 4
EXPERTS_PER_GROUP = 8
N_EXPERTS = N_GROUPS * EXPERTS_PER_GROUP
EXPERT_FF = 512
RMS_EPS = 1e-6
LN_EPS = 1e-5
NEG_BIG = -1e30

LANES = 128
SUBLANES = 8
VMEM_LIMIT = 48 * 1024 * 1024

PROJ_ROWS = 512
ATTN_TILE = 512
ATTN_CHUNK = 32
ONES_ROWS = 16
ATTN_HEADS_PER_STEP = 2
LOG2E = math.log2(math.e)
CONV_ROWS = 512
CONV_HALO = 32
SSM_ROWS = 256
SSM_UNROLL = 8
SSM_SHIFTS = (1, 2, 4)
MOE_BLOCK = 512


def _params(sem):
    return pltpu.CompilerParams(dimension_semantics=sem, vmem_limit_bytes=VMEM_LIMIT)


def _inproj_kernel(x_ref, g_ref, w_ref, wvt_ref, qq_ref, kk_ref, conv_ref, ssm_ref, vt_ref):
    x = x_ref[...]
    ms = jnp.mean(x * x, axis=-1, keepdims=True)
    h = (x * lax.rsqrt(ms + RMS_EPS) * g_ref[...]).astype(BF16)
    o = 0
    for ref in (qq_ref, kk_ref, conv_ref, ssm_ref):
        n = ref.shape[-1]
        ref[...] = jnp.dot(h, w_ref[:, o:o + n], preferred_element_type=F32).astype(ref.dtype)
        o += n
    vt_ref[0] = lax.dot_general(wvt_ref[...], h, (((1,), (1,)), ((), ())),
                                preferred_element_type=F32).astype(vt_ref.dtype)


def _inproj(x2, g, w, wvt):
    T = x2.shape[0]
    tm = PROJ_ROWS
    widths = (ATTN_WIDTH, ATTN_WIDTH, 2 * CONV_WIDTH, SSM_WIDTH)
    dtypes = (BF16, BF16, F32, F32)
    return pl.pallas_call(
        _inproj_kernel,
        grid=(T // tm,),
        in_specs=[
            pl.BlockSpec((tm, D_MODEL), lambda i: (i, 0)),
            pl.BlockSpec((1, D_MODEL), lambda i: (0, 0)),
            pl.BlockSpec(w.shape, lambda i: (0, 0)),
            pl.BlockSpec(wvt.shape, lambda i: (0, 0)),
        ],
        out_specs=[pl.BlockSpec((tm, n), lambda i: (i, 0)) for n in widths]
        + [pl.BlockSpec((1, ATTN_WIDTH, tm), lambda i: (i, 0, 0))],
        out_shape=[jax.ShapeDtypeStruct((T, n), dt) for n, dt in zip(widths, dtypes)]
        + [jax.ShapeDtypeStruct((T // tm, ATTN_WIDTH, tm), BF16)],
        compiler_params=_params(("arbitrary",)),
        name="inproj",
    )(x2, g, w, wvt)


def _attn_kernel(lam_ref, q_ref, k_ref, vt_ref, bd_ref, bs_ref, g_ref, o_ref, *scratch, out_scale):
    t = ATTN_TILE
    nh = ATTN_HEADS_PER_STEP
    qi = pl.program_id(2)
    per_head = len(scratch) // nh
    heads = [scratch[h * per_head:(h + 1) * per_head] for h in range(nh)]
    cols = lambda h: slice(h * LANES, (h + 1) * LANES)

    for h, (qs_ref, m_ref, acc_ref, sa_ref, sb_ref, pa_ref, pb_ref, aa_ref, ab_ref) in enumerate(heads):
        q = q_ref[:, cols(h)].astype(F32)
        lane = lax.broadcasted_iota(jnp.int32, q.shape, 1)
        qs_ref[0:t, :] = jnp.where(lane < QK_DIM, q, 0.0).astype(BF16)
        qs_ref[t:2 * t, :] = jnp.where(lane >= QK_DIM, q, 0.0).astype(BF16)
        m_ref[...] = jnp.full(m_ref.shape, NEG_BIG, F32)
        acc_ref[...] = jnp.zeros(acc_ref.shape, F32)
        pb_ref[...] = jnp.zeros(pb_ref.shape, BF16)
        ab_ref[...] = jnp.ones(ab_ref.shape, F32)

    chunks = [(c, c + ATTN_CHUNK) for c in range(0, t, ATTN_CHUNK)]
    fold = lambda a: a.reshape(ATTN_CHUNK // SUBLANES, SUBLANES, 2 * t)

    def scores(j, which):
        for h, refs in enumerate(heads):
            k = k_ref[pl.ds(pl.multiple_of(j * t, t), t), cols(h)]
            refs[3 + which][...] = lax.dot_general(k, refs[0][...], (((1,), (1,)), ((), ())),
                                                   preferred_element_type=F32)

    def softmax(which, bias_ref=None):
        for h, refs in enumerate(heads):
            m_ref, s_ref, p_ref, a_ref = refs[1], refs[3 + which], refs[5 + which], refs[7 + which]
            top = jnp.full((SUBLANES, 2 * t), NEG_BIG, F32)
            for lo, hi in chunks:
                s = s_ref[lo:hi, :]
                if bias_ref is not None:
                    b = bias_ref[h, lo:hi, :]
                    s = s + jnp.concatenate([b, b], axis=1)
                    s_ref[lo:hi, :] = s
                top = jnp.maximum(top, jnp.max(fold(s), axis=0))
            m_prev = m_ref[...]
            m_new = jnp.maximum(m_prev, jnp.max(top, axis=0, keepdims=True))
            a_ref[...] = jnp.exp2(m_prev - m_new)
            for lo, hi in chunks:
                p_ref[lo:hi, :] = jnp.exp2(s_ref[lo:hi, :] - m_new).astype(BF16)
            m_ref[...] = m_new

    def values(j, which):
        for h, refs in enumerate(heads):
            acc_ref, p_ref, a_ref = refs[2], refs[5 + which], refs[7 + which]
            lhs = jnp.concatenate([vt_ref[jnp.maximum(j, 0), cols(h), :], jnp.ones((ONES_ROWS, t), BF16)], axis=0)
            acc_ref[...] = a_ref[...] * acc_ref[...] + jnp.dot(lhs, p_ref[...], preferred_element_type=F32)

    SET_A, SET_B = 0, 1
    nfar = jnp.maximum(qi - 1, 0)
    scores(0, SET_A)

    def far_pair(i, carry):
        k = 2 * i
        scores(k + 1, SET_B)
        softmax(SET_A)
        values(k - 1, SET_B)
        scores(k + 2, SET_A)
        softmax(SET_B)
        values(k, SET_A)
        return carry

    lax.fori_loop(0, nfar // 2, far_pair, 0)

    @pl.when(qi == 0)
    def _():
        softmax(SET_A, bd_ref)
        values(qi, SET_A)

    @pl.when((qi >= 1) & (nfar % 2 == 0))
    def _():
        scores(qi, SET_B)
        softmax(SET_A, bs_ref)
        values(qi - 2, SET_B)
        softmax(SET_B, bd_ref)
        values(qi - 1, SET_A)
        values(qi, SET_B)

    @pl.when(nfar % 2 == 1)
    def _():
        scores(qi - 1, SET_B)
        softmax(SET_A)
        values(qi - 3, SET_B)
        scores(qi, SET_A)
        softmax(SET_B, bs_ref)
        values(qi - 2, SET_A)
        softmax(SET_A, bd_ref)
        values(qi - 1, SET_B)
        values(qi, SET_A)

    for h, refs in enumerate(heads):
        acc_ref = refs[2]
        acc = acc_ref[0:V_DIM, :]
        l = acc_ref[V_DIM:V_DIM + 1, :]
        a = acc[:, 0:t] / l[:, 0:t] - lam_ref[0] * (acc[:, t:2 * t] / l[:, t:2 * t])
        ms = jnp.mean(a * a, axis=0, keepdims=True)
        y = a * lax.rsqrt(ms + RMS_EPS) * g_ref[...] * out_scale
        o_ref[:, cols(h)] = jnp.transpose(y).astype(o_ref.dtype)


def _attention(qq, kk, vt, lam, bias_diag, bias_sub, subln_g, out_scale, B, L):
    T = B * L
    t = ATTN_TILE
    nh = ATTN_HEADS_PER_STEP
    nq = L // t
    once = pl.Buffered(1)
    head_scratch = [
        pltpu.VMEM((2 * t, LANES), BF16),
        pltpu.VMEM((1, 2 * t), F32),
        pltpu.VMEM((V_DIM + ONES_ROWS, 2 * t), F32),
        pltpu.VMEM((t, 2 * t), F32),
        pltpu.VMEM((t, 2 * t), F32),
        pltpu.VMEM((t, 2 * t), BF16),
        pltpu.VMEM((t, 2 * t), BF16),
        pltpu.VMEM((1, 2 * t), F32),
        pltpu.VMEM((1, 2 * t), F32),
    ]
    return pl.pallas_call(
        functools.partial(_attn_kernel, out_scale=out_scale),
        grid=(B, N_HEADS // nh, nq),
        in_specs=[
            pl.BlockSpec(memory_space=pltpu.SMEM),
            pl.BlockSpec((t, nh * LANES), lambda b, h, i: (b * nq + i, h)),
            pl.BlockSpec((L, nh * LANES), lambda b, h, i: (b, h), pipeline_mode=once),
            pl.BlockSpec((nq, nh * V_DIM, t), lambda b, h, i: (b, h, 0), pipeline_mode=once),
            pl.BlockSpec((nh, t, t), lambda b, h, i: (h, 0, 0), pipeline_mode=once),
            pl.BlockSpec((nh, t, t), lambda b, h, i: (h, 0, 0), pipeline_mode=once),
            pl.BlockSpec((V_DIM, 1), lambda b, h, i: (0, 0)),
        ],
        out_specs=pl.BlockSpec((t, nh * LANES), lambda b, h, i: (b * nq + i, h)),
        out_shape=jax.ShapeDtypeStruct((T, ATTN_WIDTH), BF16),
        scratch_shapes=head_scratch * nh,
        compiler_params=_params(("arbitrary", "arbitrary", "arbitrary")),
        name="diff_attn",
    )(lam, qq, kk, vt, bias_diag, bias_sub, subln_g)


def _rel_bucket(rel):
    n = jnp.maximum(rel, 0)
    nf = jnp.maximum(n, 1).astype(F32)
    large = REL_MAX_EXACT + (jnp.log(nf / REL_MAX_EXACT) / math.log(REL_MAX_DIST / REL_MAX_EXACT)
                             * (REL_BUCKETS - REL_MAX_EXACT)).astype(jnp.int32)
    large = jnp.minimum(large, REL_BUCKETS - 1)
    return jnp.where(n < REL_MAX_EXACT, n, large)


def _bias_tiles(rel_table):
    t = ATTN_TILE
    assert t >= REL_MAX_DIST
    far = rel_table[REL_BUCKETS - 1].astype(F32)
    rel_d = jnp.arange(t)[None, :] - jnp.arange(t)[:, None]

    def lookup(bucket):
        out = jnp.zeros((N_HEADS,) + bucket.shape, F32)
        for b in range(REL_BUCKETS):
            out = jnp.where((bucket == b)[None], (rel_table[b].astype(F32) - far)[:, None, None], out)
        return out * LOG2E

    bd = jnp.where((rel_d >= 0)[None], lookup(_rel_bucket(rel_d)), NEG_BIG)
    return bd, lookup(_rel_bucket(rel_d + t))


def _conv_kernel(u_ref, w_ref, b_ref, g_ref, beta_ref, o_ref, h_ref):
    tt = CONV_ROWS
    j = pl.program_id(1)

    @pl.when(j == 0)
    def _():
        h_ref[0:CONV_HALO, :] = jnp.zeros((CONV_HALO, CONV_WIDTH), F32)

    @pl.when(j > 0)
    def _():
        h_ref[0:CONV_HALO, :] = h_ref[tt:tt + CONV_HALO, :]

    u = u_ref[...]
    h_ref[CONV_HALO:CONV_HALO + tt, :] = u[:, 0:CONV_WIDTH] * jax.nn.sigmoid(u[:, CONV_WIDTH:])
    acc = jnp.broadcast_to(b_ref[...], (tt, CONV_WIDTH))
    off = CONV_HALO - (CONV_TAPS - 1)
    for k in range(CONV_TAPS):
        acc = acc + w_ref[k:k + 1, :] * h_ref[off + k:off + k + tt, :]
    mu = jnp.mean(acc, axis=-1, keepdims=True)
    cen = acc - mu
    var = jnp.mean(cen * cen, axis=-1, keepdims=True)
    y = cen * lax.rsqrt(var + LN_EPS) * g_ref[...] + beta_ref[...]
    o_ref[...] = jax.nn.silu(y).astype(o_ref.dtype)


def _conformer_conv(conv_in, w, b, g, beta, B, L):
    T = B * L
    tt = CONV_ROWS
    nt = L // tt
    vec = pl.BlockSpec((1, CONV_WIDTH), lambda bb, j: (0, 0))
    return pl.pallas_call(
        _conv_kernel,
        grid=(B, nt),
        in_specs=[
            pl.BlockSpec((tt, 2 * CONV_WIDTH), lambda bb, j: (bb * nt + j, 0)),
            pl.BlockSpec((CONV_TAPS, CONV_WIDTH), lambda bb, j: (0, 0)),
            vec, vec, vec,
        ],
        out_specs=pl.BlockSpec((tt, CONV_WIDTH), lambda bb, j: (bb * nt + j, 0)),
        out_shape=jax.ShapeDtypeStruct((T, CONV_WIDTH), BF16),
        scratch_shapes=[pltpu.VMEM((tt + CONV_HALO, CONV_WIDTH), F32)],
        compiler_params=_params(("arbitrary", "arbitrary")),
        name="conformer_conv",
    )(conv_in, w, b, g, beta)


def _ssm_kernel(u_ref, wb_ref, apr_ref, api_ref, wc_ref, d_ref, gw_ref, gb_ref, o_ref,
                xs_ref, carry_ref):
    tt = SSM_ROWS
    n = SSM_LANES
    j = pl.program_id(1)

    @pl.when(j == 0)
    def _():
        carry_ref[...] = jnp.zeros(carry_ref.shape, F32)

    u = u_ref[...]
    xs_ref[...] = jnp.dot(u.astype(BF16), wb_ref[...], preferred_element_type=F32)
    apr = apr_ref[0:SUBLANES, :]
    api = api_ref[0:SUBLANES, :]

    def block(r, carry):
        cr, ci = carry
        start = pl.multiple_of(r * SUBLANES, SUBLANES)
        xr = xs_ref[pl.ds(start, SUBLANES), 0:n]
        xi = xs_ref[pl.ds(start, SUBLANES), n:2 * n]
        for k, shift in enumerate(SSM_SHIFTS):
            ar = apr_ref[(k + 1) * SUBLANES:(k + 2) * SUBLANES, :]
            ai = api_ref[(k + 1) * SUBLANES:(k + 2) * SUBLANES, :]
            sr = pltpu.roll(xr, shift, 0)
            si = pltpu.roll(xi, shift, 0)
            xr, xi = xr + (ar * sr - ai * si), xi + (ar * si + ai * sr)
        xr, xi = xr + (apr * cr - api * ci), xi + (apr * ci + api * cr)
        xs_ref[pl.ds(start, SUBLANES), 0:n] = xr
        xs_ref[pl.ds(start, SUBLANES), n:2 * n] = xi
        return xr[SUBLANES - 1:SUBLANES, :], xi[SUBLANES - 1:SUBLANES, :]

    cr, ci = lax.fori_loop(0, tt // SUBLANES, block, (carry_ref[0:1, :], carry_ref[1:2, :]),
                           unroll=SSM_UNROLL)
    carry_ref[0:1, :] = cr
    carry_ref[1:2, :] = ci

    y = jnp.dot(xs_ref[...].astype(BF16), wc_ref[...], preferred_element_type=F32) + u * d_ref[...]
    g = jax.nn.gelu(y)
    z = jnp.dot(g.astype(BF16), gw_ref[...], preferred_element_type=F32) + gb_ref[...]
    o_ref[...] = (g * jax.nn.sigmoid(z)).astype(o_ref.dtype)


def _s5_ssm(ssm_in, wb, apr, api, wc, d, gw, gb, B, L):
    T = B * L
    tt = SSM_ROWS
    nt = L // tt
    const = lambda a: pl.BlockSpec(a.shape, lambda bb, j: (0, 0))
    return pl.pallas_call(
        _ssm_kernel,
        grid=(B, nt),
        in_specs=[pl.BlockSpec((tt, SSM_WIDTH), lambda bb, j: (bb * nt + j, 0)),
                  const(wb), const(apr), const(api), const(wc), const(d), const(gw), const(gb)],
        out_specs=pl.BlockSpec((tt, SSM_WIDTH), lambda bb, j: (bb * nt + j, 0)),
        out_shape=jax.ShapeDtypeStruct((T, SSM_WIDTH), BF16),
        scratch_shapes=[pltpu.VMEM((tt, 2 * SSM_LANES), F32), pltpu.VMEM((SUBLANES, SSM_LANES), F32)],
        compiler_params=_params(("arbitrary", "arbitrary")),
        name="s5_scan",
    )(ssm_in, wb, apr, api, wc, d, gw, gb)


def _ssm_weights(lam_re, lam_im, log_dt, b_re, b_im, c_re, c_im):
    G, P, H = SSM_GROUPS, SSM_STATE, SSM_GROUP
    dt = jnp.exp(log_dt.astype(F32))[:, None]
    lr, li = lam_re.astype(F32), lam_im.astype(F32)
    mag = jnp.exp(lr * dt)
    ar, ai = mag * jnp.cos(li * dt), mag * jnp.sin(li * dt)
    den = lr * lr + li * li
    zr = ((ar - 1.0) * lr + ai * li) / den
    zi = (ai * lr - (ar - 1.0) * li) / den
    bre, bim = b_re.astype(F32), b_im.astype(F32)
    bbr = zr[..., None] * bre - zi[..., None] * bim
    bbi = zr[..., None] * bim + zi[..., None] * bre
    eye = jnp.eye(G, dtype=F32)
    wb = jnp.concatenate([jnp.einsum('gph,gk->ghkp', bbr, eye).reshape(G * H, G * P),
                          jnp.einsum('gph,gk->ghkp', bbi, eye).reshape(G * H, G * P)], axis=1)
    wc = jnp.concatenate([jnp.einsum('ghp,gk->gpkh', c_re.astype(F32), eye).reshape(G * P, G * H),
                          -jnp.einsum('ghp,gk->gpkh', c_im.astype(F32), eye).reshape(G * P, G * H)], axis=0)
    pr, pi = [ar.reshape(1, G * P)], [ai.reshape(1, G * P)]
    for _ in range(SUBLANES - 1):
        pr, pi = (pr + [pr[-1] * pr[0] - pi[-1] * pi[0]], pi + [pr[-1] * pi[0] + pi[-1] * pr[0]])
    rows = jnp.arange(SUBLANES)[:, None]
    tr, ti = list(pr), list(pi)
    for shift in SSM_SHIFTS:
        tr.append(jnp.where(rows >= shift, pr[shift - 1], 0.0))
        ti.append(jnp.where(rows >= shift, pi[shift - 1], 0.0))
    return wb.astype(BF16), jnp.concatenate(tr, axis=0), jnp.concatenate(ti, axis=0), wc.astype(BF16)


def _outproj_kernel(x_ref, a_ref, c_ref, s_ref, w_ref, g_ref, wr_ref, br_ref,
                    x1_ref, h2_ref, route_ref, routet_ref, wcnt_ref):
    tm = PROJ_ROWS
    o1, o2 = ATTN_WIDTH, ATTN_WIDTH + CONV_WIDTH
    y = (jnp.dot(a_ref[...], w_ref[0:o1, :], preferred_element_type=F32)
         + jnp.dot(c_ref[...], w_ref[o1:o2, :], preferred_element_type=F32)
         + jnp.dot(s_ref[...], w_ref[o2:, :], preferred_element_type=F32))
    x1 = x_ref[...] + y
    x1_ref[...] = x1
    ms = jnp.mean(x1 * x1, axis=-1, keepdims=True)
    h2 = (x1 * lax.rsqrt(ms + RMS_EPS) * g_ref[...]).astype(BF16)
    h2_ref[...] = h2

    logits = jnp.dot(h2, wr_ref[...], preferred_element_type=F32) + br_ref[...]
    col = lax.broadcasted_iota(jnp.int32, logits.shape, 1)

    def first_max(vals):
        top = jnp.max(vals, axis=-1, keepdims=True)
        return top, jnp.min(jnp.where(vals == top, col, LANES), axis=-1, keepdims=True)

    glog = jnp.where(col < N_GROUPS, logits, NEG_BIG)
    gmax, gidx = first_max(glog)
    gp = 1.0 / jnp.sum(jnp.exp(glog - gmax), axis=-1, keepdims=True)
    lo = N_GROUPS + gidx * EXPERTS_PER_GROUP
    e = jnp.where((col >= lo) & (col < lo + EXPERTS_PER_GROUP), logits, NEG_BIG)
    v1, i1 = first_max(e)
    e = jnp.where(col == i1, NEG_BIG, e)
    v2, i2 = first_max(e)
    ex = jnp.exp(v2 - v1)
    w1 = gp * (1.0 / (1.0 + ex))
    w2 = gp * (ex / (1.0 + ex))

    hit1 = col == i1
    hit2 = col == i2
    onehot = jnp.where(hit1 | hit2, 1.0, 0.0)
    r = lax.broadcasted_iota(jnp.int32, (tm, tm), 0)
    c = lax.broadcasted_iota(jnp.int32, (tm, tm), 1)
    before = jnp.where(r > c, 1.0, 0.0).astype(BF16)
    prior = jnp.dot(before, onehot.astype(BF16), preferred_element_type=F32)
    count = jnp.sum(onehot, axis=0, keepdims=True)
    count = jnp.floor((count + (SUBLANES - 1)) * (1.0 / SUBLANES)) * SUBLANES
    wcnt_ref[0] = count
    run = jnp.broadcast_to(count, (SUBLANES, LANES))
    lane8 = lax.broadcasted_iota(jnp.int32, (SUBLANES, LANES), 1)
    shift = 1
    while shift < LANES:
        run = run + jnp.where(lane8 >= shift, pltpu.roll(run, shift, 1), 0.0)
        shift *= 2
    where_to = prior + (run[0:1, :] - count)
    pos1 = jnp.sum(jnp.where(hit1, where_to, 0.0), axis=-1, keepdims=True)
    pos2 = jnp.sum(jnp.where(hit2, where_to, 0.0), axis=-1, keepdims=True)

    fields = ((i1 - N_GROUPS).astype(F32), (i2 - N_GROUPS).astype(F32), w1, w2, pos1, pos2)
    route = jnp.zeros(logits.shape, F32)
    for k, val in enumerate(fields):
        route = jnp.where(col == k, val, route)
    route_ref[...] = route
    routet_ref[...] = jnp.transpose(route)[0:SUBLANES, :]


def _outproj_route(x2, a, c, s, w_out, g, wr, br):
    T = x2.shape[0]
    tm = PROJ_ROWS
    nt = T // tm
    rows = lambda n: pl.BlockSpec((tm, n), lambda i: (i, 0))
    const = lambda arr: pl.BlockSpec(arr.shape, lambda i: (0, 0))
    return pl.pallas_call(
        _outproj_kernel,
        grid=(nt,),
        in_specs=[rows(D_MODEL), rows(ATTN_WIDTH), rows(CONV_WIDTH), rows(SSM_WIDTH),
                  const(w_out), const(g), const(wr), const(br)],
        out_specs=[rows(D_MODEL), rows(D_MODEL), rows(LANES),
                   pl.BlockSpec((SUBLANES, tm), lambda i: (0, i)),
                   pl.BlockSpec((1, 1, LANES), lambda i: (i, 0, 0))],
        out_shape=[jax.ShapeDtypeStruct((T, D_MODEL), F32), jax.ShapeDtypeStruct((T, D_MODEL), BF16),
                   jax.ShapeDtypeStruct((T, LANES), F32), jax.ShapeDtypeStruct((SUBLANES, T), F32),
                   jax.ShapeDtypeStruct((nt, 1, LANES), F32)],
        compiler_params=_params(("arbitrary",)),
        name="outproj_route",
    )(x2, a, c, s, w_out, g, wr, br)


SLOTS = 2 * PROJ_ROWS + N_EXPERTS * SUBLANES
TILES = SLOTS // SUBLANES
XS_WORDS = D_MODEL // 2
assert PROJ_ROWS % SUBLANES == 0 and MOE_BLOCK % SUBLANES == 0


def _for_each_piece(n, largest, fn):
    off = jnp.int32(0)
    size = largest
    while size >= SUBLANES:
        take = (n // size) & 1

        @pl.when(take == 1)
        def _(off=off, size=size):
            fn(pl.multiple_of(off, SUBLANES), size)

        off = off + take * size
        size //= 2


def _tile_copy(buf_ref, tile, hbm_ref, hbm_tile, sem, to_hbm):
    local = buf_ref.at[pl.ds(pl.multiple_of(tile * SUBLANES, SUBLANES), SUBLANES), :]
    remote = hbm_ref.at[pl.ds(pl.multiple_of(hbm_tile * SUBLANES, SUBLANES), SUBLANES), :]
    return pltpu.make_async_copy(local, remote, sem) if to_hbm else pltpu.make_async_copy(remote, local, sem)


def _start_tiles(count, table_ref, buf_ref, hbm_ref, sem, to_hbm):
    def body(c, carry):
        _tile_copy(buf_ref, c, hbm_ref, table_ref[0, 0, c], sem, to_hbm).start()
        return carry

    lax.fori_loop(0, count, body, 0)


def _wait_tiles(count, buf_ref, hbm_ref, sem, to_hbm):
    def body(c, carry):
        _tile_copy(buf_ref, 0, hbm_ref, 0, sem, to_hbm).wait()
        return carry

    lax.fori_loop(0, count, body, 0)


def _dispatch_kernel(tot_ref, pads_ref, padn_ref, nused_ref, dst_ref, h_ref, rt_ref, xs_hbm,
                     loc_ref, zero_ref, sem, zsem, *, n_blocks):
    w = pl.program_id(0)
    last = pl.num_programs(0) - 1

    @pl.when(w == 0)
    def _():
        zero_ref[...] = jnp.zeros(zero_ref.shape, zero_ref.dtype)

        def fill(act):
            def per_expert(e, carry):
                start = pl.multiple_of(pads_ref[e], SUBLANES)

                def piece(off, size):
                    act(pltpu.make_async_copy(
                        zero_ref.at[pl.ds(0, size), :],
                        xs_hbm.at[pl.ds(pl.multiple_of(start + off, SUBLANES), size), :], zsem))
                _for_each_piece(padn_ref[e], MOE_BLOCK // 2, piece)
                return carry

            def per_block(b, carry):
                first = pl.multiple_of(b * MOE_BLOCK, MOE_BLOCK)
                act(pltpu.make_async_copy(zero_ref, xs_hbm.at[pl.ds(first, MOE_BLOCK), :], zsem))
                return carry

            lax.fori_loop(0, N_EXPERTS, per_expert, 0)
            lax.fori_loop(nused_ref[0], n_blocks, per_block, 0)

        fill(lambda cp: cp.start())
        fill(lambda cp: cp.wait())

    pos1 = rt_ref[4:5, :].astype(jnp.int32)
    pos2 = rt_ref[5:6, :].astype(jnp.int32)
    slot = lax.broadcasted_iota(jnp.int32, (SLOTS, PROJ_ROWS), 0)
    pick = jnp.where((slot == pos1) | (slot == pos2), 1.0, 0.0).astype(BF16)
    buf = loc_ref.at[w % 2]
    rows = pltpu.bitcast(jnp.dot(pick, h_ref[...], preferred_element_type=F32), jnp.uint32)
    buf[...] = rows[:, 0:XS_WORDS] | (rows[:, XS_WORDS:D_MODEL] >> 16)

    @pl.when(w > 0)
    def _():
        _wait_tiles(tot_ref[jnp.maximum(w - 1, 0)] // SUBLANES, buf, xs_hbm, sem, True)

    _start_tiles(tot_ref[w] // SUBLANES, dst_ref, buf, xs_hbm, sem, True)

    @pl.when(w == last)
    def _():
        _wait_tiles(tot_ref[w] // SUBLANES, buf, xs_hbm, sem, True)


def _dispatch(plan, h2, routet, n_blocks):
    T = h2.shape[0]
    tm = PROJ_ROWS
    return pl.pallas_call(
        functools.partial(_dispatch_kernel, n_blocks=n_blocks),
        grid_spec=pltpu.PrefetchScalarGridSpec(
            num_scalar_prefetch=4,
            grid=(T // tm,),
            in_specs=[pl.BlockSpec((1, 1, TILES), lambda w, *_: (w, 0, 0), memory_space=pltpu.SMEM),
                      pl.BlockSpec((tm, D_MODEL), lambda w, *_: (w, 0)),
                      pl.BlockSpec((SUBLANES, tm), lambda w, *_: (0, w))],
            out_specs=pl.BlockSpec(memory_space=pl.ANY),
            scratch_shapes=[pltpu.VMEM((2, SLOTS, XS_WORDS), jnp.uint32),
                            pltpu.VMEM((MOE_BLOCK, XS_WORDS), jnp.uint32),
                            pltpu.SemaphoreType.DMA, pltpu.SemaphoreType.DMA],
        ),
        out_shape=jax.ShapeDtypeStruct((n_blocks * MOE_BLOCK, XS_WORDS), jnp.uint32),
        compiler_params=_params(("arbitrary",)),
        name="moe_dispatch",
    )(plan["tot"], plan["pad_start"], plan["pad_len"], plan["nused"], plan["dst"], h2, routet)


def _expert_kernel(blk_e_ref, nused_ref, x_ref, wg_ref, wu_ref, wd_ref, o_ref):
    b = pl.program_id(0)

    @pl.when(b < nused_ref[0])
    def _():
        words = x_ref[...]
        high = pltpu.bitcast(words & jnp.uint32(0xFFFF0000), F32).astype(BF16)
        low = pltpu.bitcast(words << 16, F32).astype(BF16)
        x = jnp.concatenate([high, low], axis=1)
        gate = jnp.dot(x, wg_ref[0, 0].astype(BF16), preferred_element_type=F32)
        up = jnp.dot(x, wu_ref[0, 0].astype(BF16), preferred_element_type=F32)
        act = (jax.nn.silu(gate) * up).astype(BF16)
        o_ref[...] = jnp.dot(act, wd_ref[0, 0].astype(BF16), preferred_element_type=F32)

    @pl.when(b >= nused_ref[0])
    def _():
        o_ref[...] = jnp.zeros(o_ref.shape, o_ref.dtype)


def _experts(plan, xs, wg, wu, wd, layer, n_blocks):
    return pl.pallas_call(
        _expert_kernel,
        grid_spec=pltpu.PrefetchScalarGridSpec(
            num_scalar_prefetch=2,
            grid=(n_blocks,),
            in_specs=[pl.BlockSpec((MOE_BLOCK, XS_WORDS), lambda b, be, nu: (b, 0)),
                      pl.BlockSpec((1, 1, D_MODEL, EXPERT_FF), lambda b, be, nu: (layer, be[b], 0, 0)),
                      pl.BlockSpec((1, 1, D_MODEL, EXPERT_FF), lambda b, be, nu: (layer, be[b], 0, 0)),
                      pl.BlockSpec((1, 1, EXPERT_FF, D_MODEL), lambda b, be, nu: (layer, be[b], 0, 0))],
            out_specs=pl.BlockSpec((MOE_BLOCK, D_MODEL), lambda b, be, nu: (b, 0)),
        ),
        out_shape=jax.ShapeDtypeStruct((n_blocks * MOE_BLOCK, D_MODEL), F32),
        compiler_params=_params(("arbitrary",)),
        name="moe_experts",
    )(plan["blk_e"], plan["nused"], xs, wg, wu, wd)


def _combine_kernel(tot_ref, src_ref, src_next_ref, yb_hbm, x_ref, route_ref, g_ref, o_ref, loc_ref, sems,
                    *, final_norm):
    w = pl.program_id(0)
    last = pl.num_programs(0) - 1

    def fetch(win, table_ref):
        buf = loc_ref.at[win % 2]
        tiles = tot_ref[win] // SUBLANES

        def clear(r, carry):
            buf[pl.ds(pl.multiple_of(r * SUBLANES, SUBLANES), SUBLANES), :] = jnp.zeros((SUBLANES, D_MODEL), F32)
            return carry

        lax.fori_loop(tiles, TILES, clear, 0)
        _start_tiles(tiles, table_ref, buf, yb_hbm, sems.at[win % 2], False)

    @pl.when(w == 0)
    def _():
        fetch(w, src_ref)

    @pl.when(w < last)
    def _():
        fetch(jnp.minimum(w + 1, last), src_next_ref)

    buf = loc_ref.at[w % 2]
    _wait_tiles(tot_ref[w] // SUBLANES, buf, yb_hbm, sems.at[w % 2], False)
    route = route_ref[...]
    pos1 = route[:, 4:5].astype(jnp.int32)
    pos2 = route[:, 5:6].astype(jnp.int32)
    slot = lax.broadcasted_iota(jnp.int32, (PROJ_ROWS, SLOTS), 1)
    sel = jnp.where(slot == pos1, route[:, 2:3], 0.0) + jnp.where(slot == pos2, route[:, 3:4], 0.0)
    sel_hi = sel.astype(BF16)
    sel_lo = (sel - sel_hi.astype(F32)).astype(BF16)
    y = buf[...]
    y_hi = y.astype(BF16)
    y_lo = (y - y_hi.astype(F32)).astype(BF16)
    moe = (jnp.dot(sel_hi, y_hi, preferred_element_type=F32)
           + jnp.dot(sel_lo, y_hi, preferred_element_type=F32)
           + jnp.dot(sel_hi, y_lo, preferred_element_type=F32))
    x = x_ref[...] + moe
    if final_norm:
        ms = jnp.mean(x * x, axis=-1, keepdims=True)
        x = x * lax.rsqrt(ms + RMS_EPS) * g_ref[...]
    o_ref[...] = x


def _combine(plan, yb, x1, route, g, final_norm):
    T = x1.shape[0]
    tm = PROJ_ROWS
    nt = T // tm
    table = lambda shift: pl.BlockSpec((1, 1, TILES), lambda w, *_: (jnp.minimum(w + shift, nt - 1), 0, 0),
                                       memory_space=pltpu.SMEM)
    return pl.pallas_call(
        functools.partial(_combine_kernel, final_norm=final_norm),
        grid_spec=pltpu.PrefetchScalarGridSpec(
            num_scalar_prefetch=1,
            grid=(nt,),
            in_specs=[table(0), table(1),
                      pl.BlockSpec(memory_space=pl.ANY),
                      pl.BlockSpec((tm, D_MODEL), lambda w, *_: (w, 0)),
                      pl.BlockSpec((tm, LANES), lambda w, *_: (w, 0)),
                      pl.BlockSpec((1, D_MODEL), lambda w, *_: (0, 0))],
            out_specs=pl.BlockSpec((tm, D_MODEL), lambda w, *_: (w, 0)),
            scratch_shapes=[pltpu.VMEM((2, SLOTS, D_MODEL), F32), pltpu.SemaphoreType.DMA((2,))],
        ),
        out_shape=jax.ShapeDtypeStruct((T, D_MODEL), F32),
        compiler_params=_params(("arbitrary",)),
        name="moe_combine",
    )(plan["tot"], plan["dst"], plan["dst"], yb, x1, route, g)


def _routing_plan(wcnt, T):
    n = wcnt[:, 0, N_GROUPS:N_GROUPS + N_EXPERTS].astype(jnp.int32)
    cnt = jnp.sum(n, axis=0)
    nblk = (cnt + MOE_BLOCK - 1) // MOE_BLOCK
    blk_end = jnp.cumsum(nblk)
    first_row = (blk_end - nblk) * MOE_BLOCK
    ls = jnp.cumsum(n, axis=1) - n
    gs = first_row[None, :] + jnp.cumsum(n, axis=0) - n
    worst_rows = T * 2 + (T // PROJ_ROWS) * N_EXPERTS * SUBLANES
    n_blocks = -(-worst_rows // MOE_BLOCK) + N_EXPERTS
    blk_e = jnp.minimum(jnp.sum(jnp.arange(n_blocks)[:, None] >= blk_end[None, :], axis=1), N_EXPERTS - 1)
    tile = jnp.arange(TILES)[None, :, None]
    lt, nt8, gt = (a[:, None, :] // SUBLANES for a in (ls, n, gs))
    dst = jnp.sum(jnp.where((tile >= lt) & (tile < lt + nt8), gt + tile - lt, 0), axis=2)
    i32 = lambda a: a.reshape(-1).astype(jnp.int32)
    plan = dict(tot=i32(jnp.sum(n, axis=1)), pad_start=i32(first_row + cnt), pad_len=i32(nblk * MOE_BLOCK - cnt),
                nused=i32(blk_end[-1:]), blk_e=i32(blk_e),
                dst=dst.astype(jnp.int32).reshape(n.shape[0], 1, TILES))
    return plan, n_blocks


def _pack_w_in(w):
    scale = QK_DIM ** -0.5 * LOG2E
    q1, q2, k1, k2 = (w[:, i * QK_COLS:(i + 1) * QK_COLS].reshape(D_MODEL, N_HEADS, QK_DIM) for i in range(4))
    qq = (jnp.concatenate([q1, q2], axis=-1) * scale).reshape(D_MODEL, ATTN_WIDTH)
    kk = jnp.concatenate([k1, k2], axis=-1).reshape(D_MODEL, ATTN_WIDTH)
    v0 = 4 * QK_COLS
    packed = jnp.concatenate([qq, kk, w[:, v0 + ATTN_WIDTH:]], axis=1).astype(BF16)
    return packed, jnp.transpose(w[:, v0:v0 + ATTN_WIDTH]).astype(BF16)


def kernel(x, rel_bias, ln1_g, w_in, lam_q1, lam_k1, lam_q2, lam_k2, subln_g, conv_w, conv_b, conv_ln_g, conv_ln_b, ssm_lam_re, ssm_lam_im, ssm_log_dt, ssm_b_re, ssm_b_im, ssm_c_re, ssm_c_im, ssm_d, ssm_glu_w, ssm_glu_b, w_out, ln2_g, group_router_w, group_router_b, expert_router_w, expert_router_b, w_gate, w_up, w_down, final_g):
    B, L, D = x.shape
    T = B * L
    depth = w_in.shape[0]
    assert D == D_MODEL and L % CONV_ROWS == 0 and L % ATTN_TILE == 0 and L % SSM_ROWS == 0
    assert PROJ_ROWS == ATTN_TILE and T % PROJ_ROWS == 0 and (2 * T) % MOE_BLOCK == 0
    x2 = x.reshape(T, D)
    bias_diag, bias_sub = _bias_tiles(rel_bias)
    row = lambda v: v.astype(F32).reshape(1, -1)
    for l in range(depth):
        lam_init = 0.8 - 0.6 * math.exp(-0.3 * l)
        lam = (jnp.exp(jnp.sum(lam_q1[l].astype(F32) * lam_k1[l].astype(F32)))
               - jnp.exp(jnp.sum(lam_q2[l].astype(F32) * lam_k2[l].astype(F32))) + lam_init).reshape(1)
        qq, kk, conv_in, ssm_in, vt = _inproj(x2, row(ln1_g[l]), *_pack_w_in(w_in[l]))
        a = _attention(qq, kk, vt, lam, bias_diag, bias_sub, subln_g[l].astype(F32).reshape(V_DIM, 1),
                       1.0 - lam_init, B, L)
        c = _conformer_conv(conv_in, conv_w[l], row(conv_b[l]), row(conv_ln_g[l]), row(conv_ln_b[l]), B, L)
        wb, apr, api, wc = _ssm_weights(ssm_lam_re[l], ssm_lam_im[l], ssm_log_dt[l], ssm_b_re[l], ssm_b_im[l],
                                        ssm_c_re[l], ssm_c_im[l])
        s = _s5_ssm(ssm_in, wb, apr, api, wc, row(ssm_d[l]), ssm_glu_w[l].astype(BF16), row(ssm_glu_b[l]), B, L)
        wr = jnp.zeros((D, LANES), F32).at[:, :N_GROUPS].set(group_router_w[l]) \
            .at[:, N_GROUPS:N_GROUPS + N_EXPERTS].set(expert_router_w[l]).astype(BF16)
        br = jnp.zeros((1, LANES), F32).at[0, :N_GROUPS].set(group_router_b[l]) \
            .at[0, N_GROUPS:N_GROUPS + N_EXPERTS].set(expert_router_b[l])
        x1, h2, route, routet, wcnt = _outproj_route(x2, a, c, s, w_out[l].astype(BF16), row(ln2_g[l]), wr, br)
        plan, n_blocks = _routing_plan(wcnt, T)
        xs = _dispatch(plan, h2, routet, n_blocks)
        yb = _experts(plan, xs, w_gate, w_up, w_down, l, n_blocks)
        x2 = _combine(plan, yb, x1, route, row(final_g), final_norm=(l == depth - 1))
    return x2.reshape(B, L, D)
```

```python
import functools
import math

import jax
import jax.numpy as jnp
from jax import lax
from jax.experimental import pallas as pl
from jax.experimental.pallas import tpu as pltpu

F32 = jnp.float32
BF16 = jnp.bfloat16

D_MODEL = 1024
N_HEADS = 4
QK_DIM = 64
V_DIM = 128
ATTN_WIDTH = N_HEADS * V_DIM
QK_COLS = N_HEADS * QK_DIM
CONV_WIDTH = 256
CONV_TAPS = 31
SSM_WIDTH = 256
SSM_GROUP = 16
SSM_GROUPS = 16
SSM_STATE = 64
SSM_LANES = SSM_GROUPS * SSM_STATE
REL_BUCKETS = 32
REL_MAX_EXACT = 16
REL_MAX_DIST = 128
N_GROUPS = 4
EXPERTS_PER_GROUP = 8
N_EXPERTS = N_GROUPS * EXPERTS_PER_GROUP
EXPERT_FF = 512
RMS_EPS = 1e-6
LN_EPS = 1e-5
NEG_BIG = -1e30

LANES = 128
SUBLANES = 8
VMEM_LIMIT = 48 * 1024 * 1024

PROJ_ROWS = 512
ATTN_TILE = 512
ATTN_CHUNK = 32
ONES_ROWS = 16
ATTN_HEADS_PER_STEP = 2
LOG2E = math.log2(math.e)
CONV_ROWS = 512
CONV_HALO = 32
SSM_ROWS = 256
SSM_UNROLL = 8
SSM_SHIFTS = (1, 2, 4)
MOE_BLOCK = 512


def _params(sem):
    return pltpu.CompilerParams(dimension_semantics=sem, vmem_limit_bytes=VMEM_LIMIT)


def _inproj_kernel(x_ref, g_ref, w_ref, wvt_ref, qq_ref, kk_ref, conv_ref, ssm_ref, vt_ref):
    x = x_ref[...]
    ms = jnp.mean(x * x, axis=-1, keepdims=True)
    h = (x * lax.rsqrt(ms + RMS_EPS) * g_ref[...]).astype(BF16)
    o = 0
    for ref in (qq_ref, kk_ref, conv_ref, ssm_ref):
        n = ref.shape[-1]
        ref[...] = jnp.dot(h, w_ref[:, o:o + n], preferred_element_type=F32).astype(ref.dtype)
        o += n
    vt_ref[0] = lax.dot_general(wvt_ref[...], h, (((1,), (1,)), ((), ())),
                                preferred_element_type=F32).astype(vt_ref.dtype)


def _inproj(x2, g, w, wvt):
    T = x2.shape[0]
    tm = PROJ_ROWS
    widths = (ATTN_WIDTH, ATTN_WIDTH, 2 * CONV_WIDTH, SSM_WIDTH)
    dtypes = (BF16, BF16, F32, F32)
    return pl.pallas_call(
        _inproj_kernel,
        grid=(T // tm,),
        in_specs=[
            pl.BlockSpec((tm, D_MODEL), lambda i: (i, 0)),
            pl.BlockSpec((1, D_MODEL), lambda i: (0, 0)),
            pl.BlockSpec(w.shape, lambda i: (0, 0)),
            pl.BlockSpec(wvt.shape, lambda i: (0, 0)),
        ],
        out_specs=[pl.BlockSpec((tm, n), lambda i: (i, 0)) for n in widths]
        + [pl.BlockSpec((1, ATTN_WIDTH, tm), lambda i: (i, 0, 0))],
        out_shape=[jax.ShapeDtypeStruct((T, n), dt) for n, dt in zip(widths, dtypes)]
        + [jax.ShapeDtypeStruct((T // tm, ATTN_WIDTH, tm), BF16)],
        compiler_params=_params(("arbitrary",)),
        name="inproj",
    )(x2, g, w, wvt)


def _attn_kernel(lam_ref, q_ref, k_ref, vt_ref, bd_ref, bs_ref, g_ref, o_ref, *scratch, out_scale):
    t = ATTN_TILE
    nh = ATTN_HEADS_PER_STEP
    qi = pl.program_id(2)
    per_head = len(scratch) // nh
    heads = [scratch[h * per_head:(h + 1) * per_head] for h in range(nh)]
    cols = lambda h: slice(h * LANES, (h + 1) * LANES)

    for h, (qs_ref, m_ref, acc_ref, sa_ref, sb_ref, pa_ref, pb_ref, aa_ref, ab_ref) in enumerate(heads):
        q = q_ref[:, cols(h)].astype(F32)
        lane = lax.broadcasted_iota(jnp.int32, q.shape, 1)
        qs_ref[0:t, :] = jnp.where(lane < QK_DIM, q, 0.0).astype(BF16)
        qs_ref[t:2 * t, :] = jnp.where(lane >= QK_DIM, q, 0.0).astype(BF16)
        m_ref[...] = jnp.full(m_ref.shape, NEG_BIG, F32)
        acc_ref[...] = jnp.zeros(acc_ref.shape, F32)
        pb_ref[...] = jnp.zeros(pb_ref.shape, BF16)
        ab_ref[...] = jnp.ones(ab_ref.shape, F32)

    chunks = [(c, c + ATTN_CHUNK) for c in range(0, t, ATTN_CHUNK)]
    fold = lambda a: a.reshape(ATTN_CHUNK // SUBLANES, SUBLANES, 2 * t)

    def scores(j, which):
        for h, refs in enumerate(heads):
            k = k_ref[pl.ds(pl.multiple_of(j * t, t), t), cols(h)]
            refs[3 + which][...] = lax.dot_general(k, refs[0][...], (((1,), (1,)), ((), ())),
                                                   preferred_element_type=F32)

    def softmax(which, bias_ref=None):
        for h, refs in enumerate(heads):
            m_ref, s_ref, p_ref, a_ref = refs[1], refs[3 + which], refs[5 + which], refs[7 + which]
            top = jnp.full((SUBLANES, 2 * t), NEG_BIG, F32)
            for lo, hi in chunks:
                s = s_ref[lo:hi, :]
                if bias_ref is not None:
                    b = bias_ref[h, lo:hi, :]
                    s = s + jnp.concatenate([b, b], axis=1)
                    s_ref[lo:hi, :] = s
                top = jnp.maximum(top, jnp.max(fold(s), axis=0))
            m_prev = m_ref[...]
            m_new = jnp.maximum(m_prev, jnp.max(top, axis=0, keepdims=True))
            a_ref[...] = jnp.exp2(m_prev - m_new)
            for lo, hi in chunks:
                p_ref[lo:hi, :] = jnp.exp2(s_ref[lo:hi, :] - m_new).astype(BF16)
            m_ref[...] = m_new

    def values(j, which):
        for h, refs in enumerate(heads):
            acc_ref, p_ref, a_ref = refs[2], refs[5 + which], refs[7 + which]
            lhs = jnp.concatenate([vt_ref[jnp.maximum(j, 0), cols(h), :], jnp.ones((ONES_ROWS, t), BF16)], axis=0)
            acc_ref[...] = a_ref[...] * acc_ref[...] + jnp.dot(lhs, p_ref[...], preferred_element_type=F32)

    SET_A, SET_B = 0, 1
    nfar = jnp.maximum(qi - 1, 0)
    scores(0, SET_A)

    def far_pair(i, carry):
        k = 2 * i
        scores(k + 1, SET_B)
        softmax(SET_A)
        values(k - 1, SET_B)
        scores(k + 2, SET_A)
        softmax(SET_B)
        values(k, SET_A)
        return carry

    lax.fori_loop(0, nfar // 2, far_pair, 0)

    @pl.when(qi == 0)
    def _():
        softmax(SET_A, bd_ref)
        values(qi, SET_A)

    @pl.when((qi >= 1) & (nfar % 2 == 0))
    def _():
        scores(qi, SET_B)
        softmax(SET_A, bs_ref)
        values(qi - 2, SET_B)
        softmax(SET_B, bd_ref)
        values(qi - 1, SET_A)
        values(qi, SET_B)

    @pl.when(nfar % 2 == 1)
    def _():
        scores(qi - 1, SET_B)
        softmax(SET_A)
        values(qi - 3, SET_B)
        scores(qi, SET_A)
        softmax(SET_B, bs_ref)
        values(qi - 2, SET_A)
        softmax(SET_A, bd_ref)
        values(qi - 1, SET_B)
        values(qi, SET_A)

    for h, refs in enumerate(heads):
        acc_ref = refs[2]
        acc = acc_ref[0:V_DIM, :]
        l = acc_ref[V_DIM:V_DIM + 1, :]
        a = acc[:, 0:t] / l[:, 0:t] - lam_ref[0] * (acc[:, t:2 * t] / l[:, t:2 * t])
        ms = jnp.mean(a * a, axis=0, keepdims=True)
        y = a * lax.rsqrt(ms + RMS_EPS) * g_ref[...] * out_scale
        o_ref[:, cols(h)] = jnp.transpose(y).astype(o_ref.dtype)


def _attention(qq, kk, vt, lam, bias_diag, bias_sub, subln_g, out_scale, B, L):
    T = B * L
    t = ATTN_TILE
    nh = ATTN_HEADS_PER_STEP
    nq = L // t
    once = pl.Buffered(1)
    head_scratch = [
        pltpu.VMEM((2 * t, LANES), BF16),
        pltpu.VMEM((1, 2 * t), F32),
        pltpu.VMEM((V_DIM + ONES_ROWS, 2 * t), F32),
        pltpu.VMEM((t, 2 * t), F32),
        pltpu.VMEM((t, 2 * t), F32),
        pltpu.VMEM((t, 2 * t), BF16),
        pltpu.VMEM((t, 2 * t), BF16),
        pltpu.VMEM((1, 2 * t), F32),
        pltpu.VMEM((1, 2 * t), F32),
    ]
    return pl.pallas_call(
        functools.partial(_attn_kernel, out_scale=out_scale),
        grid=(B, N_HEADS // nh, nq),
        in_specs=[
            pl.BlockSpec(memory_space=pltpu.SMEM),
            pl.BlockSpec((t, nh * LANES), lambda b, h, i: (b * nq + i, h)),
            pl.BlockSpec((L, nh * LANES), lambda b, h, i: (b, h), pipeline_mode=once),
            pl.BlockSpec((nq, nh * V_DIM, t), lambda b, h, i: (b, h, 0), pipeline_mode=once),
            pl.BlockSpec((nh, t, t), lambda b, h, i: (h, 0, 0), pipeline_mode=once),
            pl.BlockSpec((nh, t, t), lambda b, h, i: (h, 0, 0), pipeline_mode=once),
            pl.BlockSpec((V_DIM, 1), lambda b, h, i: (0, 0)),
        ],
        out_specs=pl.BlockSpec((t, nh * LANES), lambda b, h, i: (b * nq + i, h)),
        out_shape=jax.ShapeDtypeStruct((T, ATTN_WIDTH), BF16),
        scratch_shapes=head_scratch * nh,
        compiler_params=_params(("arbitrary", "arbitrary", "arbitrary")),
        name="diff_attn",
    )(lam, qq, kk, vt, bias_diag, bias_sub, subln_g)


def _rel_bucket(rel):
    n = jnp.maximum(rel, 0)
    nf = jnp.maximum(n, 1).astype(F32)
    large = REL_MAX_EXACT + (jnp.log(nf / REL_MAX_EXACT) / math.log(REL_MAX_DIST / REL_MAX_EXACT)
                             * (REL_BUCKETS - REL_MAX_EXACT)).astype(jnp.int32)
    large = jnp.minimum(large, REL_BUCKETS - 1)
    return jnp.where(n < REL_MAX_EXACT, n, large)


def _bias_tiles(rel_table):
    t = ATTN_TILE
    assert t >= REL_MAX_DIST
    far = rel_table[REL_BUCKETS - 1].astype(F32)
    rel_d = jnp.arange(t)[None, :] - jnp.arange(t)[:, None]

    def lookup(bucket):
        out = jnp.zeros((N_HEADS,) + bucket.shape, F32)
        for b in range(REL_BUCKETS):
            out = jnp.where((bucket == b)[None], (rel_table[b].astype(F32) - far)[:, None, None], out)
        return out * LOG2E

    bd = jnp.where((rel_d >= 0)[None], lookup(_rel_bucket(rel_d)), NEG_BIG)
    return bd, lookup(_rel_bucket(rel_d + t))


def _conv_kernel(u_ref, w_ref, b_ref, g_ref, beta_ref, o_ref, h_ref, s_ref):
    tt = CONV_ROWS
    j = pl.program_id(1)

    @pl.when(j == 0)
    def _():
        h_ref[0:CONV_HALO, :] = jnp.zeros((CONV_HALO, CONV_WIDTH), F32)

    @pl.when(j > 0)
    def _():
        h_ref[0:CONV_HALO, :] = h_ref[tt:tt + CONV_HALO, :]

    u = u_ref[...]
    h_ref[CONV_HALO:CONV_HALO + tt, :] = u[:, 0:CONV_WIDTH] * jax.nn.sigmoid(u[:, CONV_WIDTH:])
    off = CONV_HALO - (CONV_TAPS - 1)
    starts = [off + k for k in range(CONV_TAPS)]
    span = tt + max(o - o % SUBLANES for o in starts if o % SUBLANES)
    assert SUBLANES - 1 + span <= tt + CONV_HALO
    for r in range(1, SUBLANES):
        s_ref[r - 1, 0:span, :] = h_ref[r:r + span, :]
    acc = jnp.broadcast_to(b_ref[...], (tt, CONV_WIDTH))
    for k in range(CONV_TAPS):
        r = (off + k) % SUBLANES
        base = off + k - r
        rows = h_ref[base:base + tt, :] if r == 0 else s_ref[r - 1, base:base + tt, :]
        acc = acc + w_ref[k:k + 1, :] * rows
    mu = jnp.mean(acc, axis=-1, keepdims=True)
    cen = acc - mu
    var = jnp.mean(cen * cen, axis=-1, keepdims=True)
    y = cen * lax.rsqrt(var + LN_EPS) * g_ref[...] + beta_ref[...]
    o_ref[...] = jax.nn.silu(y).astype(o_ref.dtype)


def _conformer_conv(conv_in, w, b, g, beta, B, L):
    T = B * L
    tt = CONV_ROWS
    nt = L // tt
    vec = pl.BlockSpec((1, CONV_WIDTH), lambda bb, j: (0, 0))
    return pl.pallas_call(
        _conv_kernel,
        grid=(B, nt),
        in_specs=[
            pl.BlockSpec((tt, 2 * CONV_WIDTH), lambda bb, j: (bb * nt + j, 0)),
            pl.BlockSpec((CONV_TAPS, CONV_WIDTH), lambda bb, j: (0, 0)),
            vec, vec, vec,
        ],
        out_specs=pl.BlockSpec((tt, CONV_WIDTH), lambda bb, j: (bb * nt + j, 0)),
        out_shape=jax.ShapeDtypeStruct((T, CONV_WIDTH), BF16),
        scratch_shapes=[pltpu.VMEM((tt + CONV_HALO, CONV_WIDTH), F32),
                        pltpu.VMEM((SUBLANES - 1, tt + CONV_HALO, CONV_WIDTH), F32)],
        compiler_params=_params(("arbitrary", "arbitrary")),
        name="conformer_conv",
    )(conv_in, w, b, g, beta)


def _ssm_kernel(u_ref, wb_ref, apr_ref, api_ref, wc_ref, d_ref, gw_ref, gb_ref, o_ref,
                xs_ref, carry_ref):
    tt = SSM_ROWS
    n = SSM_LANES
    j = pl.program_id(1)

    @pl.when(j == 0)
    def _():
        carry_ref[...] = jnp.zeros(carry_ref.shape, F32)

    u = u_ref[...]
    xs_ref[...] = jnp.dot(u.astype(BF16), wb_ref[...], preferred_element_type=F32)
    apr = apr_ref[0:SUBLANES, :]
    api = api_ref[0:SUBLANES, :]

    def block(r, carry):
        cr, ci = carry
        start = pl.multiple_of(r * SUBLANES, SUBLANES)
        xr = xs_ref[pl.ds(start, SUBLANES), 0:n]
        xi = xs_ref[pl.ds(start, SUBLANES), n:2 * n]
        for k, shift in enumerate(SSM_SHIFTS):
            ar = apr_ref[(k + 1) * SUBLANES:(k + 2) * SUBLANES, :]
            ai = api_ref[(k + 1) * SUBLANES:(k + 2) * SUBLANES, :]
            sr = pltpu.roll(xr, shift, 0)
            si = pltpu.roll(xi, shift, 0)
            xr, xi = xr + (ar * sr - ai * si), xi + (ar * si + ai * sr)
        xr, xi = xr + (apr * cr - api * ci), xi + (apr * ci + api * cr)
        xs_ref[pl.ds(start, SUBLANES), 0:n] = xr
        xs_ref[pl.ds(start, SUBLANES), n:2 * n] = xi
        return xr[SUBLANES - 1:SUBLANES, :], xi[SUBLANES - 1:SUBLANES, :]

    cr, ci = lax.fori_loop(0, tt // SUBLANES, block, (carry_ref[0:1, :], carry_ref[1:2, :]),
                           unroll=SSM_UNROLL)
    carry_ref[0:1, :] = cr
    carry_ref[1:2, :] = ci

    y = jnp.dot(xs_ref[...].astype(BF16), wc_ref[...], preferred_element_type=F32) + u * d_ref[...]
    g = jax.nn.gelu(y)
    z = jnp.dot(g.astype(BF16), gw_ref[...], preferred_element_type=F32) + gb_ref[...]
    o_ref[...] = (g * jax.nn.sigmoid(z)).astype(o_ref.dtype)


def _s5_ssm(ssm_in, wb, apr, api, wc, d, gw, gb, B, L):
    T = B * L
    tt = SSM_ROWS
    nt = L // tt
    const = lambda a: pl.BlockSpec(a.shape, lambda bb, j: (0, 0))
    return pl.pallas_call(
        _ssm_kernel,
        grid=(B, nt),
        in_specs=[pl.BlockSpec((tt, SSM_WIDTH), lambda bb, j: (bb * nt + j, 0)),
                  const(wb), const(apr), const(api), const(wc), const(d), const(gw), const(gb)],
        out_specs=pl.BlockSpec((tt, SSM_WIDTH), lambda bb, j: (bb * nt + j, 0)),
        out_shape=jax.ShapeDtypeStruct((T, SSM_WIDTH), BF16),
        scratch_shapes=[pltpu.VMEM((tt, 2 * SSM_LANES), F32), pltpu.VMEM((SUBLANES, SSM_LANES), F32)],
        compiler_params=_params(("arbitrary", "arbitrary")),
        name="s5_scan",
    )(ssm_in, wb, apr, api, wc, d, gw, gb)


def _ssm_weights(lam_re, lam_im, log_dt, b_re, b_im, c_re, c_im):
    G, P, H = SSM_GROUPS, SSM_STATE, SSM_GROUP
    dt = jnp.exp(log_dt.astype(F32))[:, None]
    lr, li = lam_re.astype(F32), lam_im.astype(F32)
    mag = jnp.exp(lr * dt)
    ar, ai = mag * jnp.cos(li * dt), mag * jnp.sin(li * dt)
    den = lr * lr + li * li
    zr = ((ar - 1.0) * lr + ai * li) / den
    zi = (ai * lr - (ar - 1.0) * li) / den
    bre, bim = b_re.astype(F32), b_im.astype(F32)
    bbr = zr[..., None] * bre - zi[..., None] * bim
    bbi = zr[..., None] * bim + zi[..., None] * bre
    eye = jnp.eye(G, dtype=F32)
    wb = jnp.concatenate([jnp.einsum('gph,gk->ghkp', bbr, eye).reshape(G * H, G * P),
                          jnp.einsum('gph,gk->ghkp', bbi, eye).reshape(G * H, G * P)], axis=1)
    wc = jnp.concatenate([jnp.einsum('ghp,gk->gpkh', c_re.astype(F32), eye).reshape(G * P, G * H),
                          -jnp.einsum('ghp,gk->gpkh', c_im.astype(F32), eye).reshape(G * P, G * H)], axis=0)
    pr, pi = [ar.reshape(1, G * P)], [ai.reshape(1, G * P)]
    for _ in range(SUBLANES - 1):
        pr, pi = (pr + [pr[-1] * pr[0] - pi[-1] * pi[0]], pi + [pr[-1] * pi[0] + pi[-1] * pr[0]])
    rows = jnp.arange(SUBLANES)[:, None]
    tr, ti = list(pr), list(pi)
    for shift in SSM_SHIFTS:
        tr.append(jnp.where(rows >= shift, pr[shift - 1], 0.0))
        ti.append(jnp.where(rows >= shift, pi[shift - 1], 0.0))
    return wb.astype(BF16), jnp.concatenate(tr, axis=0), jnp.concatenate(ti, axis=0), wc.astype(BF16)


def _outproj_kernel(x_ref, a_ref, c_ref, s_ref, w_ref, g_ref, wr_ref, br_ref,
                    x1_ref, h2_ref, route_ref, routet_ref, wcnt_ref):
    tm = PROJ_ROWS
    o1, o2 = ATTN_WIDTH, ATTN_WIDTH + CONV_WIDTH
    y = (jnp.dot(a_ref[...], w_ref[0:o1, :], preferred_element_type=F32)
         + jnp.dot(c_ref[...], w_ref[o1:o2, :], preferred_element_type=F32)
         + jnp.dot(s_ref[...], w_ref[o2:, :], preferred_element_type=F32))
    x1 = x_ref[...] + y
    x1_ref[...] = x1
    ms = jnp.mean(x1 * x1, axis=-1, keepdims=True)
    h2 = (x1 * lax.rsqrt(ms + RMS_EPS) * g_ref[...]).astype(BF16)
    h2_ref[...] = h2

    logits = jnp.dot(h2, wr_ref[...], preferred_element_type=F32) + br_ref[...]
    col = lax.broadcasted_iota(jnp.int32, logits.shape, 1)

    def first_max(vals):
        top = jnp.max(vals, axis=-1, keepdims=True)
        return top, jnp.min(jnp.where(vals == top, col, LANES), axis=-1, keepdims=True)

    glog = jnp.where(col < N_GROUPS, logits, NEG_BIG)
    gmax, gidx = first_max(glog)
    gp = 1.0 / jnp.sum(jnp.exp(glog - gmax), axis=-1, keepdims=True)
    lo = N_GROUPS + gidx * EXPERTS_PER_GROUP
    e = jnp.where((col >= lo) & (col < lo + EXPERTS_PER_GROUP), logits, NEG_BIG)
    v1, i1 = first_max(e)
    e = jnp.where(col == i1, NEG_BIG, e)
    v2, i2 = first_max(e)
    ex = jnp.exp(v2 - v1)
    w1 = gp * (1.0 / (1.0 + ex))
    w2 = gp * (ex / (1.0 + ex))

    hit1 = col == i1
    hit2 = col == i2
    onehot = jnp.where(hit1 | hit2, 1.0, 0.0)
    r = lax.broadcasted_iota(jnp.int32, (tm, tm), 0)
    c = lax.broadcasted_iota(jnp.int32, (tm, tm), 1)
    before = jnp.where(r > c, 1.0, 0.0).astype(BF16)
    prior = jnp.dot(before, onehot.astype(BF16), preferred_element_type=F32)
    count = jnp.sum(onehot, axis=0, keepdims=True)
    count = jnp.floor((count + (SUBLANES - 1)) * (1.0 / SUBLANES)) * SUBLANES
    wcnt_ref[0] = count
    run = jnp.broadcast_to(count, (SUBLANES, LANES))
    lane8 = lax.broadcasted_iota(jnp.int32, (SUBLANES, LANES), 1)
    shift = 1
    while shift < LANES:
        run = run + jnp.where(lane8 >= shift, pltpu.roll(run, shift, 1), 0.0)
        shift *= 2
    where_to = prior + (run[0:1, :] - count)
    pos1 = jnp.sum(jnp.where(hit1, where_to, 0.0), axis=-1, keepdims=True)
    pos2 = jnp.sum(jnp.where(hit2, where_to, 0.0), axis=-1, keepdims=True)

    fields = ((i1 - N_GROUPS).astype(F32), (i2 - N_GROUPS).astype(F32), w1, w2, pos1, pos2)
    route = jnp.zeros(logits.shape, F32)
    for k, val in enumerate(fields):
        route = jnp.where(col == k, val, route)
    route_ref[...] = route
    routet_ref[...] = jnp.transpose(route)[0:SUBLANES, :]


def _outproj_route(x2, a, c, s, w_out, g, wr, br):
    T = x2.shape[0]
    tm = PROJ_ROWS
    nt = T // tm
    rows = lambda n: pl.BlockSpec((tm, n), lambda i: (i, 0))
    const = lambda arr: pl.BlockSpec(arr.shape, lambda i: (0, 0))
    return pl.pallas_call(
        _outproj_kernel,
        grid=(nt,),
        in_specs=[rows(D_MODEL), rows(ATTN_WIDTH), rows(CONV_WIDTH), rows(SSM_WIDTH),
                  const(w_out), const(g), const(wr), const(br)],
        out_specs=[rows(D_MODEL), rows(D_MODEL), rows(LANES),
                   pl.BlockSpec((SUBLANES, tm), lambda i: (0, i)),
                   pl.BlockSpec((1, 1, LANES), lambda i: (i, 0, 0))],
        out_shape=[jax.ShapeDtypeStruct((T, D_MODEL), F32), jax.ShapeDtypeStruct((T, D_MODEL), BF16),
                   jax.ShapeDtypeStruct((T, LANES), F32), jax.ShapeDtypeStruct((SUBLANES, T), F32),
                   jax.ShapeDtypeStruct((nt, 1, LANES), F32)],
        compiler_params=_params(("arbitrary",)),
        name="outproj_route",
    )(x2, a, c, s, w_out, g, wr, br)


SLOTS = 2 * PROJ_ROWS + N_EXPERTS * SUBLANES
TILES = SLOTS // SUBLANES
XS_WORDS = D_MODEL // 2
assert PROJ_ROWS % SUBLANES == 0 and MOE_BLOCK % SUBLANES == 0


def _for_each_piece(n, largest, fn):
    off = jnp.int32(0)
    size = largest
    while size >= SUBLANES:
        take = (n // size) & 1

        @pl.when(take == 1)
        def _(off=off, size=size):
            fn(pl.multiple_of(off, SUBLANES), size)

        off = off + take * size
        size //= 2


def _tile_copy(buf_ref, tile, hbm_ref, hbm_tile, sem, to_hbm):
    local = buf_ref.at[pl.ds(pl.multiple_of(tile * SUBLANES, SUBLANES), SUBLANES), :]
    remote = hbm_ref.at[pl.ds(pl.multiple_of(hbm_tile * SUBLANES, SUBLANES), SUBLANES), :]
    return pltpu.make_async_copy(local, remote, sem) if to_hbm else pltpu.make_async_copy(remote, local, sem)


def _start_tiles(count, table_ref, buf_ref, hbm_ref, sem, to_hbm):
    def body(c, carry):
        _tile_copy(buf_ref, c, hbm_ref, table_ref[0, 0, c], sem, to_hbm).start()
        return carry

    lax.fori_loop(0, count, body, 0)


def _wait_tiles(count, buf_ref, hbm_ref, sem, to_hbm):
    def body(c, carry):
        _tile_copy(buf_ref, 0, hbm_ref, 0, sem, to_hbm).wait()
        return carry

    lax.fori_loop(0, count, body, 0)


def _dispatch_kernel(tot_ref, pads_ref, padn_ref, nused_ref, dst_ref, h_ref, rt_ref, xs_hbm,
                     loc_ref, zero_ref, sem, zsem, *, n_blocks):
    w = pl.program_id(0)
    last = pl.num_programs(0) - 1

    @pl.when(w == 0)
    def _():
        zero_ref[...] = jnp.zeros(zero_ref.shape, zero_ref.dtype)

        def fill(act):
            def per_expert(e, carry):
                start = pl.multiple_of(pads_ref[e], SUBLANES)

                def piece(off, size):
                    act(pltpu.make_async_copy(
                        zero_ref.at[pl.ds(0, size), :],
                        xs_hbm.at[pl.ds(pl.multiple_of(start + off, SUBLANES), size), :], zsem))
                _for_each_piece(padn_ref[e], MOE_BLOCK // 2, piece)
                return carry

            def per_block(b, carry):
                first = pl.multiple_of(b * MOE_BLOCK, MOE_BLOCK)
                act(pltpu.make_async_copy(zero_ref, xs_hbm.at[pl.ds(first, MOE_BLOCK), :], zsem))
                return carry

            lax.fori_loop(0, N_EXPERTS, per_expert, 0)
            lax.fori_loop(nused_ref[0], n_blocks, per_block, 0)

        fill(lambda cp: cp.start())
        fill(lambda cp: cp.wait())

    pos1 = rt_ref[4:5, :].astype(jnp.int32)
    pos2 = rt_ref[5:6, :].astype(jnp.int32)
    slot = lax.broadcasted_iota(jnp.int32, (SLOTS, PROJ_ROWS), 0)
    pick = jnp.where((slot == pos1) | (slot == pos2), 1.0, 0.0).astype(BF16)
    buf = loc_ref.at[w % 2]
    rows = pltpu.bitcast(jnp.dot(pick, h_ref[...], preferred_element_type=F32), jnp.uint32)
    buf[...] = rows[:, 0:XS_WORDS] | (rows[:, XS_WORDS:D_MODEL] >> 16)

    @pl.when(w > 0)
    def _():
        _wait_tiles(tot_ref[jnp.maximum(w - 1, 0)] // SUBLANES, buf, xs_hbm, sem, True)

    _start_tiles(tot_ref[w] // SUBLANES, dst_ref, buf, xs_hbm, sem, True)

    @pl.when(w == last)
    def _():
        _wait_tiles(tot_ref[w] // SUBLANES, buf, xs_hbm, sem, True)


def _dispatch(plan, h2, routet, n_blocks):
    T = h2.shape[0]
    tm = PROJ_ROWS
    return pl.pallas_call(
        functools.partial(_dispatch_kernel, n_blocks=n_blocks),
        grid_spec=pltpu.PrefetchScalarGridSpec(
            num_scalar_prefetch=4,
            grid=(T // tm,),
            in_specs=[pl.BlockSpec((1, 1, TILES), lambda w, *_: (w, 0, 0), memory_space=pltpu.SMEM),
                      pl.BlockSpec((tm, D_MODEL), lambda w, *_: (w, 0)),
                      pl.BlockSpec((SUBLANES, tm), lambda w, *_: (0, w))],
            out_specs=pl.BlockSpec(memory_space=pl.ANY),
            scratch_shapes=[pltpu.VMEM((2, SLOTS, XS_WORDS), jnp.uint32),
                            pltpu.VMEM((MOE_BLOCK, XS_WORDS), jnp.uint32),
                            pltpu.SemaphoreType.DMA, pltpu.SemaphoreType.DMA],
        ),
        out_shape=jax.ShapeDtypeStruct((n_blocks * MOE_BLOCK, XS_WORDS), jnp.uint32),
        compiler_params=_params(("arbitrary",)),
        name="moe_dispatch",
    )(plan["tot"], plan["pad_start"], plan["pad_len"], plan["nused"], plan["dst"], h2, routet)


def _expert_kernel(blk_e_ref, nused_ref, x_ref, wg_ref, wu_ref, wd_ref, o_ref):
    b = pl.program_id(0)

    @pl.when(b < nused_ref[0])
    def _():
        words = x_ref[...]
        high = pltpu.bitcast(words & jnp.uint32(0xFFFF0000), F32).astype(BF16)
        low = pltpu.bitcast(words << 16, F32).astype(BF16)
        x = jnp.concatenate([high, low], axis=1)
        gate = jnp.dot(x, wg_ref[0, 0].astype(BF16), preferred_element_type=F32)
        up = jnp.dot(x, wu_ref[0, 0].astype(BF16), preferred_element_type=F32)
        act = (jax.nn.silu(gate) * up).astype(BF16)
        o_ref[...] = jnp.dot(act, wd_ref[0, 0].astype(BF16), preferred_element_type=F32)

    @pl.when(b >= nused_ref[0])
    def _():
        o_ref[...] = jnp.zeros(o_ref.shape, o_ref.dtype)


def _experts(plan, xs, wg, wu, wd, layer, n_blocks):
    return pl.pallas_call(
        _expert_kernel,
        grid_spec=pltpu.PrefetchScalarGridSpec(
            num_scalar_prefetch=2,
            grid=(n_blocks,),
            in_specs=[pl.BlockSpec((MOE_BLOCK, XS_WORDS), lambda b, be, nu: (b, 0)),
                      pl.BlockSpec((1, 1, D_MODEL, EXPERT_FF), lambda b, be, nu: (layer, be[b], 0, 0)),
                      pl.BlockSpec((1, 1, D_MODEL, EXPERT_FF), lambda b, be, nu: (layer, be[b], 0, 0)),
                      pl.BlockSpec((1, 1, EXPERT_FF, D_MODEL), lambda b, be, nu: (layer, be[b], 0, 0))],
            out_specs=pl.BlockSpec((MOE_BLOCK, D_MODEL), lambda b, be, nu: (b, 0)),
        ),
        out_shape=jax.ShapeDtypeStruct((n_blocks * MOE_BLOCK, D_MODEL), F32),
        compiler_params=_params(("arbitrary",)),
        name="moe_experts",
    )(plan["blk_e"], plan["nused"], xs, wg, wu, wd)


def _combine_kernel(tot_ref, src_ref, src_next_ref, yb_hbm, x_ref, route_ref, g_ref, o_ref, loc_ref, sems,
                    *, final_norm):
    w = pl.program_id(0)
    last = pl.num_programs(0) - 1

    def fetch(win, table_ref):
        buf = loc_ref.at[win % 2]
        tiles = tot_ref[win] // SUBLANES

        def clear(r, carry):
            buf[pl.ds(pl.multiple_of(r * SUBLANES, SUBLANES), SUBLANES), :] = jnp.zeros((SUBLANES, D_MODEL), F32)
            return carry

        lax.fori_loop(tiles, TILES, clear, 0)
        _start_tiles(tiles, table_ref, buf, yb_hbm, sems.at[win % 2], False)

    @pl.when(w == 0)
    def _():
        fetch(w, src_ref)

    @pl.when(w < last)
    def _():
        fetch(jnp.minimum(w + 1, last), src_next_ref)

    buf = loc_ref.at[w % 2]
    _wait_tiles(tot_ref[w] // SUBLANES, buf, yb_hbm, sems.at[w % 2], False)
    route = route_ref[...]
    pos1 = route[:, 4:5].astype(jnp.int32)
    pos2 = route[:, 5:6].astype(jnp.int32)
    slot = lax.broadcasted_iota(jnp.int32, (PROJ_ROWS, SLOTS), 1)
    sel = jnp.where(slot == pos1, route[:, 2:3], 0.0) + jnp.where(slot == pos2, route[:, 3:4], 0.0)
    sel_hi = sel.astype(BF16)
    sel_lo = (sel - sel_hi.astype(F32)).astype(BF16)
    y = buf[...]
    y_hi = y.astype(BF16)
    y_lo = (y - y_hi.astype(F32)).astype(BF16)
    moe = (jnp.dot(sel_hi, y_hi, preferred_element_type=F32)
           + jnp.dot(sel_lo, y_hi, preferred_element_type=F32)
           + jnp.dot(sel_hi, y_lo, preferred_element_type=F32))
    x = x_ref[...] + moe
    if final_norm:
        ms = jnp.mean(x * x, axis=-1, keepdims=True)
        x = x * lax.rsqrt(ms + RMS_EPS) * g_ref[...]
    o_ref[...] = x


def _combine(plan, yb, x1, route, g, final_norm):
    T = x1.shape[0]
    tm = PROJ_ROWS
    nt = T // tm
    table = lambda shift: pl.BlockSpec((1, 1, TILES), lambda w, *_: (jnp.minimum(w + shift, nt - 1), 0, 0),
                                       memory_space=pltpu.SMEM)
    return pl.pallas_call(
        functools.partial(_combine_kernel, final_norm=final_norm),
        grid_spec=pltpu.PrefetchScalarGridSpec(
            num_scalar_prefetch=1,
            grid=(nt,),
            in_specs=[table(0), table(1),
                      pl.BlockSpec(memory_space=pl.ANY),
                      pl.BlockSpec((tm, D_MODEL), lambda w, *_: (w, 0)),
                      pl.BlockSpec((tm, LANES), lambda w, *_: (w, 0)),
                      pl.BlockSpec((1, D_MODEL), lambda w, *_: (0, 0))],
            out_specs=pl.BlockSpec((tm, D_MODEL), lambda w, *_: (w, 0)),
            scratch_shapes=[pltpu.VMEM((2, SLOTS, D_MODEL), F32), pltpu.SemaphoreType.DMA((2,))],
        ),
        out_shape=jax.ShapeDtypeStruct((T, D_MODEL), F32),
        compiler_params=_params(("arbitrary",)),
        name="moe_combine",
    )(plan["tot"], plan["dst"], plan["dst"], yb, x1, route, g)


def _routing_plan(wcnt, T):
    n = wcnt[:, 0, N_GROUPS:N_GROUPS + N_EXPERTS].astype(jnp.int32)
    cnt = jnp.sum(n, axis=0)
    nblk = (cnt + MOE_BLOCK - 1) // MOE_BLOCK
    blk_end = jnp.cumsum(nblk)
    first_row = (blk_end - nblk) * MOE_BLOCK
    ls = jnp.cumsum(n, axis=1) - n
    gs = first_row[None, :] + jnp.cumsum(n, axis=0) - n
    worst_rows = T * 2 + (T // PROJ_ROWS) * N_EXPERTS * SUBLANES
    n_blocks = -(-worst_rows // MOE_BLOCK) + N_EXPERTS
    blk_e = jnp.minimum(jnp.sum(jnp.arange(n_blocks)[:, None] >= blk_end[None, :], axis=1), N_EXPERTS - 1)
    tile = jnp.arange(TILES)[None, :, None]
    lt, nt8, gt = (a[:, None, :] // SUBLANES for a in (ls, n, gs))
    dst = jnp.sum(jnp.where((tile >= lt) & (tile < lt + nt8), gt + tile - lt, 0), axis=2)
    i32 = lambda a: a.reshape(-1).astype(jnp.int32)
    plan = dict(tot=i32(jnp.sum(n, axis=1)), pad_start=i32(first_row + cnt), pad_len=i32(nblk * MOE_BLOCK - cnt),
                nused=i32(blk_end[-1:]), blk_e=i32(blk_e),
                dst=dst.astype(jnp.int32).reshape(n.shape[0], 1, TILES))
    return plan, n_blocks


def _pack_w_in(w):
    scale = QK_DIM ** -0.5 * LOG2E
    q1, q2, k1, k2 = (w[:, i * QK_COLS:(i + 1) * QK_COLS].reshape(D_MODEL, N_HEADS, QK_DIM) for i in range(4))
    qq = (jnp.concatenate([q1, q2], axis=-1) * scale).reshape(D_MODEL, ATTN_WIDTH)
    kk = jnp.concatenate([k1, k2], axis=-1).reshape(D_MODEL, ATTN_WIDTH)
    v0 = 4 * QK_COLS
    packed = jnp.concatenate([qq, kk, w[:, v0 + ATTN_WIDTH:]], axis=1).astype(BF16)
    return packed, jnp.transpose(w[:, v0:v0 + ATTN_WIDTH]).astype(BF16)


def kernel(x, rel_bias, ln1_g, w_in, lam_q1, lam_k1, lam_q2, lam_k2, subln_g, conv_w, conv_b, conv_ln_g, conv_ln_b, ssm_lam_re, ssm_lam_im, ssm_log_dt, ssm_b_re, ssm_b_im, ssm_c_re, ssm_c_im, ssm_d, ssm_glu_w, ssm_glu_b, w_out, ln2_g, group_router_w, group_router_b, expert_router_w, expert_router_b, w_gate, w_up, w_down, final_g):
    B, L, D = x.shape
    T = B * L
    depth = w_in.shape[0]
    assert D == D_MODEL and L % CONV_ROWS == 0 and L % ATTN_TILE == 0 and L % SSM_ROWS == 0
    assert PROJ_ROWS == ATTN_TILE and T % PROJ_ROWS == 0 and (2 * T) % MOE_BLOCK == 0
    x2 = x.reshape(T, D)
    bias_diag, bias_sub = _bias_tiles(rel_bias)
    row = lambda v: v.astype(F32).reshape(1, -1)
    for l in range(depth):
        lam_init = 0.8 - 0.6 * math.exp(-0.3 * l)
        lam = (jnp.exp(jnp.sum(lam_q1[l].astype(F32) * lam_k1[l].astype(F32)))
               - jnp.exp(jnp.sum(lam_q2[l].astype(F32) * lam_k2[l].astype(F32))) + lam_init).reshape(1)
        qq, kk, conv_in, ssm_in, vt = _inproj(x2, row(ln1_g[l]), *_pack_w_in(w_in[l]))
        a = _attention(qq, kk, vt, lam, bias_diag, bias_sub, subln_g[l].astype(F32).reshape(V_DIM, 1),
                       1.0 - lam_init, B, L)
        c = _conformer_conv(conv_in, conv_w[l], row(conv_b[l]), row(conv_ln_g[l]), row(conv_ln_b[l]), B, L)
        wb, apr, api, wc = _ssm_weights(ssm_lam_re[l], ssm_lam_im[l], ssm_log_dt[l], ssm_b_re[l], ssm_b_im[l],
                                        ssm_c_re[l], ssm_c_im[l])
        s = _s5_ssm(ssm_in, wb, apr, api, wc, row(ssm_d[l]), ssm_glu_w[l].astype(BF16), row(ssm_glu_b[l]), B, L)
        wr = jnp.zeros((D, LANES), F32).at[:, :N_GROUPS].set(group_router_w[l]) \
            .at[:, N_GROUPS:N_GROUPS + N_EXPERTS].set(expert_router_w[l]).astype(BF16)
        br = jnp.zeros((1, LANES), F32).at[0, :N_GROUPS].set(group_router_b[l]) \
            .at[0, N_GROUPS:N_GROUPS + N_EXPERTS].set(expert_router_b[l])
        x1, h2, route, routet, wcnt = _outproj_route(x2, a, c, s, w_out[l].astype(BF16), row(ln2_g[l]), wr, br)
        plan, n_blocks = _routing_plan(wcnt, T)
        xs = _dispatch(plan, h2, routet, n_blocks)
        yb = _experts(plan, xs, w_gate, w_up, w_down, l, n_blocks)
        x2 = _combine(plan, yb, x1, route, row(final_g), final_norm=(l == depth - 1))
    return x2.reshape(B, L, D)
```

```python
import functools
import math

import jax
import jax.numpy as jnp
from jax import lax
from jax.experimental import pallas as pl
from jax.experimental.pallas import tpu as pltpu

F32 = jnp.float32
BF16 = jnp.bfloat16

D_MODEL = 1024
N_HEADS = 4
QK_DIM = 64
V_DIM = 128
ATTN_WIDTH = N_HEADS * V_DIM
QK_COLS = N_HEADS * QK_DIM
CONV_WIDTH = 256
CONV_TAPS = 31
SSM_WIDTH = 256
SSM_GROUP = 16
SSM_GROUPS = 16
SSM_STATE = 64
SSM_LANES = SSM_GROUPS * SSM_STATE
REL_BUCKETS = 32
REL_MAX_EXACT = 16
REL_MAX_DIST = 128
N_GROUPS = 4
EXPERTS_PER_GROUP = 8
N_EXPERTS = N_GROUPS * EXPERTS_PER_GROUP
EXPERT_FF = 512
RMS_EPS = 1e-6
LN_EPS = 1e-5
NEG_BIG = -1e30

LANES = 128
SUBLANES = 8
VMEM_LIMIT = 48 * 1024 * 1024

PROJ_ROWS = 512
ATTN_TILE = 512
ATTN_CHUNK = 32
ONES_ROWS = 16
ATTN_HEADS_PER_STEP = 2
LOG2E = math.log2(math.e)
CONV_ROWS = 512
CONV_HALO = 32
SSM_ROWS = 512
SSM_UNROLL = 8
SSM_SHIFTS = (1, 2, 4)
MOE_BLOCK = 512


def _params(sem):
    return pltpu.CompilerParams(dimension_semantics=sem, vmem_limit_bytes=VMEM_LIMIT)


def _inproj_kernel(x_ref, g_ref, w_ref, wvt_ref, qq_ref, kk_ref, conv_ref, ssm_ref, vt_ref):
    x = x_ref[...]
    ms = jnp.mean(x * x, axis=-1, keepdims=True)
    h = (x * lax.rsqrt(ms + RMS_EPS) * g_ref[...]).astype(BF16)
    o = 0
    for ref in (qq_ref, kk_ref, conv_ref, ssm_ref):
        n = ref.shape[-1]
        ref[...] = jnp.dot(h, w_ref[:, o:o + n], preferred_element_type=F32).astype(ref.dtype)
        o += n
    vt_ref[0] = lax.dot_general(wvt_ref[...], h, (((1,), (1,)), ((), ())),
                                preferred_element_type=F32).astype(vt_ref.dtype)


def _inproj(x2, g, w, wvt):
    T = x2.shape[0]
    tm = PROJ_ROWS
    widths = (ATTN_WIDTH, ATTN_WIDTH, 2 * CONV_WIDTH, SSM_WIDTH)
    dtypes = (BF16, BF16, F32, F32)
    return pl.pallas_call(
        _inproj_kernel,
        grid=(T // tm,),
        in_specs=[
            pl.BlockSpec((tm, D_MODEL), lambda i: (i, 0)),
            pl.BlockSpec((1, D_MODEL), lambda i: (0, 0)),
            pl.BlockSpec(w.shape, lambda i: (0, 0)),
            pl.BlockSpec(wvt.shape, lambda i: (0, 0)),
        ],
        out_specs=[pl.BlockSpec((tm, n), lambda i: (i, 0)) for n in widths]
        + [pl.BlockSpec((1, ATTN_WIDTH, tm), lambda i: (i, 0, 0))],
        out_shape=[jax.ShapeDtypeStruct((T, n), dt) for n, dt in zip(widths, dtypes)]
        + [jax.ShapeDtypeStruct((T // tm, ATTN_WIDTH, tm), BF16)],
        compiler_params=_params(("arbitrary",)),
        name="inproj",
    )(x2, g, w, wvt)


def _attn_kernel(lam_ref, q_ref, k_ref, vt_ref, bd_ref, bs_ref, g_ref, o_ref, *scratch, out_scale):
    t = ATTN_TILE
    nh = ATTN_HEADS_PER_STEP
    qi = pl.program_id(2)
    per_head = len(scratch) // nh
    heads = [scratch[h * per_head:(h + 1) * per_head] for h in range(nh)]
    cols = lambda h: slice(h * LANES, (h + 1) * LANES)

    for h, (qs_ref, m_ref, acc_ref, sa_ref, sb_ref, pa_ref, pb_ref, aa_ref, ab_ref) in enumerate(heads):
        q = q_ref[:, cols(h)].astype(F32)
        lane = lax.broadcasted_iota(jnp.int32, q.shape, 1)
        qs_ref[0:t, :] = jnp.where(lane < QK_DIM, q, 0.0).astype(BF16)
        qs_ref[t:2 * t, :] = jnp.where(lane >= QK_DIM, q, 0.0).astype(BF16)
        m_ref[...] = jnp.full(m_ref.shape, NEG_BIG, F32)
        acc_ref[...] = jnp.zeros(acc_ref.shape, F32)
        pb_ref[...] = jnp.zeros(pb_ref.shape, BF16)
        ab_ref[...] = jnp.ones(ab_ref.shape, F32)

    chunks = [(c, c + ATTN_CHUNK) for c in range(0, t, ATTN_CHUNK)]
    fold = lambda a: a.reshape(ATTN_CHUNK // SUBLANES, SUBLANES, 2 * t)

    def scores(j, which):
        for h, refs in enumerate(heads):
            k = k_ref[pl.ds(pl.multiple_of(j * t, t), t), cols(h)]
            refs[3 + which][...] = lax.dot_general(k, refs[0][...], (((1,), (1,)), ((), ())),
                                                   preferred_element_type=F32)

    def softmax(which, bias_ref=None):
        for h, refs in enumerate(heads):
            m_ref, s_ref, p_ref, a_ref = refs[1], refs[3 + which], refs[5 + which], refs[7 + which]
            top = jnp.full((SUBLANES, 2 * t), NEG_BIG, F32)
            for lo, hi in chunks:
                s = s_ref[lo:hi, :]
                if bias_ref is not None:
                    b = bias_ref[h, lo:hi, :]
                    s = s + jnp.concatenate([b, b], axis=1)
                    s_ref[lo:hi, :] = s
                top = jnp.maximum(top, jnp.max(fold(s), axis=0))
            m_prev = m_ref[...]
            m_new = jnp.maximum(m_prev, jnp.max(top, axis=0, keepdims=True))
            a_ref[...] = jnp.exp2(m_prev - m_new)
            for lo, hi in chunks:
                p_ref[lo:hi, :] = jnp.exp2(s_ref[lo:hi, :] - m_new).astype(BF16)
            m_ref[...] = m_new

    def values(j, which):
        for h, refs in enumerate(heads):
            acc_ref, p_ref, a_ref = refs[2], refs[5 + which], refs[7 + which]
            lhs = jnp.concatenate([vt_ref[jnp.maximum(j, 0), cols(h), :], jnp.ones((ONES_ROWS, t), BF16)], axis=0)
            acc_ref[...] = a_ref[...] * acc_ref[...] + jnp.dot(lhs, p_ref[...], preferred_element_type=F32)

    SET_A, SET_B = 0, 1
    nfar = jnp.maximum(qi - 1, 0)
    scores(0, SET_A)

    def far_pair(i, carry):
        k = 2 * i
        scores(k + 1, SET_B)
        softmax(SET_A)
        values(k - 1, SET_B)
        scores(k + 2, SET_A)
        softmax(SET_B)
        values(k, SET_A)
        return carry

    lax.fori_loop(0, nfar // 2, far_pair, 0)

    @pl.when(qi == 0)
    def _():
        softmax(SET_A, bd_ref)
        values(qi, SET_A)

    @pl.when((qi >= 1) & (nfar % 2 == 0))
    def _():
        scores(qi, SET_B)
        softmax(SET_A, bs_ref)
        values(qi - 2, SET_B)
        softmax(SET_B, bd_ref)
        values(qi - 1, SET_A)
        values(qi, SET_B)

    @pl.when(nfar % 2 == 1)
    def _():
        scores(qi - 1, SET_B)
        softmax(SET_A)
        values(qi - 3, SET_B)
        scores(qi, SET_A)
        softmax(SET_B, bs_ref)
        values(qi - 2, SET_A)
        softmax(SET_A, bd_ref)
        values(qi - 1, SET_B)
        values(qi, SET_A)

    for h, refs in enumerate(heads):
        acc_ref = refs[2]
        acc = acc_ref[0:V_DIM, :]
        l = acc_ref[V_DIM:V_DIM + 1, :]
        a = acc[:, 0:t] / l[:, 0:t] - lam_ref[0] * (acc[:, t:2 * t] / l[:, t:2 * t])
        ms = jnp.mean(a * a, axis=0, keepdims=True)
        y = a * lax.rsqrt(ms + RMS_EPS) * g_ref[...] * out_scale
        o_ref[:, cols(h)] = jnp.transpose(y).astype(o_ref.dtype)


def _attention(qq, kk, vt, lam, bias_diag, bias_sub, subln_g, out_scale, B, L):
    T = B * L
    t = ATTN_TILE
    nh = ATTN_HEADS_PER_STEP
    nq = L // t
    once = pl.Buffered(1)
    head_scratch = [
        pltpu.VMEM((2 * t, LANES), BF16),
        pltpu.VMEM((1, 2 * t), F32),
        pltpu.VMEM((V_DIM + ONES_ROWS, 2 * t), F32),
        pltpu.VMEM((t, 2 * t), F32),
        pltpu.VMEM((t, 2 * t), F32),
        pltpu.VMEM((t, 2 * t), BF16),
        pltpu.VMEM((t, 2 * t), BF16),
        pltpu.VMEM((1, 2 * t), F32),
        pltpu.VMEM((1, 2 * t), F32),
    ]
    return pl.pallas_call(
        functools.partial(_attn_kernel, out_scale=out_scale),
        grid=(B, N_HEADS // nh, nq),
        in_specs=[
            pl.BlockSpec(memory_space=pltpu.SMEM),
            pl.BlockSpec((t, nh * LANES), lambda b, h, i: (b * nq + i, h)),
            pl.BlockSpec((L, nh * LANES), lambda b, h, i: (b, h), pipeline_mode=once),
            pl.BlockSpec((nq, nh * V_DIM, t), lambda b, h, i: (b, h, 0), pipeline_mode=once),
            pl.BlockSpec((nh, t, t), lambda b, h, i: (h, 0, 0), pipeline_mode=once),
            pl.BlockSpec((nh, t, t), lambda b, h, i: (h, 0, 0), pipeline_mode=once),
            pl.BlockSpec((V_DIM, 1), lambda b, h, i: (0, 0)),
        ],
        out_specs=pl.BlockSpec((t, nh * LANES), lambda b, h, i: (b * nq + i, h)),
        out_shape=jax.ShapeDtypeStruct((T, ATTN_WIDTH), BF16),
        scratch_shapes=head_scratch * nh,
        compiler_params=_params(("arbitrary", "arbitrary", "arbitrary")),
        name="diff_attn",
    )(lam, qq, kk, vt, bias_diag, bias_sub, subln_g)


def _rel_bucket(rel):
    n = jnp.maximum(rel, 0)
    nf = jnp.maximum(n, 1).astype(F32)
    large = REL_MAX_EXACT + (jnp.log(nf / REL_MAX_EXACT) / math.log(REL_MAX_DIST / REL_MAX_EXACT)
                             * (REL_BUCKETS - REL_MAX_EXACT)).astype(jnp.int32)
    large = jnp.minimum(large, REL_BUCKETS - 1)
    return jnp.where(n < REL_MAX_EXACT, n, large)


def _bias_tiles(rel_table):
    t = ATTN_TILE
    assert t >= REL_MAX_DIST
    far = rel_table[REL_BUCKETS - 1].astype(F32)
    rel_d = jnp.arange(t)[None, :] - jnp.arange(t)[:, None]

    def lookup(bucket):
        out = jnp.zeros((N_HEADS,) + bucket.shape, F32)
        for b in range(REL_BUCKETS):
            out = jnp.where((bucket == b)[None], (rel_table[b].astype(F32) - far)[:, None, None], out)
        return out * LOG2E

    bd = jnp.where((rel_d >= 0)[None], lookup(_rel_bucket(rel_d)), NEG_BIG)
    return bd, lookup(_rel_bucket(rel_d + t))


def _conv_kernel(u_ref, w_ref, b_ref, g_ref, beta_ref, o_ref, h_ref, s_ref):
    tt = CONV_ROWS
    j = pl.program_id(1)

    @pl.when(j == 0)
    def _():
        h_ref[0:CONV_HALO, :] = jnp.zeros((CONV_HALO, CONV_WIDTH), F32)

    @pl.when(j > 0)
    def _():
        h_ref[0:CONV_HALO, :] = h_ref[tt:tt + CONV_HALO, :]

    u = u_ref[...]
    h_ref[CONV_HALO:CONV_HALO + tt, :] = u[:, 0:CONV_WIDTH] * jax.nn.sigmoid(u[:, CONV_WIDTH:])
    off = CONV_HALO - (CONV_TAPS - 1)
    starts = [off + k for k in range(CONV_TAPS)]
    span = tt + max(o - o % SUBLANES for o in starts if o % SUBLANES)
    assert SUBLANES - 1 + span <= tt + CONV_HALO
    for r in range(1, SUBLANES):
        s_ref[r - 1, 0:span, :] = h_ref[r:r + span, :]
    acc = jnp.broadcast_to(b_ref[...], (tt, CONV_WIDTH))
    for k in range(CONV_TAPS):
        r = (off + k) % SUBLANES
        base = off + k - r
        rows = h_ref[base:base + tt, :] if r == 0 else s_ref[r - 1, base:base + tt, :]
        acc = acc + w_ref[k:k + 1, :] * rows
    mu = jnp.mean(acc, axis=-1, keepdims=True)
    cen = acc - mu
    var = jnp.mean(cen * cen, axis=-1, keepdims=True)
    y = cen * lax.rsqrt(var + LN_EPS) * g_ref[...] + beta_ref[...]
    o_ref[...] = jax.nn.silu(y).astype(o_ref.dtype)


def _conformer_conv(conv_in, w, b, g, beta, B, L):
    T = B * L
    tt = CONV_ROWS
    nt = L // tt
    vec = pl.BlockSpec((1, CONV_WIDTH), lambda bb, j: (0, 0))
    return pl.pallas_call(
        _conv_kernel,
        grid=(B, nt),
        in_specs=[
            pl.BlockSpec((tt, 2 * CONV_WIDTH), lambda bb, j: (bb * nt + j, 0)),
            pl.BlockSpec((CONV_TAPS, CONV_WIDTH), lambda bb, j: (0, 0)),
            vec, vec, vec,
        ],
        out_specs=pl.BlockSpec((tt, CONV_WIDTH), lambda bb, j: (bb * nt + j, 0)),
        out_shape=jax.ShapeDtypeStruct((T, CONV_WIDTH), BF16),
        scratch_shapes=[pltpu.VMEM((tt + CONV_HALO, CONV_WIDTH), F32),
                        pltpu.VMEM((SUBLANES - 1, tt + CONV_HALO, CONV_WIDTH), F32)],
        compiler_params=_params(("arbitrary", "arbitrary")),
        name="conformer_conv",
    )(conv_in, w, b, g, beta)


def _ssm_kernel(u_ref, wb_ref, apr_ref, api_ref, wc_ref, d_ref, gw_ref, gb_ref, o_ref,
                xs_ref, carry_ref):
    tt = SSM_ROWS
    n = SSM_LANES
    j = pl.program_id(1)

    @pl.when(j == 0)
    def _():
        carry_ref[...] = jnp.zeros(carry_ref.shape, F32)

    u = u_ref[...]
    xs_ref[...] = jnp.dot(u.astype(BF16), wb_ref[...], preferred_element_type=F32)
    apr = apr_ref[0:SUBLANES, :]
    api = api_ref[0:SUBLANES, :]

    def block(r, carry):
        cr, ci = carry
        start = pl.multiple_of(r * SUBLANES, SUBLANES)
        xr = xs_ref[pl.ds(start, SUBLANES), 0:n]
        xi = xs_ref[pl.ds(start, SUBLANES), n:2 * n]
        for k, shift in enumerate(SSM_SHIFTS):
            ar = apr_ref[(k + 1) * SUBLANES:(k + 2) * SUBLANES, :]
            ai = api_ref[(k + 1) * SUBLANES:(k + 2) * SUBLANES, :]
            sr = pltpu.roll(xr, shift, 0)
            si = pltpu.roll(xi, shift, 0)
            xr, xi = xr + (ar * sr - ai * si), xi + (ar * si + ai * sr)
        xr, xi = xr + (apr * cr - api * ci), xi + (apr * ci + api * cr)
        xs_ref[pl.ds(start, SUBLANES), 0:n] = xr
        xs_ref[pl.ds(start, SUBLANES), n:2 * n] = xi
        return xr[SUBLANES - 1:SUBLANES, :], xi[SUBLANES - 1:SUBLANES, :]

    cr, ci = lax.fori_loop(0, tt // SUBLANES, block, (carry_ref[0:1, :], carry_ref[1:2, :]),
                           unroll=SSM_UNROLL)
    carry_ref[0:1, :] = cr
    carry_ref[1:2, :] = ci

    y = jnp.dot(xs_ref[...].astype(BF16), wc_ref[...], preferred_element_type=F32) + u * d_ref[...]
    g = jax.nn.gelu(y)
    z = jnp.dot(g.astype(BF16), gw_ref[...], preferred_element_type=F32) + gb_ref[...]
    o_ref[...] = (g * jax.nn.sigmoid(z)).astype(o_ref.dtype)


def _s5_ssm(ssm_in, wb, apr, api, wc, d, gw, gb, B, L):
    T = B * L
    tt = SSM_ROWS
    nt = L // tt
    const = lambda a: pl.BlockSpec(a.shape, lambda bb, j: (0, 0))
    return pl.pallas_call(
        _ssm_kernel,
        grid=(B, nt),
        in_specs=[pl.BlockSpec((tt, SSM_WIDTH), lambda bb, j: (bb * nt + j, 0)),
                  const(wb), const(apr), const(api), const(wc), const(d), const(gw), const(gb)],
        out_specs=pl.BlockSpec((tt, SSM_WIDTH), lambda bb, j: (bb * nt + j, 0)),
        out_shape=jax.ShapeDtypeStruct((T, SSM_WIDTH), BF16),
        scratch_shapes=[pltpu.VMEM((tt, 2 * SSM_LANES), F32), pltpu.VMEM((SUBLANES, SSM_LANES), F32)],
        compiler_params=_params(("arbitrary", "arbitrary")),
        name="s5_scan",
    )(ssm_in, wb, apr, api, wc, d, gw, gb)


def _ssm_weights(lam_re, lam_im, log_dt, b_re, b_im, c_re, c_im):
    G, P, H = SSM_GROUPS, SSM_STATE, SSM_GROUP
    dt = jnp.exp(log_dt.astype(F32))[:, None]
    lr, li = lam_re.astype(F32), lam_im.astype(F32)
    mag = jnp.exp(lr * dt)
    ar, ai = mag * jnp.cos(li * dt), mag * jnp.sin(li * dt)
    den = lr * lr + li * li
    zr = ((ar - 1.0) * lr + ai * li) / den
    zi = (ai * lr - (ar - 1.0) * li) / den
    bre, bim = b_re.astype(F32), b_im.astype(F32)
    bbr = zr[..., None] * bre - zi[..., None] * bim
    bbi = zr[..., None] * bim + zi[..., None] * bre
    eye = jnp.eye(G, dtype=F32)
    wb = jnp.concatenate([jnp.einsum('gph,gk->ghkp', bbr, eye).reshape(G * H, G * P),
                          jnp.einsum('gph,gk->ghkp', bbi, eye).reshape(G * H, G * P)], axis=1)
    wc = jnp.concatenate([jnp.einsum('ghp,gk->gpkh', c_re.astype(F32), eye).reshape(G * P, G * H),
                          -jnp.einsum('ghp,gk->gpkh', c_im.astype(F32), eye).reshape(G * P, G * H)], axis=0)
    pr, pi = [ar.reshape(1, G * P)], [ai.reshape(1, G * P)]
    for _ in range(SUBLANES - 1):
        pr, pi = (pr + [pr[-1] * pr[0] - pi[-1] * pi[0]], pi + [pr[-1] * pi[0] + pi[-1] * pr[0]])
    rows = jnp.arange(SUBLANES)[:, None]
    tr, ti = list(pr), list(pi)
    for shift in SSM_SHIFTS:
        tr.append(jnp.where(rows >= shift, pr[shift - 1], 0.0))
        ti.append(jnp.where(rows >= shift, pi[shift - 1], 0.0))
    return wb.astype(BF16), jnp.concatenate(tr, axis=0), jnp.concatenate(ti, axis=0), wc.astype(BF16)


def _outproj_kernel(x_ref, a_ref, c_ref, s_ref, w_ref, g_ref, wr_ref, br_ref,
                    x1_ref, h2_ref, route_ref, routet_ref, wcnt_ref):
    tm = PROJ_ROWS
    o1, o2 = ATTN_WIDTH, ATTN_WIDTH + CONV_WIDTH
    y = (jnp.dot(a_ref[...], w_ref[0:o1, :], preferred_element_type=F32)
         + jnp.dot(c_ref[...], w_ref[o1:o2, :], preferred_element_type=F32)
         + jnp.dot(s_ref[...], w_ref[o2:, :], preferred_element_type=F32))
    x1 = x_ref[...] + y
    x1_ref[...] = x1
    ms = jnp.mean(x1 * x1, axis=-1, keepdims=True)
    h2 = (x1 * lax.rsqrt(ms + RMS_EPS) * g_ref[...]).astype(BF16)
    h2_ref[...] = h2

    logits = jnp.dot(h2, wr_ref[...], preferred_element_type=F32) + br_ref[...]
    col = lax.broadcasted_iota(jnp.int32, logits.shape, 1)

    def first_max(vals):
        top = jnp.max(vals, axis=-1, keepdims=True)
        return top, jnp.min(jnp.where(vals == top, col, LANES), axis=-1, keepdims=True)

    glog = jnp.where(col < N_GROUPS, logits, NEG_BIG)
    gmax, gidx = first_max(glog)
    gp = 1.0 / jnp.sum(jnp.exp(glog - gmax), axis=-1, keepdims=True)
    lo = N_GROUPS + gidx * EXPERTS_PER_GROUP
    e = jnp.where((col >= lo) & (col < lo + EXPERTS_PER_GROUP), logits, NEG_BIG)
    v1, i1 = first_max(e)
    e = jnp.where(col == i1, NEG_BIG, e)
    v2, i2 = first_max(e)
    ex = jnp.exp(v2 - v1)
    w1 = gp * (1.0 / (1.0 + ex))
    w2 = gp * (ex / (1.0 + ex))

    hit1 = col == i1
    hit2 = col == i2
    onehot = jnp.where(hit1 | hit2, 1.0, 0.0)
    r = lax.broadcasted_iota(jnp.int32, (tm, tm), 0)
    c = lax.broadcasted_iota(jnp.int32, (tm, tm), 1)
    before = jnp.where(r > c, 1.0, 0.0).astype(BF16)
    prior = jnp.dot(before, onehot.astype(BF16), preferred_element_type=F32)
    count = jnp.sum(onehot, axis=0, keepdims=True)
    count = jnp.floor((count + (SUBLANES - 1)) * (1.0 / SUBLANES)) * SUBLANES
    wcnt_ref[0] = count
    run = jnp.broadcast_to(count, (SUBLANES, LANES))
    lane8 = lax.broadcasted_iota(jnp.int32, (SUBLANES, LANES), 1)
    shift = 1
    while shift < LANES:
        run = run + jnp.where(lane8 >= shift, pltpu.roll(run, shift, 1), 0.0)
        shift *= 2
    where_to = prior + (run[0:1, :] - count)
    pos1 = jnp.sum(jnp.where(hit1, where_to, 0.0), axis=-1, keepdims=True)
    pos2 = jnp.sum(jnp.where(hit2, where_to, 0.0), axis=-1, keepdims=True)

    fields = ((i1 - N_GROUPS).astype(F32), (i2 - N_GROUPS).astype(F32), w1, w2, pos1, pos2)
    route = jnp.zeros(logits.shape, F32)
    for k, val in enumerate(fields):
        route = jnp.where(col == k, val, route)
    route_ref[...] = route
    routet_ref[...] = jnp.transpose(route)[0:SUBLANES, :]


def _outproj_route(x2, a, c, s, w_out, g, wr, br):
    T = x2.shape[0]
    tm = PROJ_ROWS
    nt = T // tm
    rows = lambda n: pl.BlockSpec((tm, n), lambda i: (i, 0))
    const = lambda arr: pl.BlockSpec(arr.shape, lambda i: (0, 0))
    return pl.pallas_call(
        _outproj_kernel,
        grid=(nt,),
        in_specs=[rows(D_MODEL), rows(ATTN_WIDTH), rows(CONV_WIDTH), rows(SSM_WIDTH),
                  const(w_out), const(g), const(wr), const(br)],
        out_specs=[rows(D_MODEL), rows(D_MODEL), rows(LANES),
                   pl.BlockSpec((SUBLANES, tm), lambda i: (0, i)),
                   pl.BlockSpec((1, 1, LANES), lambda i: (i, 0, 0))],
        out_shape=[jax.ShapeDtypeStruct((T, D_MODEL), F32), jax.ShapeDtypeStruct((T, D_MODEL), BF16),
                   jax.ShapeDtypeStruct((T, LANES), F32), jax.ShapeDtypeStruct((SUBLANES, T), F32),
                   jax.ShapeDtypeStruct((nt, 1, LANES), F32)],
        compiler_params=_params(("arbitrary",)),
        name="outproj_route",
    )(x2, a, c, s, w_out, g, wr, br)


SLOTS = 2 * PROJ_ROWS + N_EXPERTS * SUBLANES
TILES = SLOTS // SUBLANES
XS_WORDS = D_MODEL // 2
assert PROJ_ROWS % SUBLANES == 0 and MOE_BLOCK % SUBLANES == 0


def _for_each_piece(n, largest, fn):
    off = jnp.int32(0)
    size = largest
    while size >= SUBLANES:
        take = (n // size) & 1

        @pl.when(take == 1)
        def _(off=off, size=size):
            fn(pl.multiple_of(off, SUBLANES), size)

        off = off + take * size
        size //= 2


def _tile_copy(buf_ref, tile, hbm_ref, hbm_tile, sem, to_hbm):
    local = buf_ref.at[pl.ds(pl.multiple_of(tile * SUBLANES, SUBLANES), SUBLANES), :]
    remote = hbm_ref.at[pl.ds(pl.multiple_of(hbm_tile * SUBLANES, SUBLANES), SUBLANES), :]
    return pltpu.make_async_copy(local, remote, sem) if to_hbm else pltpu.make_async_copy(remote, local, sem)


def _start_tiles(count, table_ref, buf_ref, hbm_ref, sem, to_hbm):
    def body(c, carry):
        _tile_copy(buf_ref, c, hbm_ref, table_ref[0, 0, c], sem, to_hbm).start()
        return carry

    lax.fori_loop(0, count, body, 0)


def _wait_tiles(count, buf_ref, hbm_ref, sem, to_hbm):
    def body(c, carry):
        _tile_copy(buf_ref, 0, hbm_ref, 0, sem, to_hbm).wait()
        return carry

    lax.fori_loop(0, count, body, 0)


def _dispatch_kernel(tot_ref, pads_ref, padn_ref, nused_ref, dst_ref, h_ref, rt_ref, xs_hbm,
                     loc_ref, zero_ref, sem, zsem, *, n_blocks):
    w = pl.program_id(0)
    last = pl.num_programs(0) - 1

    @pl.when(w == 0)
    def _():
        zero_ref[...] = jnp.zeros(zero_ref.shape, zero_ref.dtype)

        def fill(act):
            def per_expert(e, carry):
                start = pl.multiple_of(pads_ref[e], SUBLANES)

                def piece(off, size):
                    act(pltpu.make_async_copy(
                        zero_ref.at[pl.ds(0, size), :],
                        xs_hbm.at[pl.ds(pl.multiple_of(start + off, SUBLANES), size), :], zsem))
                _for_each_piece(padn_ref[e], MOE_BLOCK // 2, piece)
                return carry

            def per_block(b, carry):
                first = pl.multiple_of(b * MOE_BLOCK, MOE_BLOCK)
                act(pltpu.make_async_copy(zero_ref, xs_hbm.at[pl.ds(first, MOE_BLOCK), :], zsem))
                return carry

            lax.fori_loop(0, N_EXPERTS, per_expert, 0)
            lax.fori_loop(nused_ref[0], n_blocks, per_block, 0)

        fill(lambda cp: cp.start())
        fill(lambda cp: cp.wait())

    pos1 = rt_ref[4:5, :].astype(jnp.int32)
    pos2 = rt_ref[5:6, :].astype(jnp.int32)
    slot = lax.broadcasted_iota(jnp.int32, (SLOTS, PROJ_ROWS), 0)
    pick = jnp.where((slot == pos1) | (slot == pos2), 1.0, 0.0).astype(BF16)
    buf = loc_ref.at[w % 2]
    rows = pltpu.bitcast(jnp.dot(pick, h_ref[...], preferred_element_type=F32), jnp.uint32)
    buf[...] = rows[:, 0:XS_WORDS] | (rows[:, XS_WORDS:D_MODEL] >> 16)

    @pl.when(w > 0)
    def _():
        _wait_tiles(tot_ref[jnp.maximum(w - 1, 0)] // SUBLANES, buf, xs_hbm, sem, True)

    _start_tiles(tot_ref[w] // SUBLANES, dst_ref, buf, xs_hbm, sem, True)

    @pl.when(w == last)
    def _():
        _wait_tiles(tot_ref[w] // SUBLANES, buf, xs_hbm, sem, True)


def _dispatch(plan, h2, routet, n_blocks):
    T = h2.shape[0]
    tm = PROJ_ROWS
    return pl.pallas_call(
        functools.partial(_dispatch_kernel, n_blocks=n_blocks),
        grid_spec=pltpu.PrefetchScalarGridSpec(
            num_scalar_prefetch=4,
            grid=(T // tm,),
            in_specs=[pl.BlockSpec((1, 1, TILES), lambda w, *_: (w, 0, 0), memory_space=pltpu.SMEM),
                      pl.BlockSpec((tm, D_MODEL), lambda w, *_: (w, 0)),
                      pl.BlockSpec((SUBLANES, tm), lambda w, *_: (0, w))],
            out_specs=pl.BlockSpec(memory_space=pl.ANY),
            scratch_shapes=[pltpu.VMEM((2, SLOTS, XS_WORDS), jnp.uint32),
                            pltpu.VMEM((MOE_BLOCK, XS_WORDS), jnp.uint32),
                            pltpu.SemaphoreType.DMA, pltpu.SemaphoreType.DMA],
        ),
        out_shape=jax.ShapeDtypeStruct((n_blocks * MOE_BLOCK, XS_WORDS), jnp.uint32),
        compiler_params=_params(("arbitrary",)),
        name="moe_dispatch",
    )(plan["tot"], plan["pad_start"], plan["pad_len"], plan["nused"], plan["dst"], h2, routet)


def _expert_kernel(blk_e_ref, nused_ref, x_ref, wg_ref, wu_ref, wd_ref, o_ref):
    b = pl.program_id(0)

    @pl.when(b < nused_ref[0])
    def _():
        words = x_ref[...]
        high = pltpu.bitcast(words & jnp.uint32(0xFFFF0000), F32).astype(BF16)
        low = pltpu.bitcast(words << 16, F32).astype(BF16)
        x = jnp.concatenate([high, low], axis=1)
        gate = jnp.dot(x, wg_ref[0, 0].astype(BF16), preferred_element_type=F32)
        up = jnp.dot(x, wu_ref[0, 0].astype(BF16), preferred_element_type=F32)
        act = (jax.nn.silu(gate) * up).astype(BF16)
        o_ref[...] = jnp.dot(act, wd_ref[0, 0].astype(BF16), preferred_element_type=F32)

    @pl.when(b >= nused_ref[0])
    def _():
        o_ref[...] = jnp.zeros(o_ref.shape, o_ref.dtype)


def _experts(plan, xs, wg, wu, wd, layer, n_blocks):
    return pl.pallas_call(
        _expert_kernel,
        grid_spec=pltpu.PrefetchScalarGridSpec(
            num_scalar_prefetch=2,
            grid=(n_blocks,),
            in_specs=[pl.BlockSpec((MOE_BLOCK, XS_WORDS), lambda b, be, nu: (b, 0)),
                      pl.BlockSpec((1, 1, D_MODEL, EXPERT_FF), lambda b, be, nu: (layer, be[b], 0, 0)),
                      pl.BlockSpec((1, 1, D_MODEL, EXPERT_FF), lambda b, be, nu: (layer, be[b], 0, 0)),
                      pl.BlockSpec((1, 1, EXPERT_FF, D_MODEL), lambda b, be, nu: (layer, be[b], 0, 0))],
            out_specs=pl.BlockSpec((MOE_BLOCK, D_MODEL), lambda b, be, nu: (b, 0)),
        ),
        out_shape=jax.ShapeDtypeStruct((n_blocks * MOE_BLOCK, D_MODEL), F32),
        compiler_params=_params(("arbitrary",)),
        name="moe_experts",
    )(plan["blk_e"], plan["nused"], xs, wg, wu, wd)


def _combine_kernel(tot_ref, src_ref, src_next_ref, yb_hbm, x_ref, route_ref, g_ref, o_ref, loc_ref, sems,
                    *, final_norm):
    w = pl.program_id(0)
    last = pl.num_programs(0) - 1

    def fetch(win, table_ref):
        buf = loc_ref.at[win % 2]
        tiles = tot_ref[win] // SUBLANES

        def clear(r, carry):
            buf[pl.ds(pl.multiple_of(r * SUBLANES, SUBLANES), SUBLANES), :] = jnp.zeros((SUBLANES, D_MODEL), F32)
            return carry

        lax.fori_loop(tiles, TILES, clear, 0)
        _start_tiles(tiles, table_ref, buf, yb_hbm, sems.at[win % 2], False)

    @pl.when(w == 0)
    def _():
        fetch(w, src_ref)

    @pl.when(w < last)
    def _():
        fetch(jnp.minimum(w + 1, last), src_next_ref)

    buf = loc_ref.at[w % 2]
    _wait_tiles(tot_ref[w] // SUBLANES, buf, yb_hbm, sems.at[w % 2], False)
    route = route_ref[...]
    pos1 = route[:, 4:5].astype(jnp.int32)
    pos2 = route[:, 5:6].astype(jnp.int32)
    slot = lax.broadcasted_iota(jnp.int32, (PROJ_ROWS, SLOTS), 1)
    sel = jnp.where(slot == pos1, route[:, 2:3], 0.0) + jnp.where(slot == pos2, route[:, 3:4], 0.0)
    sel_hi = sel.astype(BF16)
    sel_lo = (sel - sel_hi.astype(F32)).astype(BF16)
    y = buf[...]
    y_hi = y.astype(BF16)
    y_lo = (y - y_hi.astype(F32)).astype(BF16)
    moe = (jnp.dot(sel_hi, y_hi, preferred_element_type=F32)
           + jnp.dot(sel_lo, y_hi, preferred_element_type=F32)
           + jnp.dot(sel_hi, y_lo, preferred_element_type=F32))
    x = x_ref[...] + moe
    if final_norm:
        ms = jnp.mean(x * x, axis=-1, keepdims=True)
        x = x * lax.rsqrt(ms + RMS_EPS) * g_ref[...]
    o_ref[...] = x


def _combine(plan, yb, x1, route, g, final_norm):
    T = x1.shape[0]
    tm = PROJ_ROWS
    nt = T // tm
    table = lambda shift: pl.BlockSpec((1, 1, TILES), lambda w, *_: (jnp.minimum(w + shift, nt - 1), 0, 0),
                                       memory_space=pltpu.SMEM)
    return pl.pallas_call(
        functools.partial(_combine_kernel, final_norm=final_norm),
        grid_spec=pltpu.PrefetchScalarGridSpec(
            num_scalar_prefetch=1,
            grid=(nt,),
            in_specs=[table(0), table(1),
                      pl.BlockSpec(memory_space=pl.ANY),
                      pl.BlockSpec((tm, D_MODEL), lambda w, *_: (w, 0)),
                      pl.BlockSpec((tm, LANES), lambda w, *_: (w, 0)),
                      pl.BlockSpec((1, D_MODEL), lambda w, *_: (0, 0))],
            out_specs=pl.BlockSpec((tm, D_MODEL), lambda w, *_: (w, 0)),
            scratch_shapes=[pltpu.VMEM((2, SLOTS, D_MODEL), F32), pltpu.SemaphoreType.DMA((2,))],
        ),
        out_shape=jax.ShapeDtypeStruct((T, D_MODEL), F32),
        compiler_params=_params(("arbitrary",)),
        name="moe_combine",
    )(plan["tot"], plan["dst"], plan["dst"], yb, x1, route, g)


def _routing_plan(wcnt, T):
    n = wcnt[:, 0, N_GROUPS:N_GROUPS + N_EXPERTS].astype(jnp.int32)
    cnt = jnp.sum(n, axis=0)
    nblk = (cnt + MOE_BLOCK - 1) // MOE_BLOCK
    blk_end = jnp.cumsum(nblk)
    first_row = (blk_end - nblk) * MOE_BLOCK
    ls = jnp.cumsum(n, axis=1) - n
    gs = first_row[None, :] + jnp.cumsum(n, axis=0) - n
    worst_rows = T * 2 + (T // PROJ_ROWS) * N_EXPERTS * SUBLANES
    n_blocks = -(-worst_rows // MOE_BLOCK) + N_EXPERTS
    blk_e = jnp.minimum(jnp.sum(jnp.arange(n_blocks)[:, None] >= blk_end[None, :], axis=1), N_EXPERTS - 1)
    tile = jnp.arange(TILES)[None, :, None]
    lt, nt8, gt = (a[:, None, :] // SUBLANES for a in (ls, n, gs))
    dst = jnp.sum(jnp.where((tile >= lt) & (tile < lt + nt8), gt + tile - lt, 0), axis=2)
    i32 = lambda a: a.reshape(-1).astype(jnp.int32)
    plan = dict(tot=i32(jnp.sum(n, axis=1)), pad_start=i32(first_row + cnt), pad_len=i32(nblk * MOE_BLOCK - cnt),
                nused=i32(blk_end[-1:]), blk_e=i32(blk_e),
                dst=dst.astype(jnp.int32).reshape(n.shape[0], 1, TILES))
    return plan, n_blocks


def _pack_w_in(w):
    scale = QK_DIM ** -0.5 * LOG2E
    q1, q2, k1, k2 = (w[:, i * QK_COLS:(i + 1) * QK_COLS].reshape(D_MODEL, N_HEADS, QK_DIM) for i in range(4))
    qq = (jnp.concatenate([q1, q2], axis=-1) * scale).reshape(D_MODEL, ATTN_WIDTH)
    kk = jnp.concatenate([k1, k2], axis=-1).reshape(D_MODEL, ATTN_WIDTH)
    v0 = 4 * QK_COLS
    packed = jnp.concatenate([qq, kk, w[:, v0 + ATTN_WIDTH:]], axis=1).astype(BF16)
    return packed, jnp.transpose(w[:, v0:v0 + ATTN_WIDTH]).astype(BF16)


def kernel(x, rel_bias, ln1_g, w_in, lam_q1, lam_k1, lam_q2, lam_k2, subln_g, conv_w, conv_b, conv_ln_g, conv_ln_b, ssm_lam_re, ssm_lam_im, ssm_log_dt, ssm_b_re, ssm_b_im, ssm_c_re, ssm_c_im, ssm_d, ssm_glu_w, ssm_glu_b, w_out, ln2_g, group_router_w, group_router_b, expert_router_w, expert_router_b, w_gate, w_up, w_down, final_g):
    B, L, D = x.shape
    T = B * L
    depth = w_in.shape[0]
    assert D == D_MODEL and L % CONV_ROWS == 0 and L % ATTN_TILE == 0 and L % SSM_ROWS == 0
    assert PROJ_ROWS == ATTN_TILE and T % PROJ_ROWS == 0 and (2 * T) % MOE_BLOCK == 0
    x2 = x.reshape(T, D)
    bias_diag, bias_sub = _bias_tiles(rel_bias)
    row = lambda v: v.astype(F32).reshape(1, -1)
    for l in range(depth):
        lam_init = 0.8 - 0.6 * math.exp(-0.3 * l)
        lam = (jnp.exp(jnp.sum(lam_q1[l].astype(F32) * lam_k1[l].astype(F32)))
               - jnp.exp(jnp.sum(lam_q2[l].astype(F32) * lam_k2[l].astype(F32))) + lam_init).reshape(1)
        qq, kk, conv_in, ssm_in, vt = _inproj(x2, row(ln1_g[l]), *_pack_w_in(w_in[l]))
        a = _attention(qq, kk, vt, lam, bias_diag, bias_sub, subln_g[l].astype(F32).reshape(V_DIM, 1),
                       1.0 - lam_init, B, L)
        c = _conformer_conv(conv_in, conv_w[l], row(conv_b[l]), row(conv_ln_g[l]), row(conv_ln_b[l]), B, L)
        wb, apr, api, wc = _ssm_weights(ssm_lam_re[l], ssm_lam_im[l], ssm_log_dt[l], ssm_b_re[l], ssm_b_im[l],
                                        ssm_c_re[l], ssm_c_im[l])
        s = _s5_ssm(ssm_in, wb, apr, api, wc, row(ssm_d[l]), ssm_glu_w[l].astype(BF16), row(ssm_glu_b[l]), B, L)
        wr = jnp.zeros((D, LANES), F32).at[:, :N_GROUPS].set(group_router_w[l]) \
            .at[:, N_GROUPS:N_GROUPS + N_EXPERTS].set(expert_router_w[l]).astype(BF16)
        br = jnp.zeros((1, LANES), F32).at[0, :N_GROUPS].set(group_router_b[l]) \
            .at[0, N_GROUPS:N_GROUPS + N_EXPERTS].set(expert_router_b[l])
        x1, h2, route, routet, wcnt = _outproj_route(x2, a, c, s, w_out[l].astype(BF16), row(ln2_g[l]), wr, br)
        plan, n_blocks = _routing_plan(wcnt, T)
        xs = _dispatch(plan, h2, routet, n_blocks)
        yb = _experts(plan, xs, w_gate, w_up, w_down, l, n_blocks)
        x2 = _combine(plan, yb, x1, route, row(final_g), final_norm=(l == depth - 1))
    return x2.reshape(B, L, D)
```
